```python
import functools
import jax
import jax.numpy as jnp
from jax import lax
import numpy as np

D_MODEL = 1024
BATCH = 8
SEQ = 2048
DEPTH = 1
DEC_BATCH = 128
DEC_SEQ = 1
PAST_LEN = 8192
PAGE_SIZE = 128

D_RNN = D_MODEL
RG_BLOCKS = 8
RG_BW = D_RNN // RG_BLOCKS
RG_C = 8.0
CONV_W = 4
N_HEADS = 8
N_KV_HEADS = 2
GROUP = N_HEADS // N_KV_HEADS
HEAD_DIM = D_MODEL // N_HEADS
NSA_W = N_HEADS * HEAD_DIM
KV_W = 2 * N_KV_HEADS * HEAD_DIM
CMP_BLOCK = 32
CMP_STRIDE = 16
CMP_RATIO = CMP_BLOCK // CMP_STRIDE
CMP_HIDDEN = HEAD_DIM
SEL_BLOCK = 64
TOP_N = 16
N_LOCAL_BLOCKS = 2
WINDOW = 512
Q_BLOCK = 128
EPS = 1e-6
NEG = -1e30
FORCE = 1e4
IN_SIZES = (D_RNN, D_RNN, NSA_W, KV_W, KV_W, KV_W, 3 * N_HEADS, NSA_W, D_MODEL, D_MODEL)
IN_COLS = sum(IN_SIZES)
IN_OFFSETS = tuple(int(v) for v in np.cumsum(IN_SIZES)[:-1])

kernel_name = "hawk_nsa_gated_hybrid_step"


def _rms(x, g):
    xf = x.astype(jnp.float32)
    y = xf * lax.rsqrt(jnp.mean(xf * xf, axis=-1, keepdims=True) + EPS)
    return (y * g.astype(jnp.float32)).astype(x.dtype)


def _kv_norm(kv, g):
    return jnp.stack([_rms(kv[:, :, 0], g), kv[:, :, 1]], axis=2)


def _masked_softmax(s, mask):
    s = jnp.where(mask, s, NEG)
    p = jnp.exp(s - jnp.max(s, axis=-1, keepdims=True)) * mask.astype(jnp.float32)
    return p / jnp.maximum(jnp.sum(p, axis=-1, keepdims=True), 1e-30)


def _project(x, c, w):
    n, l = x.shape[:2]
    mod = jnp.dot(jax.nn.silu(c), w['w_ada']) + w['b_ada']
    shift, scale, gate = jnp.split(mod[:, None, :], 3, axis=-1)
    h = _rms(x, w['norm_g']) * (1.0 + scale) + shift
    cols = jnp.dot(h, w['w_in'])
    xr, zr, q, kc, ks, kw, bg, zn, ga, gb = jnp.split(cols, IN_OFFSETS, axis=-1)
    q = _rms(q.reshape(n, l, N_HEADS, HEAD_DIM), w['q_norm_g']) * (HEAD_DIM ** -0.5)
    q = q.reshape(n, l, N_KV_HEADS, GROUP, HEAD_DIM)
    kc = kc.reshape(n, l, 2, N_KV_HEADS, HEAD_DIM)
    ks = _kv_norm(ks.reshape(n, l, 2, N_KV_HEADS, HEAD_DIM), w['k_norm_g'][1])
    kw = _kv_norm(kw.reshape(n, l, 2, N_KV_HEADS, HEAD_DIM), w['k_norm_g'][2])
    return dict(xr=xr, zr=zr, q=q, kc=kc, ks=ks, kw=kw, bg=bg, zn=zn, ga=ga, gb=gb, gate=gate)


def _lin_combine(left, right):
    a1, b1 = left
    a2, b2 = right
    return a1 * a2, a2 * b1 + b2


def _rglru_branch(xr, conv_buf, h0, w):
    n, l, _ = xr.shape
    ext = jnp.concatenate([conv_buf.astype(xr.dtype), xr], axis=1)
    xc = ext[:, 0:l] * w['conv_w'][0] + w['conv_b']
    for k in range(1, CONV_W):
        xc = xc + ext[:, k:k + l] * w['conv_w'][k]
    xb = xc.reshape(n, l, RG_BLOCKS, RG_BW)
    r = jax.nn.sigmoid((jnp.einsum('nlkc,kcd->nlkd', xb, w['rg_wa']).reshape(n, l, D_RNN) + w['rg_ba']).astype(jnp.float32))
    i = jax.nn.sigmoid((jnp.einsum('nlkc,kcd->nlkd', xb, w['rg_wx']).reshape(n, l, D_RNN) + w['rg_bx']).astype(jnp.float32))
    log_a = -RG_C * r * jax.nn.softplus(-w['rg_lambda'].astype(jnp.float32))
    a = jnp.exp(log_a)
    b = jnp.sqrt(-jnp.expm1(2.0 * log_a)) * i * xc.astype(jnp.float32)
    a_cum, b_cum = lax.associative_scan(_lin_combine, (a, b), axis=1)
    h = a_cum * h0.astype(jnp.float32)[:, None, :] + b_cum
    return h.astype(xr.dtype), h[:, -1].astype(xr.dtype), ext[:, l:]


def _overlap(nc, nsel):
    start = jnp.arange(nc) * CMP_STRIDE
    sstart = jnp.arange(nsel) * SEL_BLOCK
    return ((start[:, None] <= sstart[None, :] + SEL_BLOCK - 1)
            & (start[:, None] + CMP_BLOCK - 1 >= sstart[None, :])).astype(jnp.float32)


def _cmp_partials(rows, pool_w):
    n, l = rows.shape[:2]
    ch = rows.reshape(n, l // CMP_STRIDE, CMP_STRIDE, 2, N_KV_HEADS, HEAD_DIM)
    wr = jax.nn.softmax(pool_w.astype(jnp.float32), axis=-1).astype(rows.dtype).reshape(2, CMP_RATIO, CMP_STRIDE)
    return jnp.einsum('ncjehd,erj->nrcehd', ch, wr)


def _compress(parts, w):
    nc = parts.shape[2] - CMP_RATIO + 1
    pooled = parts[:, 0, :nc]
    for r in range(1, CMP_RATIO):
        pooled = pooled + parts[:, r, r:r + nc]
    hid = jax.nn.silu(jnp.einsum('ncehd,edf->ncehf', pooled, w['cmp_w1']))
    comp = pooled + jnp.einsum('ncehf,efd->ncehd', hid, w['cmp_w2'])
    return _rms(comp[:, :, 0], w['k_norm_g'][0]), comp[:, :, 1]


def _cmp_and_select(q, t, ck, cv, ends, overlap, n_top):
    s = jnp.einsum('qhgd,nhd->qhgn', q, ck).astype(jnp.float32)
    p = _masked_softmax(s, (ends[None, :] <= t[:, None])[:, None, None, :])
    o = jnp.einsum('qhgn,nhd->qhgd', p.astype(cv.dtype), cv)
    imp = jnp.einsum('qhgn,nj->qhj', p, overlap)
    blk = jnp.arange(overlap.shape[1])[None, :]
    cur = (t // SEL_BLOCK)[:, None]
    valid = blk <= cur
    forced = (blk == 0) | (valid & (blk > cur - N_LOCAL_BLOCKS))
    score = jnp.where(valid[:, None, :], imp + FORCE * forced[:, None, :].astype(jnp.float32), NEG)
    top, idx = lax.top_k(score, n_top)
    return o, idx, top > 0.5 * NEG


def _sel_attend(q, kvg, idx, ok, t):
    nq, nh, g = q.shape[:3]
    pos = idx[..., None] * SEL_BLOCK + jnp.arange(SEL_BLOCK)
    mask = (ok[..., None] & (pos <= t[:, None, None, None])).reshape(nq, nh, 1, -1)
    s = jnp.einsum('qhgd,qhnjd->qhgnj', q, kvg[..., 0, :]).astype(jnp.float32).reshape(nq, nh, g, -1)
    p = _masked_softmax(s, mask)
    v = kvg[..., 1, :].reshape(nq, nh, -1, HEAD_DIM)
    return jnp.einsum('qhgm,qhmd->qhgd', p.astype(v.dtype), v)


def _win_attend(q, kv, pos, t):
    s = jnp.einsum('qhgd,lhd->qhgl', q, kv[:, 0]).astype(jnp.float32)
    d = t[:, None] - pos[None, :]
    mask = (d >= 0) & (d < WINDOW) & (pos[None, :] >= 0)
    p = _masked_softmax(s, mask[:, None, None, :])
    return jnp.einsum('qhgl,lhd->qhgd', p.astype(kv.dtype), kv[:, 1])


def _nsa_prompt(q, kc, ks, kw, w):
    b_sz, s_len = q.shape[:2]
    ck, cv = _compress(_cmp_partials(kc, w['cmp_pool_w']), w)
    nc = ck.shape[1]
    ends = jnp.arange(nc) * CMP_STRIDE + (CMP_BLOCK - 1)
    nsel = s_len // SEL_BLOCK
    overlap = _overlap(nc, nsel)
    select = functools.partial(_cmp_and_select, n_top=min(TOP_N, nsel))
    kw_pad = jnp.pad(kw, ((0, 0), (WINDOW, 0), (0, 0), (0, 0), (0, 0)))
    nqb = s_len // Q_BLOCK
    heads = jnp.arange(N_KV_HEADS)[None, :, None, None]

    def one_block(item):
        b = item // nqb
        start = (item % nqb) * Q_BLOCK
        qb = lax.dynamic_slice_in_dim(q[b], start, Q_BLOCK, axis=0)
        t = start + jnp.arange(Q_BLOCK)
        o_c, idx, ok = select(qb, t, ck[b], cv[b], ends, overlap)
        rows = idx[..., None] * SEL_BLOCK + jnp.arange(SEL_BLOCK)
        o_s = _sel_attend(qb, ks[b, rows, :, heads], idx, ok, t)
        kvw = lax.dynamic_slice_in_dim(kw_pad[b], start, WINDOW + Q_BLOCK, axis=0)
        pos = start - WINDOW + jnp.arange(WINDOW + Q_BLOCK)
        o_w = _win_attend(qb, kvw, pos, t)
        return jnp.stack([o_c, o_s, o_w], axis=0)

    out = lax.map(one_block, jnp.arange(b_sz * nqb))
    out = out.reshape(b_sz, nqb, 3, Q_BLOCK, N_HEADS, HEAD_DIM).transpose(2, 0, 1, 3, 4, 5)
    return out.reshape(3, b_sz, s_len, N_HEADS, HEAD_DIM)


def _nsa_sample(q, kc, ks, kw, cache_cmp, cache_sel, win_buf, page_table, layer, w):
    db, l = q.shape[:2]
    past = page_table.shape[1] * PAGE_SIZE
    t = past + jnp.arange(l)
    past_rows = cache_cmp[layer, page_table].reshape(db, past, 2, N_KV_HEADS, HEAD_DIM).astype(kc.dtype)
    pad_c = -(-l // CMP_STRIDE) * CMP_STRIDE - l
    new_rows = jnp.pad(kc, ((0, 0), (0, pad_c), (0, 0), (0, 0), (0, 0)))
    parts = jnp.concatenate([_cmp_partials(past_rows, w['cmp_pool_w']),
                             _cmp_partials(new_rows, w['cmp_pool_w'])], axis=2)
    ck, cv = _compress(parts, w)
    nc = ck.shape[1]
    ends = jnp.arange(nc) * CMP_STRIDE + (CMP_BLOCK - 1)
    n_past_blk = past // SEL_BLOCK
    n_new_blk = -(-l // SEL_BLOCK)
    nsel = n_past_blk + n_new_blk
    overlap = _overlap(nc, nsel)
    select = functools.partial(_cmp_and_select, n_top=min(TOP_N, nsel))
    o_c, idx, ok = jax.vmap(select, in_axes=(0, None, 0, 0, None, None))(q, t, ck, cv, ends, overlap)
    bi = jnp.arange(db)[:, None, None, None, None]
    hi = jnp.arange(N_KV_HEADS)[None, None, :, None, None]
    offs = jnp.arange(SEL_BLOCK)
    pstart = jnp.minimum(idx, n_past_blk - 1) * SEL_BLOCK
    phys = page_table[bi[..., 0], pstart // PAGE_SIZE]
    g_past = cache_sel[layer, phys[..., None], (pstart % PAGE_SIZE)[..., None] + offs, :, hi]
    new_sel = jnp.pad(ks, ((0, 0), (0, n_new_blk * SEL_BLOCK - l), (0, 0), (0, 0), (0, 0)))
    nrow = jnp.clip(idx - n_past_blk, 0, n_new_blk - 1)[..., None] * SEL_BLOCK + offs
    g_new = new_sel[bi, nrow, :, hi]
    kvg = jnp.where((idx < n_past_blk)[..., None, None, None], g_past.astype(ks.dtype), g_new)
    o_s = jax.vmap(_sel_attend, in_axes=(0, 0, 0, 0, None))(q, kvg, idx, ok, t)
    wb = win_buf.shape[1]
    win_all = jnp.concatenate([win_buf.astype(kw.dtype), kw], axis=1)
    pos = past - wb + jnp.arange(wb + l)
    o_w = jax.vmap(_win_attend, in_axes=(0, 0, None, None))(q, win_all, pos, t)
    o3 = jnp.stack([o_c, o_s, o_w], axis=0).reshape(3, db, l, N_HEADS, HEAD_DIM)
    return o3, win_all[:, l:]


def _merge(x, pr, ra, o3, w):
    n, l = x.shape[:2]
    g = jax.nn.sigmoid(pr['bg'].reshape(n, l, 3, N_HEADS).astype(jnp.float32)).astype(x.dtype)
    o_nsa = jnp.einsum('rnlhd,nlrh->nlhd', o3, g).reshape(n, l, NSA_W) * jax.nn.silu(pr['zn'])
    o_rnn = ra * jax.nn.silu(pr['zr'])
    u = jax.nn.sigmoid(pr['ga']) * o_rnn + jax.nn.sigmoid(pr['gb']) * o_nsa
    return x + pr['gate'] * jnp.dot(u, w['w_out'])


def _layer(xp, xs, cp, cs, conv_buf, h_state, cache_cmp, cache_sel, win_buf, page_table, layer, w):
    b_sz, s_len = xp.shape[:2]
    pp = _project(xp, cp, w)
    ps = _project(xs, cs, w)
    ra_p, h_p, conv_p = _rglru_branch(pp['xr'], jnp.zeros((b_sz, CONV_W - 1, D_RNN), xp.dtype),
                                      jnp.zeros((b_sz, D_RNN), jnp.float32), w)
    ra_s, h_s, conv_s = _rglru_branch(ps['xr'], conv_buf, h_state, w)
    o3p = _nsa_prompt(pp['q'], pp['kc'], pp['ks'], pp['kw'], w)
    o3s, win_s = _nsa_sample(ps['q'], ps['kc'], ps['ks'], ps['kw'], cache_cmp, cache_sel, win_buf, page_table, layer, w)
    yp = _merge(xp, pp, ra_p, o3p, w)
    ys = _merge(xs, ps, ra_s, o3s, w)
    win_p = pp['kw'][:, s_len - min(WINDOW, s_len):]
    return yp, ys, (conv_p, conv_s, h_p, h_s, pp['kc'], ps['kc'], pp['ks'], ps['ks'], win_p, win_s)


def setup_inputs(seed: int = 0) -> dict:
    key = jax.random.key(seed)
    k = jax.random.split(key, 32)
    f32 = jnp.float32
    n_pages = PAST_LEN // PAGE_SIZE
    n_phys = (DEC_BATCH * n_pages * 5) // 4
    win_buf = min(WINDOW, PAST_LEN)

    def nrm(i, shape, scale=1.0):
        return jax.random.normal(k[i], shape, f32) * scale

    sig = jax.random.uniform(k[20], (DEPTH, D_RNN), f32, 0.9, 0.999) ** (1.0 / RG_C)
    page_table = jax.random.permutation(k[21], n_phys)[:DEC_BATCH * n_pages].reshape(DEC_BATCH, n_pages).astype(jnp.int32)
    return {
        'x_prompt': nrm(0, (BATCH, SEQ, D_MODEL)),
        'x_sample': nrm(1, (DEC_BATCH, DEC_SEQ, D_MODEL)),
        'c_prompt': nrm(2, (BATCH, D_MODEL)),
        'c_sample': nrm(3, (DEC_BATCH, D_MODEL)),
        'state_conv': nrm(4, (DEPTH, DEC_BATCH, CONV_W - 1, D_RNN)),
        'state_rglru': nrm(5, (DEPTH, DEC_BATCH, D_RNN), 0.5),
        'cache_cmp_kv': nrm(6, (DEPTH, n_phys, PAGE_SIZE, 2, N_KV_HEADS, HEAD_DIM)),
        'cache_sel_kv': nrm(7, (DEPTH, n_phys, PAGE_SIZE, 2, N_KV_HEADS, HEAD_DIM)),
        'state_win_kv': nrm(8, (DEPTH, DEC_BATCH, win_buf, 2, N_KV_HEADS, HEAD_DIM)),
        'page_table': page_table,
        'norm_g': 1.0 + nrm(9, (DEPTH, D_MODEL), 0.02),
        'w_ada': nrm(10, (DEPTH, D_MODEL, 3 * D_MODEL), 0.5 * D_MODEL ** -0.5),
        'b_ada': nrm(11, (DEPTH, 3 * D_MODEL), 0.01),
        'w_in': nrm(12, (DEPTH, D_MODEL, IN_COLS), D_MODEL ** -0.5),
        'conv_w': nrm(13, (DEPTH, CONV_W, D_RNN), CONV_W ** -0.5),
        'conv_b': nrm(14, (DEPTH, D_RNN), 0.01),
        'rg_wa': nrm(15, (DEPTH, RG_BLOCKS, RG_BW, RG_BW), RG_BW ** -0.5),
        'rg_ba': nrm(16, (DEPTH, D_RNN), 0.01),
        'rg_wx': nrm(17, (DEPTH, RG_BLOCKS, RG_BW, RG_BW), RG_BW ** -0.5),
        'rg_bx': nrm(18, (DEPTH, D_RNN), 0.01),
        'rg_lambda': jnp.log(sig) - jnp.log1p(-sig),
        'q_norm_g': 1.0 + nrm(19, (DEPTH, HEAD_DIM), 0.02),
        'k_norm_g': 1.0 + nrm(22, (DEPTH, 3, HEAD_DIM), 0.02),
        'cmp_pool_w': nrm(23, (DEPTH, 2, CMP_BLOCK), 0.1),
        'cmp_w1': nrm(24, (DEPTH, 2, HEAD_DIM, CMP_HIDDEN), HEAD_DIM ** -0.5),
        'cmp_w2': nrm(25, (DEPTH, 2, CMP_HIDDEN, HEAD_DIM), 0.5 * CMP_HIDDEN ** -0.5),
        'w_out': nrm(26, (DEPTH, D_MODEL, D_MODEL), D_MODEL ** -0.5),
    }


def reference(x_prompt, x_sample, c_prompt, c_sample, state_conv, state_rglru, cache_cmp_kv, cache_sel_kv,
              state_win_kv, page_table, norm_g, w_ada, b_ada, w_in, conv_w, conv_b, rg_wa, rg_ba, rg_wx, rg_bx,
              rg_lambda, q_norm_g, k_norm_g, cmp_pool_w, cmp_w1, cmp_w2, w_out):
    yp, ys = x_prompt, x_sample
    new = [[] for _ in range(10)]
    for layer in range(DEPTH):
        w = dict(norm_g=norm_g[layer], w_ada=w_ada[layer], b_ada=b_ada[layer], w_in=w_in[layer],
                 conv_w=conv_w[layer], conv_b=conv_b[layer], rg_wa=rg_wa[layer], rg_ba=rg_ba[layer],
                 rg_wx=rg_wx[layer], rg_bx=rg_bx[layer], rg_lambda=rg_lambda[layer], q_norm_g=q_norm_g[layer],
                 k_norm_g=k_norm_g[layer], cmp_pool_w=cmp_pool_w[layer], cmp_w1=cmp_w1[layer],
                 cmp_w2=cmp_w2[layer], w_out=w_out[layer])
        yp, ys, st = _layer(yp, ys, c_prompt, c_sample, state_conv[layer], state_rglru[layer], cache_cmp_kv,
                            cache_sel_kv, state_win_kv[layer], page_table, layer, w)
        for lst, v in zip(new, st):
            lst.append(v)
    conv_prompt = jnp.stack(new[0])
    conv_sample = jnp.stack(new[1])
    rglru_prompt = jnp.stack(new[2])
    rglru_sample = jnp.stack(new[3])
    cmp_kv_prompt = jnp.stack(new[4])
    cmp_kv_sample = jnp.stack(new[5])
    sel_kv_prompt = jnp.stack(new[6])
    sel_kv_sample = jnp.stack(new[7])
    win_kv_prompt = jnp.stack(new[8])
    win_kv_sample = jnp.stack(new[9])
    return (yp, ys, conv_prompt, conv_sample, rglru_prompt, rglru_sample, cmp_kv_prompt, cmp_kv_sample,
            sel_kv_prompt, sel_kv_sample, win_kv_prompt, win_kv_sample)
```

```python
import functools

import jax
import jax.numpy as jnp
from jax import lax
from jax.experimental import pallas as pl
from jax.experimental.pallas import tpu as pltpu

F32 = jnp.float32
BF16 = jnp.bfloat16

D_MODEL = 1024
RG_BLOCKS = 8
RG_BW = D_MODEL // RG_BLOCKS
RG_C = 8.0
CONV_W = 4
N_HEADS = 8
N_KV_HEADS = 2
GROUP = N_HEADS // N_KV_HEADS
HEAD_DIM = D_MODEL // N_HEADS
KV_W = 2 * N_KV_HEADS * HEAD_DIM
CMP_BLOCK = 32
CMP_STRIDE = 16
SEL_BLOCK = 64
TOP_N = 16
N_LOCAL_BLOCKS = 2
WINDOW = 512
Q_BLOCK = 128
PAGE_SIZE = 128
EPS = 1e-6
NEG = -1e30
FORCE = 1e4

C_XR, C_ZR, C_Q, C_KC, C_KS, C_KW, C_ZN, C_GA, C_GB, C_BG, C_END = (
    0, 1024, 2048, 3072, 3584, 4096, 4608, 5632, 6656, 7680, 7808)
VMEM_LIMIT = 56 * 1024 * 1024
PAGES_PER_STEP = 16


SEL_SHIFT = SEL_BLOCK.bit_length() - 1
HEAD_SHIFT = HEAD_DIM.bit_length() - 1


def _shr(x, k):
    return lax.shift_right_arithmetic(x, jnp.int32(k))


def _sigmoid(x):
    return jax.nn.sigmoid(x)


def _dot(a, b):
    return jnp.dot(a, b, preferred_element_type=F32)


def _dot_nt(a, b):
    return lax.dot_general(a, b, (((1,), (1,)), ((), ())), preferred_element_type=F32)


def _rms_rows(x, g):
    return x * lax.rsqrt(jnp.mean(x * x, axis=-1, keepdims=True) + EPS) * g


def _mod_kernel(c_ref, w_ref, b_ref, o_ref):
    c = c_ref[...]
    s = c * _sigmoid(c)
    o_ref[...] = _dot(s.astype(BF16), w_ref[...].astype(BF16)) + b_ref[...]


def _modulation(c_all, w_ada, b_ada):
    n = c_all.shape[0]
    return pl.pallas_call(
        _mod_kernel,
        grid=(3,),
        in_specs=[pl.BlockSpec((n, D_MODEL), lambda j: (0, 0)),
                  pl.BlockSpec((D_MODEL, D_MODEL), lambda j: (0, j)),
                  pl.BlockSpec((1, D_MODEL), lambda j: (0, j))],
        out_specs=pl.BlockSpec((n, D_MODEL), lambda j: (0, j)),
        out_shape=jax.ShapeDtypeStruct((n, 3 * D_MODEL), F32),
        name="adaln_mod",
    )(c_all, w_ada, b_ada.reshape(1, 3 * D_MODEL))


def _proj_kernel(x_ref, shift_ref, scale_ref, ng_ref, w_ref, qg_ref, kg_ref,
                 xr_ref, a_ref, b_ref, q_ref, kc_ref, ks_ref, kw_ref, ksb_ref, kwb_ref, g_ref):
    x = x_ref[...]
    h = _rms_rows(x, ng_ref[...])
    h = h * (1.0 + scale_ref[0]) + shift_ref[0]
    hb = h.astype(BF16)

    def mm(lo, hi):
        return _dot(hb, w_ref[:, lo:hi])

    xr_ref[...] = mm(C_XR, C_ZR)
    zr = mm(C_ZR, C_Q)
    ga = mm(C_GA, C_GB)
    a_ref[...] = _sigmoid(ga) * (zr * _sigmoid(zr))
    zn = mm(C_ZN, C_GA)
    gb = mm(C_GB, C_BG)
    b_ref[...] = _sigmoid(gb) * (zn * _sigmoid(zn))
    q = mm(C_Q, C_KC)
    for hd in range(N_HEADS):
        sl = slice(hd * HEAD_DIM, (hd + 1) * HEAD_DIM)
        q_ref[:, sl] = (_rms_rows(q[:, sl], qg_ref[...]) * (HEAD_DIM ** -0.5)).astype(BF16)
    kc_ref[...] = mm(C_KC, C_KS)
    for lo, hi, o_ref, ob_ref, gi in ((C_KS, C_KW, ks_ref, ksb_ref, 1), (C_KW, C_ZN, kw_ref, kwb_ref, 2)):
        kv = mm(lo, hi)
        for hd in range(N_KV_HEADS):
            sl = slice(hd * HEAD_DIM, (hd + 1) * HEAD_DIM)
            kn = _rms_rows(kv[:, sl], kg_ref[gi:gi + 1, :])
            o_ref[:, sl] = kn
            ob_ref[:, sl] = kn.astype(BF16)
        vsl = slice(N_KV_HEADS * HEAD_DIM, KV_W)
        o_ref[:, vsl] = kv[:, vsl]
        ob_ref[:, vsl] = kv[:, vsl].astype(BF16)
    g_ref[...] = _sigmoid(mm(C_BG, C_END))


def _project(x2d, shift, scale, norm_g, w_cat, q_norm_g, k_norm_g, tm, rows_per_mod):
    m = x2d.shape[0]
    tiles_per_mod = rows_per_mod // tm
    mod_rows = shift.shape[1]
    mod_spec = pl.BlockSpec((1, mod_rows, D_MODEL), lambda i: (i // tiles_per_mod, 0, 0))

    def row_spec(width):
        return pl.BlockSpec((tm, width), lambda i: (i, 0))

    def const_spec(shape):
        return pl.BlockSpec(shape, lambda i: (0,) * len(shape))

    widths = (D_MODEL, D_MODEL, D_MODEL, D_MODEL, KV_W, KV_W, KV_W, KV_W, KV_W, 128)
    dtypes = (F32, F32, F32, BF16, F32, F32, F32, BF16, BF16, F32)
    return pl.pallas_call(
        _proj_kernel,
        grid=(m // tm,),
        in_specs=[row_spec(D_MODEL), mod_spec, mod_spec, const_spec((1, D_MODEL)),
                  pl.BlockSpec((D_MODEL, C_END), lambda i: (0, 0), pipeline_mode=pl.Buffered(1)),
                  const_spec((1, HEAD_DIM)), const_spec((3, HEAD_DIM))],
        out_specs=[row_spec(w) for w in widths],
        out_shape=[jax.ShapeDtypeStruct((m, w), dt) for w, dt in zip(widths, dtypes)],
        compiler_params=pltpu.CompilerParams(vmem_limit_bytes=VMEM_LIMIT),
        name="in_proj",
    )(x2d, shift, scale, norm_g.reshape(1, D_MODEL), w_cat, q_norm_g.reshape(1, HEAD_DIM), k_norm_g)


def _softplus(z):
    return jnp.maximum(z, 0.0) + jnp.log1p(jnp.exp(-jnp.abs(z)))


def _rglru_coeffs(xc, wa_ref, ba_ref, wx_ref, bx_ref, lam_ref, a_out, b_out):
    xcb = xc.astype(BF16)
    sp = _softplus(-lam_ref[...])
    for k in range(RG_BLOCKS):
        sl = slice(k * RG_BW, (k + 1) * RG_BW)
        r = _sigmoid(_dot(xcb[:, sl], wa_ref[k]) + ba_ref[:, sl])
        i = _sigmoid(_dot(xcb[:, sl], wx_ref[k]) + bx_ref[:, sl])
        log_a = -RG_C * r * sp[:, sl]
        a = jnp.exp(log_a)
        a_out[:, sl] = a
        b_out[:, sl] = jnp.sqrt(-jnp.tanh(log_a) * (a * a + 1.0)) * i * xc[:, sl]


def _rglru_prompt_kernel(xr_ref, cw_ref, cb_ref, wa_ref, ba_ref, wx_ref, bx_ref, lam_ref,
                         ra_ref, hl_ref, ext_s, a_s, b_s, h_s):
    t_len = xr_ref.shape[0]

    @pl.when(pl.program_id(1) == 0)
    def _():
        ext_s[0:8, :] = jnp.zeros((8, D_MODEL), F32)
        h_s[...] = jnp.zeros((1, D_MODEL), F32)

    x = xr_ref[...]
    ext_s[8:8 + t_len, :] = x
    xc = ext_s[pl.ds(5, t_len), :] * cw_ref[0:1, :] + cb_ref[...]
    xc = xc + ext_s[pl.ds(6, t_len), :] * cw_ref[1:2, :]
    xc = xc + ext_s[pl.ds(7, t_len), :] * cw_ref[2:3, :]
    xc = xc + x * cw_ref[3:4, :]
    ext_s[0:8, :] = x[t_len - 8:t_len, :]
    _rglru_coeffs(xc, wa_ref, ba_ref, wx_ref, bx_ref, lam_ref, a_s, b_s)

    def body(t, h):
        h = a_s[pl.ds(t, 1), :] * h + b_s[pl.ds(t, 1), :]
        ra_ref[pl.ds(t, 1), :] = h
        return h

    h = lax.fori_loop(0, t_len, body, h_s[...], unroll=8)
    h_s[...] = h
    hl_ref[...] = h


def _rglru_prompt(xr, bsz, seq, cw, cb, wa, ba, wx, bx, lam, t_chunk=512):
    nchunk = seq // t_chunk

    def const_spec(shape):
        return pl.BlockSpec(shape, lambda b, c: (0,) * len(shape))

    return pl.pallas_call(
        _rglru_prompt_kernel,
        grid=(bsz, nchunk),
        in_specs=[pl.BlockSpec((t_chunk, D_MODEL), lambda b, c: (b * nchunk + c, 0)),
                  const_spec((CONV_W, D_MODEL)), const_spec((1, D_MODEL)),
                  const_spec((RG_BLOCKS, RG_BW, RG_BW)), const_spec((1, D_MODEL)),
                  const_spec((RG_BLOCKS, RG_BW, RG_BW)), const_spec((1, D_MODEL)),
                  const_spec((1, D_MODEL))],
        out_specs=[pl.BlockSpec((t_chunk, D_MODEL), lambda b, c: (b * nchunk + c, 0)),
                   pl.BlockSpec((None, 1, D_MODEL), lambda b, c: (b, 0, 0))],
        out_shape=[jax.ShapeDtypeStruct((bsz * seq, D_MODEL), F32),
                   jax.ShapeDtypeStruct((bsz, 1, D_MODEL), F32)],
        scratch_shapes=[pltpu.VMEM((t_chunk + 8, D_MODEL), F32), pltpu.VMEM((t_chunk, D_MODEL), F32),
                        pltpu.VMEM((t_chunk, D_MODEL), F32), pltpu.VMEM((1, D_MODEL), F32)],
        compiler_params=pltpu.CompilerParams(vmem_limit_bytes=VMEM_LIMIT),
        name="rglru_prompt",
    )(xr, cw, cb, wa, ba, wx, bx, lam)


def _rglru_step_kernel(xr_ref, cbuf_ref, h0_ref, cw_ref, cb_ref, wa_ref, ba_ref, wx_ref, bx_ref, lam_ref,
                       h_ref, a_s, b_s):
    xc = cbuf_ref[:, 0:D_MODEL] * cw_ref[0:1, :] + cb_ref[...]
    xc = xc + cbuf_ref[:, D_MODEL:2 * D_MODEL] * cw_ref[1:2, :]
    xc = xc + cbuf_ref[:, 2 * D_MODEL:3 * D_MODEL] * cw_ref[2:3, :]
    xc = xc + xr_ref[...] * cw_ref[3:4, :]
    _rglru_coeffs(xc, wa_ref, ba_ref, wx_ref, bx_ref, lam_ref, a_s, b_s)
    h_ref[...] = a_s[...] * h0_ref[...] + b_s[...]


def _rglru_step(xr, cbuf, h0, cw, cb, wa, ba, wx, bx, lam):
    n = xr.shape[0]
    return pl.pallas_call(
        _rglru_step_kernel,
        out_shape=jax.ShapeDtypeStruct((n, D_MODEL), F32),
        scratch_shapes=[pltpu.VMEM((n, D_MODEL), F32), pltpu.VMEM((n, D_MODEL), F32)],
        name="rglru_step",
    )(xr, cbuf, h0, cw, cb, wa, ba, wx, bx, lam)


def _pool_weights(pwt_ref):
    pw = pwt_ref[...]
    e = jnp.exp(pw - jnp.max(pw, axis=0, keepdims=True))
    sm = e / jnp.sum(e, axis=0, keepdims=True)
    lane = lax.broadcasted_iota(jnp.int32, (CMP_BLOCK, KV_W), 1)
    return jnp.where(lane < KV_W // 2, sm[:, 0:1], sm[:, 1:2])


def _compress_mlp(pooled, w1_ref, w2_ref, kg_ref):
    outs = []
    for e in range(2):
        per_head = []
        for hd in range(N_KV_HEADS):
            lo = e * N_KV_HEADS * HEAD_DIM + hd * HEAD_DIM
            p = pooled[:, lo:lo + HEAD_DIM]
            hid = _dot(p.astype(BF16), w1_ref[e])
            hid = hid * _sigmoid(hid)
            comp = p + _dot(hid.astype(BF16), w2_ref[e])
            if e == 0:
                comp = _rms_rows(comp, kg_ref[0:1, :])
            per_head.append(comp)
        outs.append(jnp.concatenate(per_head, axis=1))
    return outs[0], outs[1]


def _masked_softmax_parts(s, mask):
    sm = jnp.where(mask, s, NEG)
    p = jnp.exp(sm - jnp.max(sm, axis=-1, keepdims=True)) * mask.astype(F32)
    return p, jnp.maximum(jnp.sum(p, axis=-1, keepdims=True), 1e-30)


def _compress_prompt_kernel(x2_ref, pwt_ref, w1_ref, w2_ref, kg_ref, ck_ref, cv_ref, p1_s):
    nch = x2_ref.shape[0]
    w = _pool_weights(pwt_ref)
    p0 = jnp.zeros((nch, KV_W), F32)
    p1 = jnp.zeros((nch, KV_W), F32)
    for j in range(CMP_STRIDE):
        xj = x2_ref[:, j * KV_W:(j + 1) * KV_W]
        p0 = p0 + xj * w[j:j + 1, :]
        p1 = p1 + xj * w[CMP_STRIDE + j:CMP_STRIDE + j + 1, :]
    p1_s[0:nch, :] = p1
    p1_s[nch:nch + 8, :] = jnp.zeros((8, KV_W), F32)
    pooled = p0 + p1_s[pl.ds(1, nch), :]
    ck, cv = _compress_mlp(pooled, w1_ref, w2_ref, kg_ref)
    ck_ref[...] = ck
    cv_ref[...] = cv


def _compress_prompt(kc2, pwt, w1, w2, kg):
    bsz, nch, _ = kc2.shape

    def const_spec(shape):
        return pl.BlockSpec(shape, lambda b: (0,) * len(shape))

    return pl.pallas_call(
        _compress_prompt_kernel,
        grid=(bsz,),
        in_specs=[pl.BlockSpec((None, nch, CMP_STRIDE * KV_W), lambda b: (b, 0, 0)),
                  const_spec((CMP_BLOCK, 2)), const_spec((2, HEAD_DIM, HEAD_DIM)),
                  const_spec((2, HEAD_DIM, HEAD_DIM)), const_spec((3, HEAD_DIM))],
        out_specs=[pl.BlockSpec((None, nch, 2 * HEAD_DIM), lambda b: (b, 0, 0))] * 2,
        out_shape=[jax.ShapeDtypeStruct((bsz, nch, 2 * HEAD_DIM), F32)] * 2,
        scratch_shapes=[pltpu.VMEM((nch + 8, KV_W), F32)],
        name="compress_prompt",
    )(kc2, pwt, w1, w2, kg)


def _flash_block(qh, k, v, mask, m_i, l_i, acc):
    nq, nk = mask.shape
    s = _dot_nt(qh, k).reshape(GROUP, nq, nk)
    sm = jnp.where(mask[None], s, NEG)
    m_new = jnp.maximum(m_i, jnp.max(sm, axis=-1, keepdims=True))
    p = jnp.exp(sm - m_new) * mask[None].astype(F32)
    alpha = jnp.exp(m_i - m_new)
    l_new = alpha * l_i + jnp.sum(p, axis=-1, keepdims=True)
    pv = _dot(p.reshape(GROUP * nq, nk).astype(BF16), v).reshape(GROUP, nq, HEAD_DIM)
    return m_new, l_new, alpha * acc + pv


def _nsa_prompt_kernel(q_ref, ck_ref, cv_ref, ksb_ref, kwb_ref, g_ref, o_ref, mask_s):
    qi = pl.program_id(1)
    nq = Q_BLOCK
    ncp = ck_ref.shape[0]
    nsel = ksb_ref.shape[0] // SEL_BLOCK
    n_top = min(TOP_N, nsel)
    kb_sel = Q_BLOCK // SEL_BLOCK
    start = qi * nq
    t_col = start + lax.broadcasted_iota(jnp.int32, (nq, 1), 0)
    t_row = start + lax.broadcasted_iota(jnp.int32, (1, nq), 1)
    gates = g_ref[...]

    n_row = lax.broadcasted_iota(jnp.int32, (1, ncp), 1)
    cmp_mask = (n_row * CMP_STRIDE + (CMP_BLOCK - 1) <= t_col) & (n_row < ncp - 1)
    jj = lax.broadcasted_iota(jnp.int32, (128, ncp), 0)
    nn = lax.broadcasted_iota(jnp.int32, (128, ncp), 1)
    ov_t = ((nn * CMP_STRIDE <= jj * SEL_BLOCK + SEL_BLOCK - 1)
            & (nn * CMP_STRIDE + CMP_BLOCK - 1 >= jj * SEL_BLOCK)
            & (nn < ncp - 1) & (jj < nsel)).astype(F32)
    j_col = lax.broadcasted_iota(jnp.int32, (128, nq), 0)
    cur = _shr(t_row, SEL_SHIFT)
    valid_t = (j_col <= cur) & (j_col < nsel)
    forced_t = (j_col == 0) | (valid_t & (j_col > cur - N_LOCAL_BLOCKS))
    key_lane = lax.broadcasted_iota(jnp.int32, (nq, Q_BLOCK), 1)

    for h in range(N_KV_HEADS):
        qh = jnp.concatenate(
            [q_ref[:, (h * GROUP + g) * HEAD_DIM:(h * GROUP + g + 1) * HEAD_DIM] for g in range(GROUP)], axis=0)
        hs = slice(h * HEAD_DIM, (h + 1) * HEAD_DIM)
        vs = slice((N_KV_HEADS + h) * HEAD_DIM, (N_KV_HEADS + h + 1) * HEAD_DIM)

        s = _dot_nt(qh, ck_ref[:, hs].astype(BF16)).reshape(GROUP, nq, ncp)
        p, den = _masked_softmax_parts(s, jnp.broadcast_to(cmp_mask[None], s.shape))
        p = p / den
        o_c = _dot(p.reshape(GROUP * nq, ncp).astype(BF16), cv_ref[:, hs].astype(BF16))
        psum = p[0] + p[1] + p[2] + p[3]
        imp_t = lax.dot_general(ov_t, psum, (((1,), (1,)), ((), ())),
                                precision=lax.Precision.HIGHEST, preferred_element_type=F32)
        score_t = jnp.where(valid_t, imp_t + FORCE * forced_t.astype(F32), NEG)
        rank = jnp.zeros((128, nq), jnp.int32)
        for k in range(nsel):
            rk = score_t[k:k + 1, :]
            beats = (rk > score_t) | ((rk == score_t) & (j_col > k))
            rank = rank + beats.astype(jnp.int32)
        sel_t = ((rank < n_top) & (score_t > 0.5 * NEG)).astype(F32)
        sel = sel_t.T
        for kb in range(nsel // kb_sel):
            m = jnp.zeros((nq, Q_BLOCK), F32)
            for u in range(kb_sel):
                col = sel[:, kb * kb_sel + u:kb * kb_sel + u + 1]
                m = jnp.where(_shr(key_lane, SEL_SHIFT) == u, col, m)
            mask_s[kb] = m

        def init():
            return (jnp.full((GROUP, nq, 1), NEG, F32), jnp.zeros((GROUP, nq, 1), F32),
                    jnp.zeros((GROUP, nq, HEAD_DIM), F32))

        def sel_body(kb, carry):
            off = pl.multiple_of(kb * Q_BLOCK, Q_BLOCK)
            key = off + key_lane
            mask = (mask_s[kb] > 0.5) & (key <= t_col)
            return _flash_block(qh, ksb_ref[pl.ds(off, Q_BLOCK), hs], ksb_ref[pl.ds(off, Q_BLOCK), vs],
                                mask, *carry)

        _, l_s, acc_s = lax.fori_loop(0, qi + 1, sel_body, init())
        o_s = (acc_s / jnp.maximum(l_s, 1e-30)).reshape(GROUP * nq, HEAD_DIM)

        def win_body(kb, carry):
            off = pl.multiple_of(kb * Q_BLOCK, Q_BLOCK)
            key = off + key_lane
            d = t_col - key
            mask = (d >= 0) & (d < WINDOW)
            return _flash_block(qh, kwb_ref[pl.ds(off, Q_BLOCK), hs], kwb_ref[pl.ds(off, Q_BLOCK), vs],
                                mask, *carry)

        _, l_w, acc_w = lax.fori_loop(jnp.maximum(qi - WINDOW // Q_BLOCK, 0), qi + 1, win_body, init())
        o_w = (acc_w / jnp.maximum(l_w, 1e-30)).reshape(GROUP * nq, HEAD_DIM)

        for g in range(GROUP):
            head = h * GROUP + g
            rs = slice(g * nq, (g + 1) * nq)
            o = (o_c[rs] * gates[:, head:head + 1]
                 + o_s[rs] * gates[:, N_HEADS + head:N_HEADS + head + 1]
                 + o_w[rs] * gates[:, 2 * N_HEADS + head:2 * N_HEADS + head + 1])
            o_ref[:, head * HEAD_DIM:(head + 1) * HEAD_DIM] = o


def _nsa_prompt(q, ck, cv, ksb, kwb, gates, bsz, seq):
    nqb = seq // Q_BLOCK
    ncp = ck.shape[1]
    return pl.pallas_call(
        _nsa_prompt_kernel,
        grid=(bsz, nqb),
        in_specs=[pl.BlockSpec((Q_BLOCK, D_MODEL), lambda b, i: (b * nqb + i, 0)),
                  pl.BlockSpec((None, ncp, 2 * HEAD_DIM), lambda b, i: (b, 0, 0)),
                  pl.BlockSpec((None, ncp, 2 * HEAD_DIM), lambda b, i: (b, 0, 0)),
                  pl.BlockSpec((seq, KV_W), lambda b, i: (b, 0)),
                  pl.BlockSpec((seq, KV_W), lambda b, i: (b, 0)),
                  pl.BlockSpec((Q_BLOCK, 128), lambda b, i: (b * nqb + i, 0))],
        out_specs=pl.BlockSpec((Q_BLOCK, D_MODEL), lambda b, i: (b * nqb + i, 0)),
        out_shape=jax.ShapeDtypeStruct((bsz * seq, D_MODEL), F32),
        scratch_shapes=[pltpu.VMEM((seq // Q_BLOCK, Q_BLOCK, Q_BLOCK), F32)],
        compiler_params=pltpu.CompilerParams(vmem_limit_bytes=VMEM_LIMIT),
        name="nsa_prompt",
    )(q, ck, cv, ksb, kwb, gates)


def _nsa_sample_cmp_kernel(pt_ref, *refs, past, n_steps):
    page_refs = refs[:PAGES_PER_STEP]
    (kcn_ref, pwt_ref, w1_ref, w2_ref, kg_ref, q_ref, oc_ref, imp_ref, wb_s, p0_s, p1_s) = refs[PAGES_PER_STEP:]
    del pt_ref
    g = pl.program_id(1)
    chunks_per_page = PAGE_SIZE // CMP_STRIDE
    nchunk = past // CMP_STRIDE
    nsel = past // SEL_BLOCK + 1
    t_now = past

    @pl.when(g == 0)
    def _():
        w = _pool_weights(pwt_ref)
        for j in range(CMP_BLOCK):
            wb_s[j] = jnp.broadcast_to(w[j:j + 1, :], (chunks_per_page, KV_W))

    for k in range(PAGES_PER_STEP):
        acc0 = jnp.zeros((chunks_per_page, KV_W), F32)
        acc1 = jnp.zeros((chunks_per_page, KV_W), F32)
        for j in range(CMP_STRIDE):
            xj = page_refs[k][:, j * KV_W:(j + 1) * KV_W]
            acc0 = acc0 + xj * wb_s[j]
            acc1 = acc1 + xj * wb_s[CMP_STRIDE + j]
        row = pl.multiple_of((g * PAGES_PER_STEP + k) * chunks_per_page, chunks_per_page)
        p0_s[pl.ds(row, chunks_per_page), :] = acc0
        p1_s[pl.ds(row, chunks_per_page), :] = acc1

    @pl.when(g == n_steps - 1)
    def _():
        row8 = lax.broadcasted_iota(jnp.int32, (8, KV_W), 0)
        p1_s[pl.ds(nchunk, 8), :] = jnp.where(row8 == 0, kcn_ref[...] * wb_s[CMP_STRIDE][0:1, :], 0.0)
        pooled = p0_s[0:nchunk, :] + p1_s[pl.ds(1, nchunk), :]
        ck, cv = _compress_mlp(pooled, w1_ref, w2_ref, kg_ref)
        q8 = q_ref[...]
        row = lax.broadcasted_iota(jnp.int32, (N_HEADS, 1), 0)
        first = row < GROUP
        n_row = lax.broadcasted_iota(jnp.int32, (1, nchunk), 1)
        mask = jnp.broadcast_to(n_row * CMP_STRIDE + (CMP_BLOCK - 1) <= t_now, (N_HEADS, nchunk))
        ckb = ck.astype(BF16)
        cvb = cv.astype(BF16)
        s = jnp.where(first, _dot_nt(q8, ckb[:, 0:HEAD_DIM]), _dot_nt(q8, ckb[:, HEAD_DIM:2 * HEAD_DIM]))
        p, den = _masked_softmax_parts(s, mask)
        p = p / den
        pb = p.astype(BF16)
        oc_ref[...] = jnp.where(first, _dot(pb, cvb[:, 0:HEAD_DIM]), _dot(pb, cvb[:, HEAD_DIM:2 * HEAD_DIM]))
        nn = lax.broadcasted_iota(jnp.int32, (nchunk, 256), 0)
        jj = lax.broadcasted_iota(jnp.int32, (nchunk, 256), 1)
        ov = ((nn * CMP_STRIDE <= jj * SEL_BLOCK + SEL_BLOCK - 1)
              & (nn * CMP_STRIDE + CMP_BLOCK - 1 >= jj * SEL_BLOCK) & (jj < nsel)).astype(F32)
        imp8 = jnp.dot(p, ov, precision=lax.Precision.HIGHEST, preferred_element_type=F32)
        imp_ref[0:1, :] = jnp.sum(jnp.where(first, imp8, 0.0), axis=0, keepdims=True)
        imp_ref[1:2, :] = jnp.sum(jnp.where(first, 0.0, imp8), axis=0, keepdims=True)


def _nsa_sample_cmp(page_table, cache_cmp, kc_new, pwt, w1, w2, kg, q8):
    db, n_pages = page_table.shape
    past = n_pages * PAGE_SIZE
    n_steps = n_pages // PAGES_PER_STEP
    chunks_per_page = PAGE_SIZE // CMP_STRIDE
    pages = cache_cmp.reshape(cache_cmp.shape[0], chunks_per_page, CMP_STRIDE * KV_W)

    def page_spec(k):
        return pl.BlockSpec((None, chunks_per_page, CMP_STRIDE * KV_W),
                            lambda b, g, pt: (pt[b * n_pages + g * PAGES_PER_STEP + k], 0, 0))

    def const_spec(shape):
        return pl.BlockSpec(shape, lambda b, g, pt: (0,) * len(shape))

    nchunk = past // CMP_STRIDE
    grid_spec = pltpu.PrefetchScalarGridSpec(
        num_scalar_prefetch=1,
        grid=(db, n_steps),
        in_specs=[page_spec(k) for k in range(PAGES_PER_STEP)] + [
            pl.BlockSpec((None, 1, KV_W), lambda b, g, pt: (b, 0, 0)),
            const_spec((CMP_BLOCK, 2)), const_spec((2, HEAD_DIM, HEAD_DIM)),
            const_spec((2, HEAD_DIM, HEAD_DIM)), const_spec((3, HEAD_DIM)),
            pl.BlockSpec((None, N_HEADS, HEAD_DIM), lambda b, g, pt: (b, 0, 0))],
        out_specs=[pl.BlockSpec((None, N_HEADS, HEAD_DIM), lambda b, g, pt: (b, 0, 0)),
                   pl.BlockSpec((None, 2, 256), lambda b, g, pt: (b, 0, 0))],
        scratch_shapes=[pltpu.VMEM((CMP_BLOCK, chunks_per_page, KV_W), F32),
                        pltpu.VMEM((nchunk + 8, KV_W), F32), pltpu.VMEM((nchunk + 8, KV_W), F32)])
    return pl.pallas_call(
        functools.partial(_nsa_sample_cmp_kernel, past=past, n_steps=n_steps),
        grid_spec=grid_spec,
        out_shape=[jax.ShapeDtypeStruct((db, N_HEADS, HEAD_DIM), F32),
                   jax.ShapeDtypeStruct((db, 2, 256), F32)],
        compiler_params=pltpu.CompilerParams(vmem_limit_bytes=VMEM_LIMIT),
        name="nsa_sample_cmp",
    )(page_table.reshape(-1), *([pages] * PAGES_PER_STEP), kc_new, pwt, w1, w2, kg, q8)


def _select_kernel(imp_ref, idx_ref, *, t_now, nsel):
    imp = imp_ref[...]
    rows, width = imp.shape
    j = lax.broadcasted_iota(jnp.int32, (rows, width), 1)
    jf = j.astype(F32)
    cur = t_now // SEL_BLOCK
    valid = (j <= cur) & (j < nsel)
    forced = (j == 0) | (valid & (j > cur - N_LOCAL_BLOCKS))
    score = jnp.where(valid, imp + FORCE * forced.astype(F32), NEG)
    col = lax.broadcasted_iota(jnp.int32, (rows, 128), 1)
    out = jnp.full((rows, 128), -1, jnp.int32)
    for it in range(min(TOP_N, nsel)):
        m = jnp.max(score, axis=-1, keepdims=True)
        idx = jnp.min(jnp.where(score == m, jf, 1e9), axis=-1, keepdims=True)
        out = jnp.where(col == it, jnp.where(m > 0.5 * NEG, idx.astype(jnp.int32), -1), out)
        score = jnp.where(jf == idx, -3e38, score)
    idx_ref[...] = out


def _select(imp2d, t_now, nsel):
    rows = imp2d.shape[0]
    return pl.pallas_call(
        functools.partial(_select_kernel, t_now=t_now, nsel=nsel),
        out_shape=jax.ShapeDtypeStruct((rows, 128), jnp.int32),
        name="nsa_sample_select",
    )(imp2d)


def _nsa_sample_sel_kernel(idx_ref, pt_ref, *refs, n_past_blk, n_top):
    blk_refs = refs[:N_KV_HEADS * n_top]
    q_ref, ksn_ref, os_ref = refs[N_KV_HEADS * n_top:]
    del pt_ref
    b = pl.program_id(0)
    q8 = q_ref[...]
    q8f = q8.astype(F32)
    ksn = ksn_ref[...]
    nkeys = n_top * SEL_BLOCK
    lane = lax.broadcasted_iota(jnp.int32, (1, nkeys), 1)
    row = lax.broadcasted_iota(jnp.int32, (N_HEADS, 1), 0)
    first = row < GROUP
    per_head = []
    for h in range(N_KV_HEADS):
        hs = slice(h * HEAD_DIM, (h + 1) * HEAD_DIM)
        vs = slice((N_KV_HEADS + h) * HEAD_DIM, (N_KV_HEADS + h + 1) * HEAD_DIM)
        k_all = jnp.concatenate([blk_refs[h * n_top + n][:, hs] for n in range(n_top)], axis=0).astype(BF16)
        v_all = jnp.concatenate([blk_refs[h * n_top + n][:, vs] for n in range(n_top)], axis=0).astype(BF16)
        slot_ok = jnp.zeros((1, nkeys), F32)
        new_ok = jnp.zeros((1, 1), F32)
        for n in range(n_top):
            ix = idx_ref[(b * N_KV_HEADS + h) * n_top + n]
            past_ok = jnp.where((ix >= 0) & (ix < n_past_blk), 1.0, 0.0)
            slot_ok = jnp.where(_shr(lane, SEL_SHIFT) == n, past_ok, slot_ok)
            new_ok = jnp.maximum(new_ok, jnp.where(ix >= n_past_blk, 1.0, 0.0))
        mask = jnp.broadcast_to(slot_ok > 0.5, (N_HEADS, nkeys))
        new_mask = jnp.broadcast_to(new_ok > 0.5, (N_HEADS, 1))
        s = jnp.where(mask, _dot_nt(q8, k_all), NEG)
        s_new = jnp.where(new_mask, jnp.sum(q8f * ksn[:, hs], axis=-1, keepdims=True), NEG)
        mx = jnp.maximum(jnp.max(s, axis=-1, keepdims=True), s_new)
        p = jnp.exp(s - mx) * mask.astype(F32)
        p_new = jnp.exp(s_new - mx) * new_mask.astype(F32)
        den = jnp.maximum(jnp.sum(p, axis=-1, keepdims=True) + p_new, 1e-30)
        per_head.append((_dot(p.astype(BF16), v_all) + p_new * ksn[:, vs]) / den)
    os_ref[...] = jnp.where(first, per_head[0], per_head[1])


def _nsa_sample_sel(sel_idx, page_table, cache_sel, q8, ks_new):
    db, n_pages = page_table.shape
    n_past_blk = n_pages * PAGE_SIZE // SEL_BLOCK
    n_top = sel_idx.shape[-1]
    halves = PAGE_SIZE // SEL_BLOCK
    blocks = cache_sel.reshape(cache_sel.shape[0] * halves, SEL_BLOCK, KV_W)

    def blk_spec(h, n):
        def index_map(b, idx, pt):
            blk = jnp.clip(idx[(b * N_KV_HEADS + h) * n_top + n], 0, n_past_blk - 1)
            return (pt[b * n_pages + blk // halves] * halves + blk % halves, 0, 0)
        return pl.BlockSpec((None, SEL_BLOCK, KV_W), index_map)

    grid_spec = pltpu.PrefetchScalarGridSpec(
        num_scalar_prefetch=2,
        grid=(db,),
        in_specs=[blk_spec(h, n) for h in range(N_KV_HEADS) for n in range(n_top)] + [
            pl.BlockSpec((None, N_HEADS, HEAD_DIM), lambda b, idx, pt: (b, 0, 0)),
            pl.BlockSpec((None, 1, KV_W), lambda b, idx, pt: (b, 0, 0))],
        out_specs=pl.BlockSpec((None, N_HEADS, HEAD_DIM), lambda b, idx, pt: (b, 0, 0)))
    return pl.pallas_call(
        functools.partial(_nsa_sample_sel_kernel, n_past_blk=n_past_blk, n_top=n_top),
        grid_spec=grid_spec,
        out_shape=jax.ShapeDtypeStruct((db, N_HEADS, HEAD_DIM), F32),
        compiler_params=pltpu.CompilerParams(vmem_limit_bytes=VMEM_LIMIT),
        name="nsa_sample_sel",
    )(sel_idx.reshape(-1), page_table.reshape(-1), *([blocks] * (N_KV_HEADS * n_top)), q8, ks_new)


def _nsa_sample_win_kernel(win_ref, kwn_ref, q_ref, wout_ref, ow_ref, *, past):
    wb = win_ref.shape[0]
    x = win_ref[...]
    kwn = kwn_ref[...]
    rowi = lax.broadcasted_iota(jnp.int32, (wb, KV_W), 0)
    wout_ref[...] = jnp.where(rowi == wb - 1, kwn, pltpu.roll(x, wb - 1, axis=0))
    q8 = q_ref[...]
    q8f = q8.astype(F32)
    t_now = past
    pos = past - wb + lax.broadcasted_iota(jnp.int32, (1, wb), 1)
    d = t_now - pos
    mask = jnp.broadcast_to((d >= 0) & (d < WINDOW) & (pos >= 0), (N_HEADS, wb))
    row = lax.broadcasted_iota(jnp.int32, (N_HEADS, 1), 0)
    first = row < GROUP
    xb = x.astype(BF16)
    per_head = []
    for h in range(N_KV_HEADS):
        hs = slice(h * HEAD_DIM, (h + 1) * HEAD_DIM)
        vs = slice((N_KV_HEADS + h) * HEAD_DIM, (N_KV_HEADS + h + 1) * HEAD_DIM)
        s = jnp.where(mask, _dot_nt(q8, xb[:, hs]), NEG)
        s_new = jnp.sum(q8f * kwn[:, hs], axis=-1, keepdims=True)
        mx = jnp.maximum(jnp.max(s, axis=-1, keepdims=True), s_new)
        p = jnp.exp(s - mx) * mask.astype(F32)
        p_new = jnp.exp(s_new - mx)
        den = jnp.maximum(jnp.sum(p, axis=-1, keepdims=True) + p_new, 1e-30)
        per_head.append((_dot(p.astype(BF16), xb[:, vs]) + p_new * kwn[:, vs]) / den)
    ow_ref[...] = jnp.where(first, per_head[0], per_head[1])


def _nsa_sample_win(win_buf, kw_new, q8, past):
    db, wb, _ = win_buf.shape
    return pl.pallas_call(
        functools.partial(_nsa_sample_win_kernel, past=past),
        grid=(db,),
        in_specs=[pl.BlockSpec((None, wb, KV_W), lambda b: (b, 0, 0)),
                  pl.BlockSpec((None, 1, KV_W), lambda b: (b, 0, 0)),
                  pl.BlockSpec((None, N_HEADS, HEAD_DIM), lambda b: (b, 0, 0))],
        out_specs=[pl.BlockSpec((None, wb, KV_W), lambda b: (b, 0, 0)),
                   pl.BlockSpec((None, N_HEADS, HEAD_DIM), lambda b: (b, 0, 0))],
        out_shape=[jax.ShapeDtypeStruct((db, wb, KV_W), F32),
                   jax.ShapeDtypeStruct((db, N_HEADS, HEAD_DIM), F32)],
        name="nsa_sample_win",
    )(win_buf, kw_new, q8)


def _gate_mix_kernel(oc_ref, os_ref, ow_ref, g_ref, o_ref):
    g = g_ref[...]
    c = lax.broadcasted_iota(jnp.int32, (128, D_MODEL), 0)
    head = _shr(lax.broadcasted_iota(jnp.int32, (128, D_MODEL), 1), HEAD_SHIFT)
    acc = jnp.zeros(o_ref.shape, F32)
    for r, ref in enumerate((oc_ref, os_ref, ow_ref)):
        expand = (c == r * N_HEADS + head).astype(F32)
        acc = acc + ref[...] * jnp.dot(g, expand, precision=lax.Precision.HIGHEST, preferred_element_type=F32)
    o_ref[...] = acc


def _merge_kernel(x_ref, a_ref, ra_ref, b_ref, on_ref, gate_ref, w_ref, y_ref):
    u = a_ref[...] * ra_ref[...] + b_ref[...] * on_ref[...]
    y_ref[...] = x_ref[...] + gate_ref[0] * _dot(u.astype(BF16), w_ref[...])


def _merge(x2d, a, ra, b, o_nsa, gate, w_out, tm, rows_per_mod):
    m = x2d.shape[0]
    tiles_per_mod = rows_per_mod // tm
    row_spec = pl.BlockSpec((tm, D_MODEL), lambda i: (i, 0))
    return pl.pallas_call(
        _merge_kernel,
        grid=(m // tm,),
        in_specs=[row_spec] * 5 + [
            pl.BlockSpec((1, gate.shape[1], D_MODEL), lambda i: (i // tiles_per_mod, 0, 0)),
            pl.BlockSpec((D_MODEL, D_MODEL), lambda i: (0, 0))],
        out_specs=row_spec,
        out_shape=jax.ShapeDtypeStruct((m, D_MODEL), F32),
        compiler_params=pltpu.CompilerParams(vmem_limit_bytes=VMEM_LIMIT),
        name="merge_out_proj",
    )(x2d, a, ra, b, o_nsa, gate, w_out)


def _rearranged_w_in(w_in):
    sizes = (D_MODEL, D_MODEL, D_MODEL, KV_W, KV_W, KV_W, 3 * N_HEADS, D_MODEL, D_MODEL, D_MODEL)
    offs = [0]
    for s in sizes:
        offs.append(offs[-1] + s)
    xr, zr, q, kc, ks, kw, bg, zn, ga, gb = [w_in[:, offs[i]:offs[i + 1]] for i in range(len(sizes))]
    pad = jnp.zeros((D_MODEL, C_END - C_BG - 3 * N_HEADS), w_in.dtype)
    return jnp.concatenate([xr, zr, q, kc, ks, kw, zn, ga, gb, bg, pad], axis=1).astype(BF16)


def kernel(x_prompt, x_sample, c_prompt, c_sample, state_conv, state_rglru, cache_cmp_kv, cache_sel_kv,
           state_win_kv, page_table, norm_g, w_ada, b_ada, w_in, conv_w, conv_b, rg_wa, rg_ba, rg_wx, rg_bx,
           rg_lambda, q_norm_g, k_norm_g, cmp_pool_w, cmp_w1, cmp_w2, w_out):
    depth = norm_g.shape[0]
    assert depth == 1 and x_sample.shape[1] == 1
    bsz, seq, _ = x_prompt.shape
    db = x_sample.shape[0]
    n_pages = page_table.shape[1]
    past = n_pages * PAGE_SIZE
    layer = 0

    w_cat = _rearranged_w_in(w_in[layer])
    w_out_b = w_out[layer].astype(BF16)
    wa_b = rg_wa[layer].astype(BF16)
    wx_b = rg_wx[layer].astype(BF16)
    w1_b = cmp_w1[layer].astype(BF16)
    w2_b = cmp_w2[layer].astype(BF16)
    row = lambda v: v.reshape(1, -1)
    rg_args = (conv_w[layer], row(conv_b[layer]), wa_b, row(rg_ba[layer]), wx_b, row(rg_bx[layer]),
               row(rg_lambda[layer]))
    pwt = cmp_pool_w[layer].T
    kg = k_norm_g[layer]

    mod = _modulation(jnp.concatenate([c_prompt, c_sample], axis=0), w_ada[layer], b_ada[layer])
    shift, scale, gate = mod[:, :D_MODEL], mod[:, D_MODEL:2 * D_MODEL], mod[:, 2 * D_MODEL:]

    xp2 = x_prompt.reshape(bsz * seq, D_MODEL)
    pm = lambda v: v[:bsz].reshape(bsz, 1, D_MODEL)
    (xr_p, a_p, b_p, q_p, kc_p, ks_p, kw_p, ksb_p, kwb_p, g_p) = _project(
        xp2, pm(shift), pm(scale), norm_g[layer], w_cat, q_norm_g[layer], kg, tm=256, rows_per_mod=seq)
    ra_p, h_p = _rglru_prompt(xr_p, bsz, seq, *rg_args)
    ck_p, cv_p = _compress_prompt(kc_p.reshape(bsz, seq // CMP_STRIDE, CMP_STRIDE * KV_W), pwt, w1_b, w2_b, kg)
    on_p = _nsa_prompt(q_p, ck_p, cv_p, ksb_p, kwb_p, g_p, bsz, seq)
    y_p = _merge(xp2, a_p, ra_p, b_p, on_p, pm(gate), w_out_b, tm=256, rows_per_mod=seq)

    xs2 = x_sample.reshape(db, D_MODEL)
    sm = lambda v: v[bsz:].reshape(1, db, D_MODEL)
    (xr_s, a_s, b_s, q_s, kc_s, ks_s, kw_s, _, _, g_s) = _project(
        xs2, sm(shift), sm(scale), norm_g[layer], w_cat, q_norm_g[layer], kg, tm=db, rows_per_mod=db)
    h_s = _rglru_step(xr_s, state_conv[layer].reshape(db, (CONV_W - 1) * D_MODEL), state_rglru[layer], *rg_args)
    q8 = q_s.reshape(db, N_HEADS, HEAD_DIM)
    oc_s, imp = _nsa_sample_cmp(page_table, cache_cmp_kv[layer], kc_s.reshape(db, 1, KV_W), pwt, w1_b, w2_b, kg, q8)
    nsel = past // SEL_BLOCK + 1
    sel_idx = _select(imp.reshape(db * N_KV_HEADS, 256), past, nsel)[:, :min(TOP_N, nsel)]
    os_s = _nsa_sample_sel(sel_idx, page_table, cache_sel_kv[layer], q8, ks_s.reshape(db, 1, KV_W))
    wb = state_win_kv.shape[2]
    win_s, ow_s = _nsa_sample_win(state_win_kv[layer].reshape(db, wb, KV_W), kw_s.reshape(db, 1, KV_W), q8, past)
    on_s = pl.pallas_call(
        _gate_mix_kernel, out_shape=jax.ShapeDtypeStruct((db, D_MODEL), F32), name="nsa_sample_mix",
    )(oc_s.reshape(db, D_MODEL), os_s.reshape(db, D_MODEL), ow_s.reshape(db, D_MODEL), g_s)
    y_s = _merge(xs2, a_s, h_s, b_s, on_s, sm(gate), w_out_b, tm=db, rows_per_mod=db)

    kv_shape = (2, N_KV_HEADS, HEAD_DIM)
    xr_p3 = xr_p.reshape(bsz, seq, D_MODEL)
    conv_prompt = xr_p3[:, seq - (CONV_W - 1):][None]
    conv_sample = jnp.concatenate([state_conv[layer][:, 1:], xr_s[:, None, :]], axis=1)[None]
    win_len = min(WINDOW, seq)
    return (y_p.reshape(bsz, seq, D_MODEL), y_s.reshape(db, 1, D_MODEL),
            conv_prompt, conv_sample,
            h_p.reshape(1, bsz, D_MODEL), h_s.reshape(1, db, D_MODEL),
            kc_p.reshape(1, bsz, seq, *kv_shape), kc_s.reshape(1, db, 1, *kv_shape),
            ks_p.reshape(1, bsz, seq, *kv_shape), ks_s.reshape(1, db, 1, *kv_shape),
            kw_p.reshape(bsz, seq, *kv_shape)[:, seq - win_len:][None],
            win_s.reshape(1, db, wb, *kv_shape))
```

```python
import functools

import jax
import jax.numpy as jnp
from jax import lax
from jax.experimental import pallas as pl
from jax.experimental.pallas import tpu as pltpu

F32 = jnp.float32
BF16 = jnp.bfloat16

D_MODEL = 1024
RG_BLOCKS = 8
RG_BW = D_MODEL // RG_BLOCKS
RG_C = 8.0
CONV_W = 4
N_HEADS = 8
N_KV_HEADS = 2
GROUP = N_HEADS // N_KV_HEADS
HEAD_DIM = D_MODEL // N_HEADS
KV_W = 2 * N_KV_HEADS * HEAD_DIM
KV_ROWS = 2 * N_KV_HEADS
CMP_BLOCK = 32
CMP_STRIDE = 16
SEL_BLOCK = 64
TOP_N = 16
N_LOCAL_BLOCKS = 2
WINDOW = 512
Q_BLOCK = 128
PAGE_SIZE = 128
EPS = 1e-6
NEG = -1e30
FORCE = 1e4

C_XR, C_ZR, C_Q, C_KC, C_KS, C_KW, C_ZN, C_GA, C_GB, C_BG, C_END = (
    0, 1024, 2048, 3072, 3584, 4096, 4608, 5632, 6656, 7680, 7808)
VMEM_LIMIT = 56 * 1024 * 1024
PAGES_PER_STEP = 16


SEL_SHIFT = SEL_BLOCK.bit_length() - 1
HEAD_SHIFT = HEAD_DIM.bit_length() - 1


def _shr(x, k):
    return lax.shift_right_arithmetic(x, jnp.int32(k))


def _sigmoid(x):
    return jax.nn.sigmoid(x)


def _dot(a, b):
    return jnp.dot(a, b, preferred_element_type=F32)


def _dot_nt(a, b):
    return lax.dot_general(a, b, (((1,), (1,)), ((), ())), preferred_element_type=F32)


def _rms_rows(x, g):
    return x * lax.rsqrt(jnp.mean(x * x, axis=-1, keepdims=True) + EPS) * g


def _mod_kernel(c_ref, w_ref, b_ref, o_ref):
    c = c_ref[...]
    s = c * _sigmoid(c)
    o_ref[...] = _dot(s.astype(BF16), w_ref[...].astype(BF16)) + b_ref[...]


def _modulation(c_all, w_ada, b_ada):
    n = c_all.shape[0]
    return pl.pallas_call(
        _mod_kernel,
        grid=(3,),
        in_specs=[pl.BlockSpec((n, D_MODEL), lambda j: (0, 0)),
                  pl.BlockSpec((D_MODEL, D_MODEL), lambda j: (0, j)),
                  pl.BlockSpec((1, D_MODEL), lambda j: (0, j))],
        out_specs=pl.BlockSpec((n, D_MODEL), lambda j: (0, j)),
        out_shape=jax.ShapeDtypeStruct((n, 3 * D_MODEL), F32),
        name="adaln_mod",
    )(c_all, w_ada, b_ada.reshape(1, 3 * D_MODEL))


def _proj_kernel(x_ref, shift_ref, scale_ref, ng_ref, w_ref, qg_ref, kg_ref,
                 xr_ref, a_ref, b_ref, q_ref, kc_ref, ks_ref, kw_ref, ksb_ref, kwb_ref, g_ref):
    x = x_ref[...]
    h = _rms_rows(x, ng_ref[...])
    h = h * (1.0 + scale_ref[0]) + shift_ref[0]
    hb = h.astype(BF16)

    def mm(lo, hi):
        return _dot(hb, w_ref[:, lo:hi])

    xr_ref[...] = mm(C_XR, C_ZR)
    zr = mm(C_ZR, C_Q)
    ga = mm(C_GA, C_GB)
    a_ref[...] = _sigmoid(ga) * (zr * _sigmoid(zr))
    zn = mm(C_ZN, C_GA)
    gb = mm(C_GB, C_BG)
    b_ref[...] = _sigmoid(gb) * (zn * _sigmoid(zn))
    q = mm(C_Q, C_KC)
    for hd in range(N_HEADS):
        sl = slice(hd * HEAD_DIM, (hd + 1) * HEAD_DIM)
        q_ref[:, sl] = (_rms_rows(q[:, sl], qg_ref[...]) * (HEAD_DIM ** -0.5)).astype(BF16)
    tm = x.shape[0]

    def store_kv(o_ref, eh, val):
        o_ref[pl.ds(eh, tm, stride=KV_ROWS), :] = val

    kc = mm(C_KC, C_KS)
    for eh in range(KV_ROWS):
        store_kv(kc_ref, eh, kc[:, eh * HEAD_DIM:(eh + 1) * HEAD_DIM])
    for lo, hi, o_ref, ob_ref, gi in ((C_KS, C_KW, ks_ref, ksb_ref, 1), (C_KW, C_ZN, kw_ref, kwb_ref, 2)):
        kv = mm(lo, hi)
        for hd in range(N_KV_HEADS):
            sl = slice(hd * HEAD_DIM, (hd + 1) * HEAD_DIM)
            kn = _rms_rows(kv[:, sl], kg_ref[gi:gi + 1, :])
            store_kv(o_ref, hd, kn)
            ob_ref[:, sl] = kn.astype(BF16)
            vsl = slice((N_KV_HEADS + hd) * HEAD_DIM, (N_KV_HEADS + hd + 1) * HEAD_DIM)
            store_kv(o_ref, N_KV_HEADS + hd, kv[:, vsl])
            ob_ref[:, vsl] = kv[:, vsl].astype(BF16)
    g_ref[...] = _sigmoid(mm(C_BG, C_END))


def _project(x2d, shift, scale, norm_g, w_cat, q_norm_g, k_norm_g, tm, rows_per_mod):
    m = x2d.shape[0]
    tiles_per_mod = rows_per_mod // tm
    mod_rows = shift.shape[1]
    mod_spec = pl.BlockSpec((1, mod_rows, D_MODEL), lambda i: (i // tiles_per_mod, 0, 0))

    def row_spec(width, mult=1):
        return pl.BlockSpec((tm * mult, width), lambda i: (i, 0))

    def const_spec(shape):
        return pl.BlockSpec(shape, lambda i: (0,) * len(shape))

    outs = ((D_MODEL, F32, 1), (D_MODEL, F32, 1), (D_MODEL, F32, 1), (D_MODEL, BF16, 1),
            (HEAD_DIM, F32, KV_ROWS), (HEAD_DIM, F32, KV_ROWS), (HEAD_DIM, F32, KV_ROWS),
            (KV_W, BF16, 1), (KV_W, BF16, 1), (128, F32, 1))
    return pl.pallas_call(
        _proj_kernel,
        grid=(m // tm,),
        in_specs=[row_spec(D_MODEL), mod_spec, mod_spec, const_spec((1, D_MODEL)),
                  pl.BlockSpec((D_MODEL, C_END), lambda i: (0, 0), pipeline_mode=pl.Buffered(1)),
                  const_spec((1, HEAD_DIM)), const_spec((3, HEAD_DIM))],
        out_specs=[row_spec(w, mult) for w, _, mult in outs],
        out_shape=[jax.ShapeDtypeStruct((m * mult, w), dt) for w, dt, mult in outs],
        compiler_params=pltpu.CompilerParams(vmem_limit_bytes=VMEM_LIMIT),
        name="in_proj",
    )(x2d, shift, scale, norm_g.reshape(1, D_MODEL), w_cat, q_norm_g.reshape(1, HEAD_DIM), k_norm_g)


def _softplus(z):
    return jnp.maximum(z, 0.0) + jnp.log1p(jnp.exp(-jnp.abs(z)))


def _rglru_coeffs(xc, wa_ref, ba_ref, wx_ref, bx_ref, lam_ref, a_out, b_out):
    xcb = xc.astype(BF16)
    sp = _softplus(-lam_ref[...])
    for k in range(RG_BLOCKS):
        sl = slice(k * RG_BW, (k + 1) * RG_BW)
        r = _sigmoid(_dot(xcb[:, sl], wa_ref[k]) + ba_ref[:, sl])
        i = _sigmoid(_dot(xcb[:, sl], wx_ref[k]) + bx_ref[:, sl])
        log_a = -RG_C * r * sp[:, sl]
        a = jnp.exp(log_a)
        a_out[:, sl] = a
        b_out[:, sl] = jnp.sqrt(-jnp.tanh(log_a) * (a * a + 1.0)) * i * xc[:, sl]


def _rglru_prompt_kernel(xr_ref, cw_ref, cb_ref, wa_ref, ba_ref, wx_ref, bx_ref, lam_ref,
                         ra_ref, hl_ref, ext_s, a_s, b_s, h_s):
    t_len = xr_ref.shape[0]

    @pl.when(pl.program_id(1) == 0)
    def _():
        ext_s[0:8, :] = jnp.zeros((8, D_MODEL), F32)
        h_s[...] = jnp.zeros((1, D_MODEL), F32)

    x = xr_ref[...]
    ext_s[8:8 + t_len, :] = x
    xc = ext_s[pl.ds(5, t_len), :] * cw_ref[0:1, :] + cb_ref[...]
    xc = xc + ext_s[pl.ds(6, t_len), :] * cw_ref[1:2, :]
    xc = xc + ext_s[pl.ds(7, t_len), :] * cw_ref[2:3, :]
    xc = xc + x * cw_ref[3:4, :]
    ext_s[0:8, :] = x[t_len - 8:t_len, :]
    _rglru_coeffs(xc, wa_ref, ba_ref, wx_ref, bx_ref, lam_ref, a_s, b_s)

    def body(t, h):
        h = a_s[pl.ds(t, 1), :] * h + b_s[pl.ds(t, 1), :]
        ra_ref[pl.ds(t, 1), :] = h
        return h

    h = lax.fori_loop(0, t_len, body, h_s[...], unroll=8)
    h_s[...] = h
    hl_ref[...] = h


def _rglru_prompt(xr, bsz, seq, cw, cb, wa, ba, wx, bx, lam, t_chunk=512):
    nchunk = seq // t_chunk

    def const_spec(shape):
        return pl.BlockSpec(shape, lambda b, c: (0,) * len(shape))

    return pl.pallas_call(
        _rglru_prompt_kernel,
        grid=(bsz, nchunk),
        in_specs=[pl.BlockSpec((t_chunk, D_MODEL), lambda b, c: (b * nchunk + c, 0)),
                  const_spec((CONV_W, D_MODEL)), const_spec((1, D_MODEL)),
                  const_spec((RG_BLOCKS, RG_BW, RG_BW)), const_spec((1, D_MODEL)),
                  const_spec((RG_BLOCKS, RG_BW, RG_BW)), const_spec((1, D_MODEL)),
                  const_spec((1, D_MODEL))],
        out_specs=[pl.BlockSpec((t_chunk, D_MODEL), lambda b, c: (b * nchunk + c, 0)),
                   pl.BlockSpec((None, 1, D_MODEL), lambda b, c: (b, 0, 0))],
        out_shape=[jax.ShapeDtypeStruct((bsz * seq, D_MODEL), F32),
                   jax.ShapeDtypeStruct((bsz, 1, D_MODEL), F32)],
        scratch_shapes=[pltpu.VMEM((t_chunk + 8, D_MODEL), F32), pltpu.VMEM((t_chunk, D_MODEL), F32),
                        pltpu.VMEM((t_chunk, D_MODEL), F32), pltpu.VMEM((1, D_MODEL), F32)],
        compiler_params=pltpu.CompilerParams(vmem_limit_bytes=VMEM_LIMIT),
        name="rglru_prompt",
    )(xr, cw, cb, wa, ba, wx, bx, lam)


def _rglru_step_kernel(xr_ref, cbuf_ref, h0_ref, cw_ref, cb_ref, wa_ref, ba_ref, wx_ref, bx_ref, lam_ref,
                       h_ref, a_s, b_s):
    xc = cbuf_ref[:, 0:D_MODEL] * cw_ref[0:1, :] + cb_ref[...]
    xc = xc + cbuf_ref[:, D_MODEL:2 * D_MODEL] * cw_ref[1:2, :]
    xc = xc + cbuf_ref[:, 2 * D_MODEL:3 * D_MODEL] * cw_ref[2:3, :]
    xc = xc + xr_ref[...] * cw_ref[3:4, :]
    _rglru_coeffs(xc, wa_ref, ba_ref, wx_ref, bx_ref, lam_ref, a_s, b_s)
    h_ref[...] = a_s[...] * h0_ref[...] + b_s[...]


def _rglru_step(xr, cbuf, h0, cw, cb, wa, ba, wx, bx, lam):
    n = xr.shape[0]
    return pl.pallas_call(
        _rglru_step_kernel,
        out_shape=jax.ShapeDtypeStruct((n, D_MODEL), F32),
        scratch_shapes=[pltpu.VMEM((n, D_MODEL), F32), pltpu.VMEM((n, D_MODEL), F32)],
        name="rglru_step",
    )(xr, cbuf, h0, cw, cb, wa, ba, wx, bx, lam)


CHUNK_ROWS = CMP_STRIDE * KV_ROWS


def _pool_weight_rows(praw_ref):
    praw = praw_ref[...]
    row = lax.broadcasted_iota(jnp.int32, praw.shape, 0)
    is_k = (row & (KV_ROWS - 1)) < N_KV_HEADS
    m_k = jnp.max(jnp.where(is_k, praw, -3e38), axis=0, keepdims=True)
    m_v = jnp.max(jnp.where(is_k, -3e38, praw), axis=0, keepdims=True)
    ex = jnp.exp(praw - jnp.where(is_k, m_k, m_v))
    s_k = jnp.sum(jnp.where(is_k, ex, 0.0), axis=0, keepdims=True) * (1.0 / N_KV_HEADS)
    s_v = jnp.sum(jnp.where(is_k, 0.0, ex), axis=0, keepdims=True) * (1.0 / N_KV_HEADS)
    return ex / jnp.where(is_k, s_k, s_v)


def _pool_rows(x, wv):
    n = x.shape[0] // CHUNK_ROWS
    x4 = x.reshape(n, CHUNK_ROWS // 8, 8, HEAD_DIM)
    w4 = wv.reshape(2, CHUNK_ROWS // 8, 8, HEAD_DIM)
    p0 = jnp.sum(x4 * w4[0][None], axis=1)
    p1 = jnp.sum(x4 * w4[1][None], axis=1)
    return p0.reshape(n * 8, HEAD_DIM), p1.reshape(n * 8, HEAD_DIM)


def _pooled_head(p0_s, p1_s, eh, nc):
    def col(ref, start):
        return ref[pl.ds(start, nc, stride=8), :]
    return (col(p0_s, eh) + col(p0_s, eh + KV_ROWS)) + (col(p1_s, 8 + eh) + col(p1_s, 8 + eh + KV_ROWS))


def _compress_mlp(p0_s, p1_s, nc, w1_ref, w2_ref, kg_ref):
    outs = []
    for e in range(2):
        per_head = []
        for hd in range(N_KV_HEADS):
            p = _pooled_head(p0_s, p1_s, e * N_KV_HEADS + hd, nc)
            hid = _dot(p.astype(BF16), w1_ref[e])
            hid = hid * _sigmoid(hid)
            comp = p + _dot(hid.astype(BF16), w2_ref[e])
            if e == 0:
                comp = _rms_rows(comp, kg_ref[0:1, :])
            per_head.append(comp)
        outs.append(jnp.concatenate(per_head, axis=1))
    return outs[0], outs[1]


def _masked_softmax_parts(s, mask):
    sm = jnp.where(mask, s, NEG)
    p = jnp.exp(sm - jnp.max(sm, axis=-1, keepdims=True)) * mask.astype(F32)
    return p, jnp.maximum(jnp.sum(p, axis=-1, keepdims=True), 1e-30)


POOL_SLAB = 8


def _compress_prompt_kernel(x_ref, praw_ref, w1_ref, w2_ref, kg_ref, ck_ref, cv_ref, p0_s, p1_s):
    nch = x_ref.shape[0] // CHUNK_ROWS
    wv = _pool_weight_rows(praw_ref)

    def slab(i, _):
        x = x_ref[pl.ds(pl.multiple_of(i * (POOL_SLAB * CHUNK_ROWS), POOL_SLAB * CHUNK_ROWS),
                        POOL_SLAB * CHUNK_ROWS), :]
        p0, p1 = _pool_rows(x, wv)
        rows = pl.ds(pl.multiple_of(i * (POOL_SLAB * 8), POOL_SLAB * 8), POOL_SLAB * 8)
        p0_s[rows, :] = p0
        p1_s[rows, :] = p1
        return 0

    lax.fori_loop(0, nch // POOL_SLAB, slab, 0)
    p1_s[pl.ds(nch * 8, 8), :] = jnp.zeros((8, HEAD_DIM), F32)
    ck, cv = _compress_mlp(p0_s, p1_s, nch, w1_ref, w2_ref, kg_ref)
    ck_ref[...] = ck
    cv_ref[...] = cv


def _compress_prompt(kc_rows, praw, w1, w2, kg, bsz, seq):
    nch = seq // CMP_STRIDE

    def const_spec(shape):
        return pl.BlockSpec(shape, lambda b: (0,) * len(shape))

    return pl.pallas_call(
        _compress_prompt_kernel,
        grid=(bsz,),
        in_specs=[pl.BlockSpec((seq * KV_ROWS, HEAD_DIM), lambda b: (b, 0)),
                  const_spec((2 * CHUNK_ROWS, HEAD_DIM)), const_spec((2, HEAD_DIM, HEAD_DIM)),
                  const_spec((2, HEAD_DIM, HEAD_DIM)), const_spec((3, HEAD_DIM))],
        out_specs=[pl.BlockSpec((None, nch, 2 * HEAD_DIM), lambda b: (b, 0, 0))] * 2,
        out_shape=[jax.ShapeDtypeStruct((bsz, nch, 2 * HEAD_DIM), F32)] * 2,
        scratch_shapes=[pltpu.VMEM(((nch + 1) * 8, HEAD_DIM), F32)] * 2,
        name="compress_prompt",
    )(kc_rows, praw, w1, w2, kg)


def _flash_block(qh, k, v, mask, m_i, l_i, acc):
    nq, nk = mask.shape
    s = _dot_nt(qh, k).reshape(GROUP, nq, nk)
    sm = jnp.where(mask[None], s, NEG)
    m_new = jnp.maximum(m_i, jnp.max(sm, axis=-1, keepdims=True))
    p = jnp.exp(sm - m_new) * mask[None].astype(F32)
    alpha = jnp.exp(m_i - m_new)
    l_new = alpha * l_i + jnp.sum(p, axis=-1, keepdims=True)
    pv = _dot(p.reshape(GROUP * nq, nk).astype(BF16), v).reshape(GROUP, nq, HEAD_DIM)
    return m_new, l_new, alpha * acc + pv


def _nsa_prompt_kernel(q_ref, ck_ref, cv_ref, ksb_ref, kwb_ref, g_ref, o_ref, mask_s):
    qi = pl.program_id(1)
    nq = Q_BLOCK
    ncp = ck_ref.shape[0]
    nsel = ksb_ref.shape[0] // SEL_BLOCK
    n_top = min(TOP_N, nsel)
    kb_sel = Q_BLOCK // SEL_BLOCK
    start = qi * nq
    t_col = start + lax.broadcasted_iota(jnp.int32, (nq, 1), 0)
    t_row = start + lax.broadcasted_iota(jnp.int32, (1, nq), 1)
    gates = g_ref[...]

    n_row = lax.broadcasted_iota(jnp.int32, (1, ncp), 1)
    cmp_mask = (n_row * CMP_STRIDE + (CMP_BLOCK - 1) <= t_col) & (n_row < ncp - 1)
    jj = lax.broadcasted_iota(jnp.int32, (128, ncp), 0)
    nn = lax.broadcasted_iota(jnp.int32, (128, ncp), 1)
    ov_t = ((nn * CMP_STRIDE <= jj * SEL_BLOCK + SEL_BLOCK - 1)
            & (nn * CMP_STRIDE + CMP_BLOCK - 1 >= jj * SEL_BLOCK)
            & (nn < ncp - 1) & (jj < nsel)).astype(F32)
    j_col = lax.broadcasted_iota(jnp.int32, (128, nq), 0)
    cur = _shr(t_row, SEL_SHIFT)
    valid_t = (j_col <= cur) & (j_col < nsel)
    forced_t = (j_col == 0) | (valid_t & (j_col > cur - N_LOCAL_BLOCKS))
    key_lane = lax.broadcasted_iota(jnp.int32, (nq, Q_BLOCK), 1)

    for h in range(N_KV_HEADS):
        qh = jnp.concatenate(
            [q_ref[:, (h * GROUP + g) * HEAD_DIM:(h * GROUP + g + 1) * HEAD_DIM] for g in range(GROUP)], axis=0)
        hs = slice(h * HEAD_DIM, (h + 1) * HEAD_DIM)
        vs = slice((N_KV_HEADS + h) * HEAD_DIM, (N_KV_HEADS + h + 1) * HEAD_DIM)

        s = _dot_nt(qh, ck_ref[:, hs].astype(BF16)).reshape(GROUP, nq, ncp)
        p, den = _masked_softmax_parts(s, jnp.broadcast_to(cmp_mask[None], s.shape))
        p = p / den
        o_c = _dot(p.reshape(GROUP * nq, ncp).astype(BF16), cv_ref[:, hs].astype(BF16))
        psum = p[0] + p[1] + p[2] + p[3]
        imp_t = lax.dot_general(ov_t, psum, (((1,), (1,)), ((), ())),
                                precision=lax.Precision.HIGHEST, preferred_element_type=F32)
        score_t = jnp.where(valid_t, imp_t + FORCE * forced_t.astype(F32), NEG)
        rank = jnp.zeros((128, nq), jnp.int32)
        for k in range(nsel):
            rk = score_t[k:k + 1, :]
            beats = (rk > score_t) | ((rk == score_t) & (j_col > k))
            rank = rank + beats.astype(jnp.int32)
        sel_t = ((rank < n_top) & (score_t > 0.5 * NEG)).astype(F32)
        sel = sel_t.T
        for kb in range(nsel // kb_sel):
            m = jnp.zeros((nq, Q_BLOCK), F32)
            for u in range(kb_sel):
                col = sel[:, kb * kb_sel + u:kb * kb_sel + u + 1]
                m = jnp.where(_shr(key_lane, SEL_SHIFT) == u, col, m)
            mask_s[kb] = m

        def init():
            return (jnp.full((GROUP, nq, 1), NEG, F32), jnp.zeros((GROUP, nq, 1), F32),
                    jnp.zeros((GROUP, nq, HEAD_DIM), F32))

        def sel_body(kb, carry):
            off = pl.multiple_of(kb * Q_BLOCK, Q_BLOCK)
            key = off + key_lane
            mask = (mask_s[kb] > 0.5) & (key <= t_col)
            return _flash_block(qh, ksb_ref[pl.ds(off, Q_BLOCK), hs], ksb_ref[pl.ds(off, Q_BLOCK), vs],
                                mask, *carry)

        _, l_s, acc_s = lax.fori_loop(0, qi + 1, sel_body, init())
        o_s = (acc_s / jnp.maximum(l_s, 1e-30)).reshape(GROUP * nq, HEAD_DIM)

        def win_body(kb, carry):
            off = pl.multiple_of(kb * Q_BLOCK, Q_BLOCK)
            key = off + key_lane
            d = t_col - key
            mask = (d >= 0) & (d < WINDOW)
            return _flash_block(qh, kwb_ref[pl.ds(off, Q_BLOCK), hs], kwb_ref[pl.ds(off, Q_BLOCK), vs],
                                mask, *carry)

        _, l_w, acc_w = lax.fori_loop(jnp.maximum(qi - WINDOW // Q_BLOCK, 0), qi + 1, win_body, init())
        o_w = (acc_w / jnp.maximum(l_w, 1e-30)).reshape(GROUP * nq, HEAD_DIM)

        for g in range(GROUP):
            head = h * GROUP + g
            rs = slice(g * nq, (g + 1) * nq)
            o = (o_c[rs] * gates[:, head:head + 1]
                 + o_s[rs] * gates[:, N_HEADS + head:N_HEADS + head + 1]
                 + o_w[rs] * gates[:, 2 * N_HEADS + head:2 * N_HEADS + head + 1])
            o_ref[:, head * HEAD_DIM:(head + 1) * HEAD_DIM] = o


def _nsa_prompt(q, ck, cv, ksb, kwb, gates, bsz, seq):
    nqb = seq // Q_BLOCK
    ncp = ck.shape[1]
    return pl.pallas_call(
        _nsa_prompt_kernel,
        grid=(bsz, nqb),
        in_specs=[pl.BlockSpec((Q_BLOCK, D_MODEL), lambda b, i: (b * nqb + i, 0)),
                  pl.BlockSpec((None, ncp, 2 * HEAD_DIM), lambda b, i: (b, 0, 0)),
                  pl.BlockSpec((None, ncp, 2 * HEAD_DIM), lambda b, i: (b, 0, 0)),
                  pl.BlockSpec((seq, KV_W), lambda b, i: (b, 0)),
                  pl.BlockSpec((seq, KV_W), lambda b, i: (b, 0)),
                  pl.BlockSpec((Q_BLOCK, 128), lambda b, i: (b * nqb + i, 0))],
        out_specs=pl.BlockSpec((Q_BLOCK, D_MODEL), lambda b, i: (b * nqb + i, 0)),
        out_shape=jax.ShapeDtypeStruct((bsz * seq, D_MODEL), F32),
        scratch_shapes=[pltpu.VMEM((seq // Q_BLOCK, Q_BLOCK, Q_BLOCK), F32)],
        compiler_params=pltpu.CompilerParams(vmem_limit_bytes=VMEM_LIMIT),
        name="nsa_prompt",
    )(q, ck, cv, ksb, kwb, gates)


def _nsa_sample_cmp_kernel(pt_ref, *refs, past, n_steps):
    page_refs = refs[:PAGES_PER_STEP]
    (kcn_ref, praw_ref, w1_ref, w2_ref, kg_ref, q_ref, oc_ref, imp_ref, wv_s, p0_s, p1_s) = refs[PAGES_PER_STEP:]
    del pt_ref
    g = pl.program_id(1)
    chunks_per_page = PAGE_SIZE // CMP_STRIDE
    nchunk = past // CMP_STRIDE
    nsel = past // SEL_BLOCK + 1
    t_now = past

    @pl.when(g == 0)
    def _():
        wv_s[...] = _pool_weight_rows(praw_ref)

    wv = wv_s[...]
    for k in range(PAGES_PER_STEP):
        p0, p1 = _pool_rows(page_refs[k][...], wv)
        rows = pl.ds(pl.multiple_of((g * PAGES_PER_STEP + k) * (chunks_per_page * 8), chunks_per_page * 8),
                     chunks_per_page * 8)
        p0_s[rows, :] = p0
        p1_s[rows, :] = p1

    @pl.when(g == n_steps - 1)
    def _():
        p1_s[pl.ds(nchunk * 8, KV_ROWS), :] = kcn_ref[...] * wv_s[pl.ds(CHUNK_ROWS, KV_ROWS), :]
        p1_s[pl.ds(nchunk * 8 + KV_ROWS, KV_ROWS), :] = jnp.zeros((KV_ROWS, HEAD_DIM), F32)
        ck, cv = _compress_mlp(p0_s, p1_s, nchunk, w1_ref, w2_ref, kg_ref)
        q8 = q_ref[...]
        row = lax.broadcasted_iota(jnp.int32, (N_HEADS, 1), 0)
        first = row < GROUP
        n_row = lax.broadcasted_iota(jnp.int32, (1, nchunk), 1)
        mask = jnp.broadcast_to(n_row * CMP_STRIDE + (CMP_BLOCK - 1) <= t_now, (N_HEADS, nchunk))
        ckb = ck.astype(BF16)
        cvb = cv.astype(BF16)
        s = jnp.where(first, _dot_nt(q8, ckb[:, 0:HEAD_DIM]), _dot_nt(q8, ckb[:, HEAD_DIM:2 * HEAD_DIM]))
        p, den = _masked_softmax_parts(s, mask)
        p = p / den
        pb = p.astype(BF16)
        oc_ref[...] = jnp.where(first, _dot(pb, cvb[:, 0:HEAD_DIM]), _dot(pb, cvb[:, HEAD_DIM:2 * HEAD_DIM]))
        nn = lax.broadcasted_iota(jnp.int32, (nchunk, 256), 0)
        jj = lax.broadcasted_iota(jnp.int32, (nchunk, 256), 1)
        ov = ((nn * CMP_STRIDE <= jj * SEL_BLOCK + SEL_BLOCK - 1)
              & (nn * CMP_STRIDE + CMP_BLOCK - 1 >= jj * SEL_BLOCK) & (jj < nsel)).astype(F32)
        imp8 = jnp.dot(p, ov, precision=lax.Precision.HIGHEST, preferred_element_type=F32)
        imp_ref[0:1, :] = jnp.sum(jnp.where(first, imp8, 0.0), axis=0, keepdims=True)
        imp_ref[1:2, :] = jnp.sum(jnp.where(first, 0.0, imp8), axis=0, keepdims=True)


def _nsa_sample_cmp(page_table, cache_cmp, kc_new, pwt, w1, w2, kg, q8):
    db, n_pages = page_table.shape
    past = n_pages * PAGE_SIZE
    n_steps = n_pages // PAGES_PER_STEP
    pages = cache_cmp.reshape(cache_cmp.shape[0], PAGE_SIZE * KV_ROWS, HEAD_DIM)

    def page_spec(k):
        return pl.BlockSpec((None, PAGE_SIZE * KV_ROWS, HEAD_DIM),
                            lambda b, g, pt: (pt[b * n_pages + g * PAGES_PER_STEP + k], 0, 0))

    def const_spec(shape):
        return pl.BlockSpec(shape, lambda b, g, pt: (0,) * len(shape))

    nchunk = past // CMP_STRIDE
    grid_spec = pltpu.PrefetchScalarGridSpec(
        num_scalar_prefetch=1,
        grid=(db, n_steps),
        in_specs=[page_spec(k) for k in range(PAGES_PER_STEP)] + [
            pl.BlockSpec((None, KV_ROWS, HEAD_DIM), lambda b, g, pt: (b, 0, 0)),
            const_spec((2 * CHUNK_ROWS, HEAD_DIM)), const_spec((2, HEAD_DIM, HEAD_DIM)),
            const_spec((2, HEAD_DIM, HEAD_DIM)), const_spec((3, HEAD_DIM)),
            pl.BlockSpec((None, N_HEADS, HEAD_DIM), lambda b, g, pt: (b, 0, 0))],
        out_specs=[pl.BlockSpec((None, N_HEADS, HEAD_DIM), lambda b, g, pt: (b, 0, 0)),
                   pl.BlockSpec((None, 2, 256), lambda b, g, pt: (b, 0, 0))],
        scratch_shapes=[pltpu.VMEM((2 * CHUNK_ROWS, HEAD_DIM), F32),
                        pltpu.VMEM(((nchunk + 1) * 8, HEAD_DIM), F32),
                        pltpu.VMEM(((nchunk + 1) * 8, HEAD_DIM), F32)])
    return pl.pallas_call(
        functools.partial(_nsa_sample_cmp_kernel, past=past, n_steps=n_steps),
        grid_spec=grid_spec,
        out_shape=[jax.ShapeDtypeStruct((db, N_HEADS, HEAD_DIM), F32),
                   jax.ShapeDtypeStruct((db, 2, 256), F32)],
        compiler_params=pltpu.CompilerParams(vmem_limit_bytes=VMEM_LIMIT),
        name="nsa_sample_cmp",
    )(page_table.reshape(-1), *([pages] * PAGES_PER_STEP), kc_new, pwt, w1, w2, kg, q8)


def _select_kernel(imp_ref, idx_ref, *, t_now, nsel):
    imp = imp_ref[...]
    rows, width = imp.shape
    j = lax.broadcasted_iota(jnp.int32, (rows, width), 1)
    jf = j.astype(F32)
    cur = t_now // SEL_BLOCK
    valid = (j <= cur) & (j < nsel)
    forced = (j == 0) | (valid & (j > cur - N_LOCAL_BLOCKS))
    score = jnp.where(valid, imp + FORCE * forced.astype(F32), NEG)
    col = lax.broadcasted_iota(jnp.int32, (rows, 128), 1)
    out = jnp.full((rows, 128), -1, jnp.int32)
    for it in range(min(TOP_N, nsel)):
        m = jnp.max(score, axis=-1, keepdims=True)
        idx = jnp.min(jnp.where(score == m, jf, 1e9), axis=-1, keepdims=True)
        out = jnp.where(col == it, jnp.where(m > 0.5 * NEG, idx.astype(jnp.int32), -1), out)
        score = jnp.where(jf == idx, -3e38, score)
    idx_ref[...] = out


def _select(imp2d, t_now, nsel):
    rows = imp2d.shape[0]
    return pl.pallas_call(
        functools.partial(_select_kernel, t_now=t_now, nsel=nsel),
        out_shape=jax.ShapeDtypeStruct((rows, 128), jnp.int32),
        name="nsa_sample_select",
    )(imp2d)


def _nsa_sample_sel_kernel(idx_ref, pt_ref, *refs, n_past_blk, n_top):
    blk_refs = refs[:N_KV_HEADS * n_top]
    q_ref, ksn_ref, os_ref = refs[N_KV_HEADS * n_top:]
    del pt_ref
    b = pl.program_id(0)
    q8 = q_ref[...]
    q8f = q8.astype(F32)
    ksn = ksn_ref[...]
    nkeys = n_top * SEL_BLOCK
    lane = lax.broadcasted_iota(jnp.int32, (1, nkeys), 1)
    row = lax.broadcasted_iota(jnp.int32, (N_HEADS, 1), 0)
    first = row < GROUP
    per_head = []
    for h in range(N_KV_HEADS):
        def rows_of(ref, eh):
            return ref[pl.ds(eh, SEL_BLOCK, stride=KV_ROWS), :]
        k_all = jnp.concatenate([rows_of(blk_refs[h * n_top + n], h) for n in range(n_top)],
                                axis=0).astype(BF16)
        v_all = jnp.concatenate([rows_of(blk_refs[h * n_top + n], N_KV_HEADS + h) for n in range(n_top)],
                                axis=0).astype(BF16)
        k_new = ksn[h:h + 1, :]
        v_new = ksn[N_KV_HEADS + h:N_KV_HEADS + h + 1, :]
        slot_ok = jnp.zeros((1, nkeys), F32)
        new_ok = jnp.zeros((1, 1), F32)
        for n in range(n_top):
            ix = idx_ref[(b * N_KV_HEADS + h) * n_top + n]
            past_ok = jnp.where((ix >= 0) & (ix < n_past_blk), 1.0, 0.0)
            slot_ok = jnp.where(_shr(lane, SEL_SHIFT) == n, past_ok, slot_ok)
            new_ok = jnp.maximum(new_ok, jnp.where(ix >= n_past_blk, 1.0, 0.0))
        mask = jnp.broadcast_to(slot_ok > 0.5, (N_HEADS, nkeys))
        new_mask = jnp.broadcast_to(new_ok > 0.5, (N_HEADS, 1))
        s = jnp.where(mask, _dot_nt(q8, k_all), NEG)
        s_new = jnp.where(new_mask, jnp.sum(q8f * k_new, axis=-1, keepdims=True), NEG)
        mx = jnp.maximum(jnp.max(s, axis=-1, keepdims=True), s_new)
        p = jnp.exp(s - mx) * mask.astype(F32)
        p_new = jnp.exp(s_new - mx) * new_mask.astype(F32)
        den = jnp.maximum(jnp.sum(p, axis=-1, keepdims=True) + p_new, 1e-30)
        per_head.append((_dot(p.astype(BF16), v_all) + p_new * v_new) / den)
    os_ref[...] = jnp.where(first, per_head[0], per_head[1])


def _nsa_sample_sel(sel_idx, page_table, cache_sel, q8, ks_new):
    db, n_pages = page_table.shape
    n_past_blk = n_pages * PAGE_SIZE // SEL_BLOCK
    n_top = sel_idx.shape[-1]
    halves = PAGE_SIZE // SEL_BLOCK
    blocks = cache_sel.reshape(cache_sel.shape[0] * halves, SEL_BLOCK * KV_ROWS, HEAD_DIM)

    def blk_spec(h, n):
        def index_map(b, idx, pt):
            blk = jnp.clip(idx[(b * N_KV_HEADS + h) * n_top + n], 0, n_past_blk - 1)
            return (pt[b * n_pages + blk // halves] * halves + blk % halves, 0, 0)
        return pl.BlockSpec((None, SEL_BLOCK * KV_ROWS, HEAD_DIM), index_map)

    grid_spec = pltpu.PrefetchScalarGridSpec(
        num_scalar_prefetch=2,
        grid=(db,),
        in_specs=[blk_spec(h, n) for h in range(N_KV_HEADS) for n in range(n_top)] + [
            pl.BlockSpec((None, N_HEADS, HEAD_DIM), lambda b, idx, pt: (b, 0, 0)),
            pl.BlockSpec((None, KV_ROWS, HEAD_DIM), lambda b, idx, pt: (b, 0, 0))],
        out_specs=pl.BlockSpec((None, N_HEADS, HEAD_DIM), lambda b, idx, pt: (b, 0, 0)))
    return pl.pallas_call(
        functools.partial(_nsa_sample_sel_kernel, n_past_blk=n_past_blk, n_top=n_top),
        grid_spec=grid_spec,
        out_shape=jax.ShapeDtypeStruct((db, N_HEADS, HEAD_DIM), F32),
        compiler_params=pltpu.CompilerParams(vmem_limit_bytes=VMEM_LIMIT),
        name="nsa_sample_sel",
    )(sel_idx.reshape(-1), page_table.reshape(-1), *([blocks] * (N_KV_HEADS * n_top)), q8, ks_new)


def _nsa_sample_win_kernel(win_ref, kwn_ref, q_ref, wout_ref, ow_ref, *, past):
    nrows = win_ref.shape[0]
    wb = nrows // KV_ROWS
    kwn = kwn_ref[...]
    wout_ref[...] = pltpu.roll(win_ref[...], nrows - KV_ROWS, axis=0)
    wout_ref[pl.ds(nrows - KV_ROWS, KV_ROWS), :] = kwn
    q8 = q_ref[...]
    q8f = q8.astype(F32)
    t_now = past
    pos = past - wb + lax.broadcasted_iota(jnp.int32, (1, wb), 1)
    d = t_now - pos
    mask = jnp.broadcast_to((d >= 0) & (d < WINDOW) & (pos >= 0), (N_HEADS, wb))
    row = lax.broadcasted_iota(jnp.int32, (N_HEADS, 1), 0)
    first = row < GROUP
    per_head = []
    for h in range(N_KV_HEADS):
        k_h = win_ref[pl.ds(h, wb, stride=KV_ROWS), :].astype(BF16)
        v_h = win_ref[pl.ds(N_KV_HEADS + h, wb, stride=KV_ROWS), :].astype(BF16)
        s = jnp.where(mask, _dot_nt(q8, k_h), NEG)
        s_new = jnp.sum(q8f * kwn[h:h + 1, :], axis=-1, keepdims=True)
        mx = jnp.maximum(jnp.max(s, axis=-1, keepdims=True), s_new)
        p = jnp.exp(s - mx) * mask.astype(F32)
        p_new = jnp.exp(s_new - mx)
        den = jnp.maximum(jnp.sum(p, axis=-1, keepdims=True) + p_new, 1e-30)
        per_head.append((_dot(p.astype(BF16), v_h) + p_new * kwn[N_KV_HEADS + h:N_KV_HEADS + h + 1, :]) / den)
    ow_ref[...] = jnp.where(first, per_head[0], per_head[1])


def _nsa_sample_win(win_rows, kw_new, q8, past):
    db, nrows, _ = win_rows.shape
    return pl.pallas_call(
        functools.partial(_nsa_sample_win_kernel, past=past),
        grid=(db,),
        in_specs=[pl.BlockSpec((None, nrows, HEAD_DIM), lambda b: (b, 0, 0)),
                  pl.BlockSpec((None, KV_ROWS, HEAD_DIM), lambda b: (b, 0, 0)),
                  pl.BlockSpec((None, N_HEADS, HEAD_DIM), lambda b: (b, 0, 0))],
        out_specs=[pl.BlockSpec((None, nrows, HEAD_DIM), lambda b: (b, 0, 0)),
                   pl.BlockSpec((None, N_HEADS, HEAD_DIM), lambda b: (b, 0, 0))],
        out_shape=[jax.ShapeDtypeStruct((db, nrows, HEAD_DIM), F32),
                   jax.ShapeDtypeStruct((db, N_HEADS, HEAD_DIM), F32)],
        name="nsa_sample_win",
    )(win_rows, kw_new, q8)


def _gate_mix_kernel(oc_ref, os_ref, ow_ref, g_ref, o_ref):
    g = g_ref[...]
    c = lax.broadcasted_iota(jnp.int32, (128, D_MODEL), 0)
    head = _shr(lax.broadcasted_iota(jnp.int32, (128, D_MODEL), 1), HEAD_SHIFT)
    acc = jnp.zeros(o_ref.shape, F32)
    for r, ref in enumerate((oc_ref, os_ref, ow_ref)):
        expand = (c == r * N_HEADS + head).astype(F32)
        acc = acc + ref[...] * jnp.dot(g, expand, precision=lax.Precision.HIGHEST, preferred_element_type=F32)
    o_ref[...] = acc


def _merge_kernel(x_ref, a_ref, ra_ref, b_ref, on_ref, gate_ref, w_ref, y_ref):
    u = a_ref[...] * ra_ref[...] + b_ref[...] * on_ref[...]
    y_ref[...] = x_ref[...] + gate_ref[0] * _dot(u.astype(BF16), w_ref[...])


def _merge(x2d, a, ra, b, o_nsa, gate, w_out, tm, rows_per_mod):
    m = x2d.shape[0]
    tiles_per_mod = rows_per_mod // tm
    row_spec = pl.BlockSpec((tm, D_MODEL), lambda i: (i, 0))
    return pl.pallas_call(
        _merge_kernel,
        grid=(m // tm,),
        in_specs=[row_spec] * 5 + [
            pl.BlockSpec((1, gate.shape[1], D_MODEL), lambda i: (i // tiles_per_mod, 0, 0)),
            pl.BlockSpec((D_MODEL, D_MODEL), lambda i: (0, 0))],
        out_specs=row_spec,
        out_shape=jax.ShapeDtypeStruct((m, D_MODEL), F32),
        compiler_params=pltpu.CompilerParams(vmem_limit_bytes=VMEM_LIMIT),
        name="merge_out_proj",
    )(x2d, a, ra, b, o_nsa, gate, w_out)


def _rearranged_w_in(w_in):
    sizes = (D_MODEL, D_MODEL, D_MODEL, KV_W, KV_W, KV_W, 3 * N_HEADS, D_MODEL, D_MODEL, D_MODEL)
    offs = [0]
    for s in sizes:
        offs.append(offs[-1] + s)
    xr, zr, q, kc, ks, kw, bg, zn, ga, gb = [w_in[:, offs[i]:offs[i + 1]] for i in range(len(sizes))]
    pad = jnp.zeros((D_MODEL, C_END - C_BG - 3 * N_HEADS), w_in.dtype)
    return jnp.concatenate([xr, zr, q, kc, ks, kw, zn, ga, gb, bg, pad], axis=1).astype(BF16)


def kernel(x_prompt, x_sample, c_prompt, c_sample, state_conv, state_rglru, cache_cmp_kv, cache_sel_kv,
           state_win_kv, page_table, norm_g, w_ada, b_ada, w_in, conv_w, conv_b, rg_wa, rg_ba, rg_wx, rg_bx,
           rg_lambda, q_norm_g, k_norm_g, cmp_pool_w, cmp_w1, cmp_w2, w_out):
    depth = norm_g.shape[0]
    assert depth == 1 and x_sample.shape[1] == 1
    bsz, seq, _ = x_prompt.shape
    db = x_sample.shape[0]
    n_pages = page_table.shape[1]
    past = n_pages * PAGE_SIZE
    layer = 0

    w_cat = _rearranged_w_in(w_in[layer])
    w_out_b = w_out[layer].astype(BF16)
    wa_b = rg_wa[layer].astype(BF16)
    wx_b = rg_wx[layer].astype(BF16)
    w1_b = cmp_w1[layer].astype(BF16)
    w2_b = cmp_w2[layer].astype(BF16)
    row = lambda v: v.reshape(1, -1)
    rg_args = (conv_w[layer], row(conv_b[layer]), wa_b, row(rg_ba[layer]), wx_b, row(rg_bx[layer]),
               row(rg_lambda[layer]))
    praw = jnp.broadcast_to(
        cmp_pool_w[layer].reshape(2, 2, CMP_STRIDE).transpose(1, 2, 0)[:, :, :, None, None],
        (2, CMP_STRIDE, 2, N_KV_HEADS, HEAD_DIM)).reshape(2 * CHUNK_ROWS, HEAD_DIM)
    kg = k_norm_g[layer]

    mod = _modulation(jnp.concatenate([c_prompt, c_sample], axis=0), w_ada[layer], b_ada[layer])
    shift, scale, gate = mod[:, :D_MODEL], mod[:, D_MODEL:2 * D_MODEL], mod[:, 2 * D_MODEL:]

    xp2 = x_prompt.reshape(bsz * seq, D_MODEL)
    pm = lambda v: v[:bsz].reshape(bsz, 1, D_MODEL)
    (xr_p, a_p, b_p, q_p, kc_p, ks_p, kw_p, ksb_p, kwb_p, g_p) = _project(
        xp2, pm(shift), pm(scale), norm_g[layer], w_cat, q_norm_g[layer], kg, tm=256, rows_per_mod=seq)
    ra_p, h_p = _rglru_prompt(xr_p, bsz, seq, *rg_args)
    ck_p, cv_p = _compress_prompt(kc_p, praw, w1_b, w2_b, kg, bsz, seq)
    on_p = _nsa_prompt(q_p, ck_p, cv_p, ksb_p, kwb_p, g_p, bsz, seq)
    y_p = _merge(xp2, a_p, ra_p, b_p, on_p, pm(gate), w_out_b, tm=256, rows_per_mod=seq)

    xs2 = x_sample.reshape(db, D_MODEL)
    sm = lambda v: v[bsz:].reshape(1, db, D_MODEL)
    (xr_s, a_s, b_s, q_s, kc_s, ks_s, kw_s, _, _, g_s) = _project(
        xs2, sm(shift), sm(scale), norm_g[layer], w_cat, q_norm_g[layer], kg, tm=db, rows_per_mod=db)
    h_s = _rglru_step(xr_s, state_conv[layer].reshape(db, (CONV_W - 1) * D_MODEL), state_rglru[layer], *rg_args)
    q8 = q_s.reshape(db, N_HEADS, HEAD_DIM)
    new_rows = lambda v: v.reshape(db, KV_ROWS, HEAD_DIM)
    n_phys = cache_cmp_kv.shape[1]
    oc_s, imp = _nsa_sample_cmp(page_table, cache_cmp_kv.reshape(depth * n_phys, PAGE_SIZE * KV_ROWS, HEAD_DIM),
                                new_rows(kc_s), praw, w1_b, w2_b, kg, q8)
    nsel = past // SEL_BLOCK + 1
    sel_idx = _select(imp.reshape(db * N_KV_HEADS, 256), past, nsel)[:, :min(TOP_N, nsel)]
    os_s = _nsa_sample_sel(sel_idx, page_table,
                           cache_sel_kv.reshape(depth * n_phys, PAGE_SIZE * KV_ROWS, HEAD_DIM), q8, new_rows(ks_s))
    wb = state_win_kv.shape[2]
    win_s, ow_s = _nsa_sample_win(state_win_kv.reshape(depth * db, wb * KV_ROWS, HEAD_DIM), new_rows(kw_s), q8, past)
    on_s = pl.pallas_call(
        _gate_mix_kernel, out_shape=jax.ShapeDtypeStruct((db, D_MODEL), F32), name="nsa_sample_mix",
    )(oc_s.reshape(db, D_MODEL), os_s.reshape(db, D_MODEL), ow_s.reshape(db, D_MODEL), g_s)
    y_s = _merge(xs2, a_s, h_s, b_s, on_s, sm(gate), w_out_b, tm=db, rows_per_mod=db)

    kv_shape = (2, N_KV_HEADS, HEAD_DIM)
    xr_p3 = xr_p.reshape(bsz, seq, D_MODEL)
    conv_prompt = xr_p3[:, seq - (CONV_W - 1):][None]
    conv_sample = jnp.concatenate([state_conv[layer][:, 1:], xr_s[:, None, :]], axis=1)[None]
    win_len = min(WINDOW, seq)
    return (y_p.reshape(bsz, seq, D_MODEL), y_s.reshape(db, 1, D_MODEL),
            conv_prompt, conv_sample,
            h_p.reshape(1, bsz, D_MODEL), h_s.reshape(1, db, D_MODEL),
            kc_p.reshape(1, bsz, seq, *kv_shape), kc_s.reshape(1, db, 1, *kv_shape),
            ks_p.reshape(1, bsz, seq, *kv_shape), ks_s.reshape(1, db, 1, *kv_shape),
            kw_p.reshape(bsz, seq * KV_ROWS, HEAD_DIM)[:, (seq - win_len) * KV_ROWS:].reshape(
                1, bsz, win_len, *kv_shape),
            win_s.reshape(1, db, wb, *kv_shape))
```

```python
import functools

import jax
import jax.numpy as jnp
from jax import lax
from jax.experimental import pallas as pl
from jax.experimental.pallas import tpu as pltpu

F32 = jnp.float32
BF16 = jnp.bfloat16

D_MODEL = 1024
RG_BLOCKS = 8
RG_BW = D_MODEL // RG_BLOCKS
RG_C = 8.0
CONV_W = 4
N_HEADS = 8
N_KV_HEADS = 2
GROUP = N_HEADS // N_KV_HEADS
HEAD_DIM = D_MODEL // N_HEADS
KV_W = 2 * N_KV_HEADS * HEAD_DIM
KV_ROWS = 2 * N_KV_HEADS
CMP_BLOCK = 32
CMP_STRIDE = 16
SEL_BLOCK = 64
TOP_N = 16
N_LOCAL_BLOCKS = 2
WINDOW = 512
Q_BLOCK = 128
PAGE_SIZE = 128
EPS = 1e-6
NEG = -1e30
FORCE = 1e4

C_XR, C_ZR, C_Q, C_KC, C_KS, C_KW, C_ZN, C_GA, C_GB, C_BG, C_END = (
    0, 1024, 2048, 3072, 3584, 4096, 4608, 5632, 6656, 7680, 7808)
VMEM_LIMIT = 56 * 1024 * 1024
PAGES_PER_STEP = 16


SEL_SHIFT = SEL_BLOCK.bit_length() - 1
HEAD_SHIFT = HEAD_DIM.bit_length() - 1


def _shr(x, k):
    return lax.shift_right_arithmetic(x, jnp.int32(k))


def _sigmoid(x):
    return jax.nn.sigmoid(x)


def _dot(a, b):
    return jnp.dot(a, b, preferred_element_type=F32)


def _dot_nt(a, b):
    return lax.dot_general(a, b, (((1,), (1,)), ((), ())), preferred_element_type=F32)


def _rms_rows(x, g):
    return x * lax.rsqrt(jnp.mean(x * x, axis=-1, keepdims=True) + EPS) * g


def _mod_kernel(c_ref, w_ref, b_ref, o_ref):
    c = c_ref[...]
    s = c * _sigmoid(c)
    o_ref[...] = _dot(s.astype(BF16), w_ref[...].astype(BF16)) + b_ref[...]


def _modulation(c_all, w_ada, b_ada):
    n = c_all.shape[0]
    return pl.pallas_call(
        _mod_kernel,
        grid=(3,),
        in_specs=[pl.BlockSpec((n, D_MODEL), lambda j: (0, 0)),
                  pl.BlockSpec((D_MODEL, D_MODEL), lambda j: (0, j)),
                  pl.BlockSpec((1, D_MODEL), lambda j: (0, j))],
        out_specs=pl.BlockSpec((n, D_MODEL), lambda j: (0, j)),
        out_shape=jax.ShapeDtypeStruct((n, 3 * D_MODEL), F32),
        name="adaln_mod",
    )(c_all, w_ada, b_ada.reshape(1, 3 * D_MODEL))


def _proj_kernel(x_ref, shift_ref, scale_ref, ng_ref, w_ref, qg_ref, kg_ref,
                 xr_ref, a_ref, b_ref, q_ref, kc_ref, ks_ref, kw_ref, skb_ref, svt_ref, wkb_ref, wvt_ref, g_ref):
    x = x_ref[...]
    h = _rms_rows(x, ng_ref[...])
    h = h * (1.0 + scale_ref[0]) + shift_ref[0]
    hb = h.astype(BF16)

    def mm(lo, hi):
        return _dot(hb, w_ref[:, lo:hi])

    xr_ref[...] = mm(C_XR, C_ZR)
    zr = mm(C_ZR, C_Q)
    ga = mm(C_GA, C_GB)
    a_ref[...] = _sigmoid(ga) * (zr * _sigmoid(zr))
    zn = mm(C_ZN, C_GA)
    gb = mm(C_GB, C_BG)
    b_ref[...] = _sigmoid(gb) * (zn * _sigmoid(zn))
    q = mm(C_Q, C_KC)
    for hd in range(N_HEADS):
        sl = slice(hd * HEAD_DIM, (hd + 1) * HEAD_DIM)
        q_ref[:, sl] = (_rms_rows(q[:, sl], qg_ref[...]) * (HEAD_DIM ** -0.5)).astype(BF16)
    tm = x.shape[0]

    def store_kv(o_ref, eh, val):
        o_ref[pl.ds(eh, tm, stride=KV_ROWS), :] = val

    kc = mm(C_KC, C_KS)
    for eh in range(KV_ROWS):
        store_kv(kc_ref, eh, kc[:, eh * HEAD_DIM:(eh + 1) * HEAD_DIM])
    for lo, hi, o_ref, kb_ref, vt_ref, gi in ((C_KS, C_KW, ks_ref, skb_ref, svt_ref, 1),
                                              (C_KW, C_ZN, kw_ref, wkb_ref, wvt_ref, 2)):
        kv = mm(lo, hi)
        for hd in range(N_KV_HEADS):
            sl = slice(hd * HEAD_DIM, (hd + 1) * HEAD_DIM)
            kn = _rms_rows(kv[:, sl], kg_ref[gi:gi + 1, :])
            store_kv(o_ref, hd, kn)
            kb_ref[:, sl] = kn.astype(BF16)
            v = kv[:, (N_KV_HEADS + hd) * HEAD_DIM:(N_KV_HEADS + hd + 1) * HEAD_DIM]
            store_kv(o_ref, N_KV_HEADS + hd, v)
            v_t = v.T.astype(BF16)
            for j in range(tm // Q_BLOCK):
                vt_ref[j, sl, :] = v_t[:, j * Q_BLOCK:(j + 1) * Q_BLOCK]
    g_ref[...] = _sigmoid(mm(C_BG, C_END))


def _project(x2d, shift, scale, norm_g, w_cat, q_norm_g, k_norm_g, tm, rows_per_mod):
    m = x2d.shape[0]
    tiles_per_mod = rows_per_mod // tm
    mod_rows = shift.shape[1]
    mod_spec = pl.BlockSpec((1, mod_rows, D_MODEL), lambda i: (i // tiles_per_mod, 0, 0))

    def row_spec(width, mult=1):
        return pl.BlockSpec((tm * mult, width), lambda i: (i, 0))

    def const_spec(shape):
        return pl.BlockSpec(shape, lambda i: (0,) * len(shape))

    kv_heads_w = N_KV_HEADS * HEAD_DIM
    k_bf16 = (pl.BlockSpec((tm, kv_heads_w), lambda i: (i, 0)), jax.ShapeDtypeStruct((m, kv_heads_w), BF16))
    vt_bf16 = (pl.BlockSpec((tm // Q_BLOCK, kv_heads_w, Q_BLOCK), lambda i: (i, 0, 0)),
               jax.ShapeDtypeStruct((m // Q_BLOCK, kv_heads_w, Q_BLOCK), BF16))

    def rows(width, dtype, mult=1):
        return row_spec(width, mult), jax.ShapeDtypeStruct((m * mult, width), dtype)

    outs = (rows(D_MODEL, F32), rows(D_MODEL, F32), rows(D_MODEL, F32), rows(D_MODEL, BF16),
            rows(HEAD_DIM, F32, KV_ROWS), rows(HEAD_DIM, F32, KV_ROWS), rows(HEAD_DIM, F32, KV_ROWS),
            k_bf16, vt_bf16, k_bf16, vt_bf16, rows(128, F32))
    return pl.pallas_call(
        _proj_kernel,
        grid=(m // tm,),
        in_specs=[row_spec(D_MODEL), mod_spec, mod_spec, const_spec((1, D_MODEL)),
                  pl.BlockSpec((D_MODEL, C_END), lambda i: (0, 0), pipeline_mode=pl.Buffered(1)),
                  const_spec((1, HEAD_DIM)), const_spec((3, HEAD_DIM))],
        out_specs=[spec for spec, _ in outs],
        out_shape=[shape for _, shape in outs],
        compiler_params=pltpu.CompilerParams(vmem_limit_bytes=VMEM_LIMIT),
        name="in_proj",
    )(x2d, shift, scale, norm_g.reshape(1, D_MODEL), w_cat, q_norm_g.reshape(1, HEAD_DIM), k_norm_g)


def _softplus(z):
    return jnp.maximum(z, 0.0) + jnp.log1p(jnp.exp(-jnp.abs(z)))


def _rglru_coeffs(xc, wa_ref, ba_ref, wx_ref, bx_ref, lam_ref, a_out, b_out):
    xcb = xc.astype(BF16)
    sp = _softplus(-lam_ref[...])
    for k in range(RG_BLOCKS):
        sl = slice(k * RG_BW, (k + 1) * RG_BW)
        r = _sigmoid(_dot(xcb[:, sl], wa_ref[k]) + ba_ref[:, sl])
        i = _sigmoid(_dot(xcb[:, sl], wx_ref[k]) + bx_ref[:, sl])
        log_a = -RG_C * r * sp[:, sl]
        a = jnp.exp(log_a)
        a_out[:, sl] = a
        b_out[:, sl] = jnp.sqrt(-jnp.tanh(log_a) * (a * a + 1.0)) * i * xc[:, sl]


def _rglru_prompt_kernel(xr_ref, cw_ref, cb_ref, wa_ref, ba_ref, wx_ref, bx_ref, lam_ref,
                         ra_ref, hl_ref, ext_s, a_s, b_s, h_s):
    t_len = xr_ref.shape[0]

    @pl.when(pl.program_id(1) == 0)
    def _():
        ext_s[0:8, :] = jnp.zeros((8, D_MODEL), F32)
        h_s[...] = jnp.zeros((1, D_MODEL), F32)

    x = xr_ref[...]
    ext_s[8:8 + t_len, :] = x
    xc = ext_s[pl.ds(5, t_len), :] * cw_ref[0:1, :] + cb_ref[...]
    xc = xc + ext_s[pl.ds(6, t_len), :] * cw_ref[1:2, :]
    xc = xc + ext_s[pl.ds(7, t_len), :] * cw_ref[2:3, :]
    xc = xc + x * cw_ref[3:4, :]
    ext_s[0:8, :] = x[t_len - 8:t_len, :]
    _rglru_coeffs(xc, wa_ref, ba_ref, wx_ref, bx_ref, lam_ref, a_s, b_s)

    def body(t, h):
        h = a_s[pl.ds(t, 1), :] * h + b_s[pl.ds(t, 1), :]
        ra_ref[pl.ds(t, 1), :] = h
        return h

    h = lax.fori_loop(0, t_len, body, h_s[...], unroll=8)
    h_s[...] = h
    hl_ref[...] = h


def _rglru_prompt(xr, bsz, seq, cw, cb, wa, ba, wx, bx, lam, t_chunk=512):
    nchunk = seq // t_chunk

    def const_spec(shape):
        return pl.BlockSpec(shape, lambda b, c: (0,) * len(shape))

    return pl.pallas_call(
        _rglru_prompt_kernel,
        grid=(bsz, nchunk),
        in_specs=[pl.BlockSpec((t_chunk, D_MODEL), lambda b, c: (b * nchunk + c, 0)),
                  const_spec((CONV_W, D_MODEL)), const_spec((1, D_MODEL)),
                  const_spec((RG_BLOCKS, RG_BW, RG_BW)), const_spec((1, D_MODEL)),
                  const_spec((RG_BLOCKS, RG_BW, RG_BW)), const_spec((1, D_MODEL)),
                  const_spec((1, D_MODEL))],
        out_specs=[pl.BlockSpec((t_chunk, D_MODEL), lambda b, c: (b * nchunk + c, 0)),
                   pl.BlockSpec((None, 1, D_MODEL), lambda b, c: (b, 0, 0))],
        out_shape=[jax.ShapeDtypeStruct((bsz * seq, D_MODEL), F32),
                   jax.ShapeDtypeStruct((bsz, 1, D_MODEL), F32)],
        scratch_shapes=[pltpu.VMEM((t_chunk + 8, D_MODEL), F32), pltpu.VMEM((t_chunk, D_MODEL), F32),
                        pltpu.VMEM((t_chunk, D_MODEL), F32), pltpu.VMEM((1, D_MODEL), F32)],
        compiler_params=pltpu.CompilerParams(vmem_limit_bytes=VMEM_LIMIT),
        name="rglru_prompt",
    )(xr, cw, cb, wa, ba, wx, bx, lam)


def _rglru_step_kernel(xr_ref, cbuf_ref, h0_ref, cw_ref, cb_ref, wa_ref, ba_ref, wx_ref, bx_ref, lam_ref,
                       h_ref, a_s, b_s):
    xc = cbuf_ref[:, 0:D_MODEL] * cw_ref[0:1, :] + cb_ref[...]
    xc = xc + cbuf_ref[:, D_MODEL:2 * D_MODEL] * cw_ref[1:2, :]
    xc = xc + cbuf_ref[:, 2 * D_MODEL:3 * D_MODEL] * cw_ref[2:3, :]
    xc = xc + xr_ref[...] * cw_ref[3:4, :]
    _rglru_coeffs(xc, wa_ref, ba_ref, wx_ref, bx_ref, lam_ref, a_s, b_s)
    h_ref[...] = a_s[...] * h0_ref[...] + b_s[...]


def _rglru_step(xr, cbuf, h0, cw, cb, wa, ba, wx, bx, lam):
    n = xr.shape[0]
    return pl.pallas_call(
        _rglru_step_kernel,
        out_shape=jax.ShapeDtypeStruct((n, D_MODEL), F32),
        scratch_shapes=[pltpu.VMEM((n, D_MODEL), F32), pltpu.VMEM((n, D_MODEL), F32)],
        name="rglru_step",
    )(xr, cbuf, h0, cw, cb, wa, ba, wx, bx, lam)


CHUNK_ROWS = CMP_STRIDE * KV_ROWS


def _pool_weight_rows(praw_ref):
    praw = praw_ref[...]
    row = lax.broadcasted_iota(jnp.int32, praw.shape, 0)
    is_k = (row & (KV_ROWS - 1)) < N_KV_HEADS
    m_k = jnp.max(jnp.where(is_k, praw, -3e38), axis=0, keepdims=True)
    m_v = jnp.max(jnp.where(is_k, -3e38, praw), axis=0, keepdims=True)
    ex = jnp.exp(praw - jnp.where(is_k, m_k, m_v))
    s_k = jnp.sum(jnp.where(is_k, ex, 0.0), axis=0, keepdims=True) * (1.0 / N_KV_HEADS)
    s_v = jnp.sum(jnp.where(is_k, 0.0, ex), axis=0, keepdims=True) * (1.0 / N_KV_HEADS)
    return ex / jnp.where(is_k, s_k, s_v)


def _pool_rows(x, wv):
    n = x.shape[0] // CHUNK_ROWS
    x4 = x.reshape(n, CHUNK_ROWS // 8, 8, HEAD_DIM)
    w4 = wv.reshape(2, CHUNK_ROWS // 8, 8, HEAD_DIM)
    p0 = jnp.sum(x4 * w4[0][None], axis=1)
    p1 = jnp.sum(x4 * w4[1][None], axis=1)
    return p0.reshape(n * 8, HEAD_DIM), p1.reshape(n * 8, HEAD_DIM)


def _pooled_head(p0_s, p1_s, eh, nc):
    def col(ref, start):
        return ref[pl.ds(start, nc, stride=8), :]
    return (col(p0_s, eh) + col(p0_s, eh + KV_ROWS)) + (col(p1_s, 8 + eh) + col(p1_s, 8 + eh + KV_ROWS))


def _compress_mlp(p0_s, p1_s, nc, w1_ref, w2_ref, kg_ref):
    outs = []
    for e in range(2):
        per_head = []
        for hd in range(N_KV_HEADS):
            p = _pooled_head(p0_s, p1_s, e * N_KV_HEADS + hd, nc)
            hid = _dot(p.astype(BF16), w1_ref[e])
            hid = hid * _sigmoid(hid)
            comp = p + _dot(hid.astype(BF16), w2_ref[e])
            if e == 0:
                comp = _rms_rows(comp, kg_ref[0:1, :])
            per_head.append(comp)
        outs.append(jnp.concatenate(per_head, axis=1))
    return outs[0], outs[1]


def _masked_softmax_parts(s, mask):
    sm = jnp.where(mask, s, NEG)
    p = jnp.exp(sm - jnp.max(sm, axis=-1, keepdims=True)) * mask.astype(F32)
    return p, jnp.maximum(jnp.sum(p, axis=-1, keepdims=True), 1e-30)


POOL_SLAB = 8


def _compress_prompt_kernel(x_ref, praw_ref, w1_ref, w2_ref, kg_ref, ck_ref, cvt_ref, p0_s, p1_s):
    nch = x_ref.shape[0] // CHUNK_ROWS
    wv = _pool_weight_rows(praw_ref)

    def slab(i, _):
        x = x_ref[pl.ds(pl.multiple_of(i * (POOL_SLAB * CHUNK_ROWS), POOL_SLAB * CHUNK_ROWS),
                        POOL_SLAB * CHUNK_ROWS), :]
        p0, p1 = _pool_rows(x, wv)
        rows = pl.ds(pl.multiple_of(i * (POOL_SLAB * 8), POOL_SLAB * 8), POOL_SLAB * 8)
        p0_s[rows, :] = p0
        p1_s[rows, :] = p1
        return 0

    lax.fori_loop(0, nch // POOL_SLAB, slab, 0)
    p1_s[pl.ds(nch * 8, 8), :] = jnp.zeros((8, HEAD_DIM), F32)
    ck, cv = _compress_mlp(p0_s, p1_s, nch, w1_ref, w2_ref, kg_ref)
    ck_ref[...] = ck
    cvt_ref[...] = cv.T


def _compress_prompt(kc_rows, praw, w1, w2, kg, bsz, seq):
    nch = seq // CMP_STRIDE

    def const_spec(shape):
        return pl.BlockSpec(shape, lambda b: (0,) * len(shape))

    return pl.pallas_call(
        _compress_prompt_kernel,
        grid=(bsz,),
        in_specs=[pl.BlockSpec((seq * KV_ROWS, HEAD_DIM), lambda b: (b, 0)),
                  const_spec((2 * CHUNK_ROWS, HEAD_DIM)), const_spec((2, HEAD_DIM, HEAD_DIM)),
                  const_spec((2, HEAD_DIM, HEAD_DIM)), const_spec((3, HEAD_DIM))],
        out_specs=[pl.BlockSpec((None, nch, 2 * HEAD_DIM), lambda b: (b, 0, 0)),
                   pl.BlockSpec((None, 2 * HEAD_DIM, nch), lambda b: (b, 0, 0))],
        out_shape=[jax.ShapeDtypeStruct((bsz, nch, 2 * HEAD_DIM), F32),
                   jax.ShapeDtypeStruct((bsz, 2 * HEAD_DIM, nch), F32)],
        scratch_shapes=[pltpu.VMEM(((nch + 1) * 8, HEAD_DIM), F32)] * 2,
        name="compress_prompt",
    )(kc_rows, praw, w1, w2, kg)


GQ = GROUP * Q_BLOCK
WIN_BLOCKS = WINDOW // Q_BLOCK + 1


def _bias_groups(s_t, bias):
    return jnp.concatenate([s_t[:, g * Q_BLOCK:(g + 1) * Q_BLOCK] + bias for g in range(GROUP)], axis=1)


def _fold8(x, op):
    return op(x.reshape(x.shape[0] // 8, 8, x.shape[1]), axis=0)


def _finish_t(o_t, m, l8):
    den = jnp.maximum(jnp.sum(l8, axis=0, keepdims=True), 1e-30)
    return jnp.where(m > 0.5 * NEG, o_t / den, 0.0)


def _nsa_prompt_kernel(q_ref, ck_ref, cvt_ref, sk_ref, svt_ref, wk_ref, wvt_ref, g_ref, o_ref,
                       bias_s, ps_s, pw_s):
    qi = pl.program_id(1)
    nq = Q_BLOCK
    ncp = ck_ref.shape[0]
    nkb = svt_ref.shape[0]
    nsel = nkb * (Q_BLOCK // SEL_BLOCK)
    n_top = min(TOP_N, nsel)
    start = qi * nq
    t_row = start + lax.broadcasted_iota(jnp.int32, (1, nq), 1)
    gates_t = g_ref[...].T

    n_sub = lax.broadcasted_iota(jnp.int32, (ncp, nq), 0)
    cmp_mask = (n_sub * CMP_STRIDE + (CMP_BLOCK - 1) <= t_row) & (n_sub < ncp - 1)
    jj = lax.broadcasted_iota(jnp.int32, (128, ncp), 0)
    nn = lax.broadcasted_iota(jnp.int32, (128, ncp), 1)
    ov_t = ((nn * CMP_STRIDE <= jj * SEL_BLOCK + SEL_BLOCK - 1)
            & (nn * CMP_STRIDE + CMP_BLOCK - 1 >= jj * SEL_BLOCK)
            & (nn < ncp - 1) & (jj < nsel)).astype(F32)
    j_col = lax.broadcasted_iota(jnp.int32, (128, nq), 0)
    cur = _shr(t_row, SEL_SHIFT)
    valid_t = (j_col <= cur) & (j_col < nsel)
    forced_t = (j_col == 0) | (valid_t & (j_col > cur - N_LOCAL_BLOCKS))
    key_sub = lax.broadcasted_iota(jnp.int32, (Q_BLOCK, nq), 0)
    q_lane = lax.broadcasted_iota(jnp.int32, (Q_BLOCK, nq), 1)
    win_bias = {0: jnp.where(q_lane < key_sub, 0.0, NEG), WIN_BLOCKS - 1: jnp.where(q_lane >= key_sub, 0.0, NEG)}

    for h in range(N_KV_HEADS):
        qh = jnp.concatenate(
            [q_ref[:, (h * GROUP + g) * HEAD_DIM:(h * GROUP + g + 1) * HEAD_DIM] for g in range(GROUP)], axis=0)
        hs = slice(h * HEAD_DIM, (h + 1) * HEAD_DIM)

        s_c = _dot_nt(ck_ref[:, hs].astype(BF16), qh)
        p_groups = []
        for g in range(GROUP):
            sm = jnp.where(cmp_mask, s_c[:, g * nq:(g + 1) * nq], NEG)
            p = jnp.exp(sm - jnp.max(sm, axis=0, keepdims=True)) * cmp_mask.astype(F32)
            p_groups.append(p / jnp.maximum(jnp.sum(p, axis=0, keepdims=True), 1e-30))
        o_c = _dot(cvt_ref[hs, :].astype(BF16), jnp.concatenate(p_groups, axis=1).astype(BF16))
        psum = p_groups[0] + p_groups[1] + p_groups[2] + p_groups[3]
        imp_t = jnp.dot(ov_t, psum, precision=lax.Precision.HIGHEST, preferred_element_type=F32)
        score_t = jnp.where(valid_t, imp_t + FORCE * forced_t.astype(F32), NEG)
        rank = jnp.zeros((128, nq), jnp.int32)
        for k in range(nsel):
            rk = score_t[k:k + 1, :]
            beats = (rk > score_t) | ((rk == score_t) & (j_col > k))
            rank = rank + beats.astype(jnp.int32)
        sel_t = (rank < n_top) & (score_t > 0.5 * NEG)
        sel_f = sel_t.astype(F32)

        def sel_branch(nblk):
            def run():
                for kb in range(nblk):
                    lo = sel_f[2 * kb:2 * kb + 1, :]
                    hi = sel_f[2 * kb + 1:2 * kb + 2, :]
                    picked = jnp.where(key_sub < SEL_BLOCK, lo, hi) > 0.5
                    bias_s[kb] = jnp.where(picked & (kb * Q_BLOCK + key_sub <= t_row), 0.0, NEG)

                def scores(kb):
                    return _bias_groups(_dot_nt(sk_ref[kb * Q_BLOCK:(kb + 1) * Q_BLOCK, hs], qh), bias_s[kb])

                m8 = jnp.full((8, GQ), NEG, F32)
                for kb in range(nblk):
                    m8 = jnp.maximum(m8, _fold8(scores(kb), jnp.max))
                m_s = jnp.max(m8, axis=0, keepdims=True)
                l_s = jnp.zeros((8, GQ), F32)
                for kb in range(nblk):
                    p = jnp.exp(scores(kb) - m_s)
                    ps_s[kb * Q_BLOCK:(kb + 1) * Q_BLOCK, :] = p.astype(BF16)
                    l_s = l_s + _fold8(p, jnp.sum)
                v_sel = jnp.concatenate([svt_ref[kb, hs, :] for kb in range(nblk)], axis=1)
                return _finish_t(_dot(v_sel, ps_s[0:nblk * Q_BLOCK, :]), m_s, l_s)
            return run

        o_s = lax.cond(qi < nkb // 2, sel_branch(nkb // 2), sel_branch(nkb))

        def win_scores(r):
            kb = qi - (WIN_BLOCKS - 1) + r
            kbc = jnp.maximum(kb, 0)
            off = pl.multiple_of(kbc * Q_BLOCK, Q_BLOCK)
            s_t = _dot_nt(wk_ref[pl.ds(off, Q_BLOCK), hs], qh) + jnp.where(kb < 0, NEG, 0.0)
            if r in win_bias:
                s_t = _bias_groups(s_t, win_bias[r])
            return s_t, kbc

        m8 = jnp.full((8, GQ), NEG, F32)
        for r in range(WIN_BLOCKS):
            m8 = jnp.maximum(m8, _fold8(win_scores(r)[0], jnp.max))
        m_w = jnp.max(m8, axis=0, keepdims=True)
        l_w = jnp.zeros((8, GQ), F32)
        v_blocks = []
        for r in range(WIN_BLOCKS):
            s_t, kbc = win_scores(r)
            p = jnp.exp(s_t - m_w)
            pw_s[r * Q_BLOCK:(r + 1) * Q_BLOCK, :] = p.astype(BF16)
            l_w = l_w + _fold8(p, jnp.sum)
            v_blocks.append(wvt_ref[kbc, hs, :])
        o_w = _finish_t(_dot(jnp.concatenate(v_blocks, axis=1), pw_s[...]), m_w, l_w)

        for g in range(GROUP):
            head = h * GROUP + g
            cs = slice(g * nq, (g + 1) * nq)
            o = (o_c[:, cs] * gates_t[head:head + 1, :]
                 + o_s[:, cs] * gates_t[N_HEADS + head:N_HEADS + head + 1, :]
                 + o_w[:, cs] * gates_t[2 * N_HEADS + head:2 * N_HEADS + head + 1, :])
            o_ref[:, head * HEAD_DIM:(head + 1) * HEAD_DIM] = o.T


def _nsa_prompt(q, ck, cvt, sk, svt, wk, wvt, gates, bsz, seq):
    nqb = seq // Q_BLOCK
    ncp = ck.shape[1]
    kv_heads_w = N_KV_HEADS * HEAD_DIM
    k_spec = pl.BlockSpec((seq, kv_heads_w), lambda b, i: (b, 0))
    vt_spec = pl.BlockSpec((nqb, kv_heads_w, Q_BLOCK), lambda b, i: (b, 0, 0))
    return pl.pallas_call(
        _nsa_prompt_kernel,
        grid=(bsz, nqb),
        in_specs=[pl.BlockSpec((Q_BLOCK, D_MODEL), lambda b, i: (b * nqb + i, 0)),
                  pl.BlockSpec((None, ncp, kv_heads_w), lambda b, i: (b, 0, 0)),
                  pl.BlockSpec((None, kv_heads_w, ncp), lambda b, i: (b, 0, 0)),
                  k_spec, vt_spec, k_spec, vt_spec,
                  pl.BlockSpec((Q_BLOCK, 128), lambda b, i: (b * nqb + i, 0))],
        out_specs=pl.BlockSpec((Q_BLOCK, D_MODEL), lambda b, i: (b * nqb + i, 0)),
        out_shape=jax.ShapeDtypeStruct((bsz * seq, D_MODEL), F32),
        scratch_shapes=[pltpu.VMEM((nqb, Q_BLOCK, Q_BLOCK), F32),
                        pltpu.VMEM((seq, GQ), BF16),
                        pltpu.VMEM((WIN_BLOCKS * Q_BLOCK, GQ), BF16)],
        compiler_params=pltpu.CompilerParams(vmem_limit_bytes=VMEM_LIMIT),
        name="nsa_prompt",
    )(q, ck, cvt, sk, svt, wk, wvt, gates)


def _nsa_sample_cmp_kernel(pt_ref, *refs, past, n_steps):
    page_refs = refs[:PAGES_PER_STEP]
    (kcn_ref, praw_ref, w1_ref, w2_ref, kg_ref, q_ref, oc_ref, imp_ref, wv_s, p0_s, p1_s) = refs[PAGES_PER_STEP:]
    del pt_ref
    g = pl.program_id(1)
    chunks_per_page = PAGE_SIZE // CMP_STRIDE
    nchunk = past // CMP_STRIDE
    nsel = past // SEL_BLOCK + 1
    t_now = past

    @pl.when(g == 0)
    def _():
        wv_s[...] = _pool_weight_rows(praw_ref)

    wv = wv_s[...]
    for k in range(PAGES_PER_STEP):
        p0, p1 = _pool_rows(page_refs[k][...], wv)
        rows = pl.ds(pl.multiple_of((g * PAGES_PER_STEP + k) * (chunks_per_page * 8), chunks_per_page * 8),
                     chunks_per_page * 8)
        p0_s[rows, :] = p0
        p1_s[rows, :] = p1

    @pl.when(g == n_steps - 1)
    def _():
        p1_s[pl.ds(nchunk * 8, KV_ROWS), :] = kcn_ref[...] * wv_s[pl.ds(CHUNK_ROWS, KV_ROWS), :]
        p1_s[pl.ds(nchunk * 8 + KV_ROWS, KV_ROWS), :] = jnp.zeros((KV_ROWS, HEAD_DIM), F32)
        ck, cv = _compress_mlp(p0_s, p1_s, nchunk, w1_ref, w2_ref, kg_ref)
        q8 = q_ref[...]
        row = lax.broadcasted_iota(jnp.int32, (N_HEADS, 1), 0)
        first = row < GROUP
        n_row = lax.broadcasted_iota(jnp.int32, (1, nchunk), 1)
        mask = jnp.broadcast_to(n_row * CMP_STRIDE + (CMP_BLOCK - 1) <= t_now, (N_HEADS, nchunk))
        ckb = ck.astype(BF16)
        cvb = cv.astype(BF16)
        s = jnp.where(first, _dot_nt(q8, ckb[:, 0:HEAD_DIM]), _dot_nt(q8, ckb[:, HEAD_DIM:2 * HEAD_DIM]))
        p, den = _masked_softmax_parts(s, mask)
        p = p / den
        pb = p.astype(BF16)
        oc_ref[...] = jnp.where(first, _dot(pb, cvb[:, 0:HEAD_DIM]), _dot(pb, cvb[:, HEAD_DIM:2 * HEAD_DIM]))
        nn = lax.broadcasted_iota(jnp.int32, (nchunk, 256), 0)
        jj = lax.broadcasted_iota(jnp.int32, (nchunk, 256), 1)
        ov = ((nn * CMP_STRIDE <= jj * SEL_BLOCK + SEL_BLOCK - 1)
              & (nn * CMP_STRIDE + CMP_BLOCK - 1 >= jj * SEL_BLOCK) & (jj < nsel)).astype(F32)
        imp8 = jnp.dot(p, ov, precision=lax.Precision.HIGHEST, preferred_element_type=F32)
        imp_ref[0:1, :] = jnp.sum(jnp.where(first, imp8, 0.0), axis=0, keepdims=True)
        imp_ref[1:2, :] = jnp.sum(jnp.where(first, 0.0, imp8), axis=0, keepdims=True)


def _nsa_sample_cmp(page_table, cache_cmp, kc_new, pwt, w1, w2, kg, q8):
    db, n_pages = page_table.shape
    past = n_pages * PAGE_SIZE
    n_steps = n_pages // PAGES_PER_STEP
    pages = cache_cmp.reshape(cache_cmp.shape[0], PAGE_SIZE * KV_ROWS, HEAD_DIM)

    def page_spec(k):
        return pl.BlockSpec((None, PAGE_SIZE * KV_ROWS, HEAD_DIM),
                            lambda b, g, pt: (pt[b * n_pages + g * PAGES_PER_STEP + k], 0, 0))

    def const_spec(shape):
        return pl.BlockSpec(shape, lambda b, g, pt: (0,) * len(shape))

    nchunk = past // CMP_STRIDE
    grid_spec = pltpu.PrefetchScalarGridSpec(
        num_scalar_prefetch=1,
        grid=(db, n_steps),
        in_specs=[page_spec(k) for k in range(PAGES_PER_STEP)] + [
            pl.BlockSpec((None, KV_ROWS, HEAD_DIM), lambda b, g, pt: (b, 0, 0)),
            const_spec((2 * CHUNK_ROWS, HEAD_DIM)), const_spec((2, HEAD_DIM, HEAD_DIM)),
            const_spec((2, HEAD_DIM, HEAD_DIM)), const_spec((3, HEAD_DIM)),
            pl.BlockSpec((None, N_HEADS, HEAD_DIM), lambda b, g, pt: (b, 0, 0))],
        out_specs=[pl.BlockSpec((None, N_HEADS, HEAD_DIM), lambda b, g, pt: (b, 0, 0)),
                   pl.BlockSpec((None, 2, 256), lambda b, g, pt: (b, 0, 0))],
        scratch_shapes=[pltpu.VMEM((2 * CHUNK_ROWS, HEAD_DIM), F32),
                        pltpu.VMEM(((nchunk + 1) * 8, HEAD_DIM), F32),
                        pltpu.VMEM(((nchunk + 1) * 8, HEAD_DIM), F32)])
    return pl.pallas_call(
        functools.partial(_nsa_sample_cmp_kernel, past=past, n_steps=n_steps),
        grid_spec=grid_spec,
        out_shape=[jax.ShapeDtypeStruct((db, N_HEADS, HEAD_DIM), F32),
                   jax.ShapeDtypeStruct((db, 2, 256), F32)],
        compiler_params=pltpu.CompilerParams(vmem_limit_bytes=VMEM_LIMIT),
        name="nsa_sample_cmp",
    )(page_table.reshape(-1), *([pages] * PAGES_PER_STEP), kc_new, pwt, w1, w2, kg, q8)


def _select_kernel(imp_ref, idx_ref, *, t_now, nsel):
    imp = imp_ref[...]
    rows, width = imp.shape
    j = lax.broadcasted_iota(jnp.int32, (rows, width), 1)
    jf = j.astype(F32)
    cur = t_now // SEL_BLOCK
    valid = (j <= cur) & (j < nsel)
    forced = (j == 0) | (valid & (j > cur - N_LOCAL_BLOCKS))
    score = jnp.where(valid, imp + FORCE * forced.astype(F32), NEG)
    col = lax.broadcasted_iota(jnp.int32, (rows, 128), 1)
    out = jnp.full((rows, 128), -1, jnp.int32)
    for it in range(min(TOP_N, nsel)):
        m = jnp.max(score, axis=-1, keepdims=True)
        idx = jnp.min(jnp.where(score == m, jf, 1e9), axis=-1, keepdims=True)
        out = jnp.where(col == it, jnp.where(m > 0.5 * NEG, idx.astype(jnp.int32), -1), out)
        score = jnp.where(jf == idx, -3e38, score)
    idx_ref[...] = out


def _select(imp2d, t_now, nsel):
    rows = imp2d.shape[0]
    return pl.pallas_call(
        functools.partial(_select_kernel, t_now=t_now, nsel=nsel),
        out_shape=jax.ShapeDtypeStruct((rows, 128), jnp.int32),
        name="nsa_sample_select",
    )(imp2d)


def _nsa_sample_sel_kernel(idx_ref, pt_ref, *refs, n_past_blk, n_top):
    blk_refs = refs[:N_KV_HEADS * n_top]
    q_ref, ksn_ref, os_ref = refs[N_KV_HEADS * n_top:]
    del pt_ref
    b = pl.program_id(0)
    q8 = q_ref[...]
    q8f = q8.astype(F32)
    ksn = ksn_ref[...]
    blk_rows = SEL_BLOCK * KV_ROWS
    nrows = n_top * blk_rows
    lane = lax.broadcasted_iota(jnp.int32, (1, nrows), 1)
    row = lax.broadcasted_iota(jnp.int32, (N_HEADS, 1), 0)
    first = row < GROUP
    per_head = []
    for h in range(N_KV_HEADS):
        rows_b = jnp.concatenate([blk_refs[h * n_top + n][...] for n in range(n_top)], axis=0).astype(BF16)
        k_new = ksn[h:h + 1, :]
        v_new = ksn[N_KV_HEADS + h:N_KV_HEADS + h + 1, :]
        slot_ok = jnp.zeros((1, nrows), F32)
        new_ok = jnp.zeros((1, 1), F32)
        for n in range(n_top):
            ix = idx_ref[(b * N_KV_HEADS + h) * n_top + n]
            past_ok = jnp.where((ix >= 0) & (ix < n_past_blk), 1.0, 0.0)
            slot_ok = jnp.where((lane >= n * blk_rows) & (lane < (n + 1) * blk_rows), past_ok, slot_ok)
            new_ok = jnp.maximum(new_ok, jnp.where(ix >= n_past_blk, 1.0, 0.0))
        mask = jnp.broadcast_to((slot_ok > 0.5) & ((lane & (KV_ROWS - 1)) == h), (N_HEADS, nrows))
        new_mask = jnp.broadcast_to(new_ok > 0.5, (N_HEADS, 1))
        s = jnp.where(mask, _dot_nt(q8, rows_b), NEG)
        s_new = jnp.where(new_mask, jnp.sum(q8f * k_new, axis=-1, keepdims=True), NEG)
        mx = jnp.maximum(jnp.max(s, axis=-1, keepdims=True), s_new)
        p = jnp.exp(s - mx) * mask.astype(F32)
        p_new = jnp.exp(s_new - mx) * new_mask.astype(F32)
        den = jnp.maximum(jnp.sum(p, axis=-1, keepdims=True) + p_new, 1e-30)
        p_on_v = pltpu.roll(p, N_KV_HEADS, axis=1).astype(BF16)
        per_head.append((_dot(p_on_v, rows_b) + p_new * v_new) / den)
    os_ref[...] = jnp.where(first, per_head[0], per_head[1])


def _nsa_sample_sel(sel_idx, page_table, cache_sel, q8, ks_new):
    db, n_pages = page_table.shape
    n_past_blk = n_pages * PAGE_SIZE // SEL_BLOCK
    n_top = sel_idx.shape[-1]
    halves = PAGE_SIZE // SEL_BLOCK
    blocks = cache_sel.reshape(cache_sel.shape[0] * halves, SEL_BLOCK * KV_ROWS, HEAD_DIM)
    blk = jnp.clip(sel_idx.reshape(db, N_KV_HEADS * n_top), 0, n_past_blk - 1)
    phys = jnp.take_along_axis(page_table, blk // halves, axis=1) * halves + blk % halves

    def blk_spec(h, n):
        return pl.BlockSpec((None, SEL_BLOCK * KV_ROWS, HEAD_DIM),
                            lambda b, idx, pb: (pb[(b * N_KV_HEADS + h) * n_top + n], 0, 0))

    grid_spec = pltpu.PrefetchScalarGridSpec(
        num_scalar_prefetch=2,
        grid=(db,),
        in_specs=[blk_spec(h, n) for h in range(N_KV_HEADS) for n in range(n_top)] + [
            pl.BlockSpec((None, N_HEADS, HEAD_DIM), lambda b, idx, pt: (b, 0, 0)),
            pl.BlockSpec((None, KV_ROWS, HEAD_DIM), lambda b, idx, pt: (b, 0, 0))],
        out_specs=pl.BlockSpec((None, N_HEADS, HEAD_DIM), lambda b, idx, pt: (b, 0, 0)))
    return pl.pallas_call(
        functools.partial(_nsa_sample_sel_kernel, n_past_blk=n_past_blk, n_top=n_top),
        grid_spec=grid_spec,
        out_shape=jax.ShapeDtypeStruct((db, N_HEADS, HEAD_DIM), F32),
        compiler_params=pltpu.CompilerParams(vmem_limit_bytes=VMEM_LIMIT),
        name="nsa_sample_sel",
    )(sel_idx.reshape(-1), phys.reshape(-1), *([blocks] * (N_KV_HEADS * n_top)), q8, ks_new)


def _nsa_sample_win_kernel(win_ref, kwn_ref, q_ref, wout_ref, ow_ref, *, past):
    nrows = win_ref.shape[0]
    wb = nrows // KV_ROWS
    kwn = kwn_ref[...]
    x = win_ref[...]
    wout_ref[...] = pltpu.roll(x, nrows - KV_ROWS, axis=0)
    wout_ref[pl.ds(nrows - KV_ROWS, KV_ROWS), :] = kwn
    q8 = q_ref[...]
    q8f = q8.astype(F32)
    t_now = past
    lane = lax.broadcasted_iota(jnp.int32, (N_HEADS, nrows), 1)
    row = lax.broadcasted_iota(jnp.int32, (N_HEADS, nrows), 0)
    pos = past - wb + _shr(lane, KV_ROWS.bit_length() - 1)
    d = t_now - pos
    mask = ((d >= 0) & (d < WINDOW) & (pos >= 0)
            & ((lane & (KV_ROWS - 1)) == _shr(row, GROUP.bit_length() - 1)))
    first = lax.broadcasted_iota(jnp.int32, (N_HEADS, 1), 0) < GROUP
    k_new = jnp.where(first, kwn[0:1, :], kwn[1:2, :])
    v_new = jnp.where(first, kwn[N_KV_HEADS:N_KV_HEADS + 1, :], kwn[N_KV_HEADS + 1:N_KV_HEADS + 2, :])
    xb = x.astype(BF16)
    s = jnp.where(mask, _dot_nt(q8, xb), NEG)
    s_new = jnp.sum(q8f * k_new, axis=-1, keepdims=True)
    mx = jnp.maximum(jnp.max(s, axis=-1, keepdims=True), s_new)
    p = jnp.exp(s - mx) * mask.astype(F32)
    p_new = jnp.exp(s_new - mx)
    den = jnp.maximum(jnp.sum(p, axis=-1, keepdims=True) + p_new, 1e-30)
    p_on_v = pltpu.roll(p, N_KV_HEADS, axis=1).astype(BF16)
    ow_ref[...] = (_dot(p_on_v, xb) + p_new * v_new) / den


def _nsa_sample_win(win_rows, kw_new, q8, past):
    db, nrows, _ = win_rows.shape
    return pl.pallas_call(
        functools.partial(_nsa_sample_win_kernel, past=past),
        grid=(db,),
        in_specs=[pl.BlockSpec((None, nrows, HEAD_DIM), lambda b: (b, 0, 0)),
                  pl.BlockSpec((None, KV_ROWS, HEAD_DIM), lambda b: (b, 0, 0)),
                  pl.BlockSpec((None, N_HEADS, HEAD_DIM), lambda b: (b, 0, 0))],
        out_specs=[pl.BlockSpec((None, nrows, HEAD_DIM), lambda b: (b, 0, 0)),
                   pl.BlockSpec((None, N_HEADS, HEAD_DIM), lambda b: (b, 0, 0))],
        out_shape=[jax.ShapeDtypeStruct((db, nrows, HEAD_DIM), F32),
                   jax.ShapeDtypeStruct((db, N_HEADS, HEAD_DIM), F32)],
        name="nsa_sample_win",
    )(win_rows, kw_new, q8)


def _gate_mix_kernel(oc_ref, os_ref, ow_ref, g_ref, o_ref):
    g = g_ref[...]
    c = lax.broadcasted_iota(jnp.int32, (128, D_MODEL), 0)
    head = _shr(lax.broadcasted_iota(jnp.int32, (128, D_MODEL), 1), HEAD_SHIFT)
    acc = jnp.zeros(o_ref.shape, F32)
    for r, ref in enumerate((oc_ref, os_ref, ow_ref)):
        expand = (c == r * N_HEADS + head).astype(F32)
        acc = acc + ref[...] * jnp.dot(g, expand, precision=lax.Precision.HIGHEST, preferred_element_type=F32)
    o_ref[...] = acc


def _merge_kernel(x_ref, a_ref, ra_ref, b_ref, on_ref, gate_ref, w_ref, y_ref):
    u = a_ref[...] * ra_ref[...] + b_ref[...] * on_ref[...]
    y_ref[...] = x_ref[...] + gate_ref[0] * _dot(u.astype(BF16), w_ref[...])


def _merge(x2d, a, ra, b, o_nsa, gate, w_out, tm, rows_per_mod):
    m = x2d.shape[0]
    tiles_per_mod = rows_per_mod // tm
    row_spec = pl.BlockSpec((tm, D_MODEL), lambda i: (i, 0))
    return pl.pallas_call(
        _merge_kernel,
        grid=(m // tm,),
        in_specs=[row_spec] * 5 + [
            pl.BlockSpec((1, gate.shape[1], D_MODEL), lambda i: (i // tiles_per_mod, 0, 0)),
            pl.BlockSpec((D_MODEL, D_MODEL), lambda i: (0, 0))],
        out_specs=row_spec,
        out_shape=jax.ShapeDtypeStruct((m, D_MODEL), F32),
        compiler_params=pltpu.CompilerParams(vmem_limit_bytes=VMEM_LIMIT),
        name="merge_out_proj",
    )(x2d, a, ra, b, o_nsa, gate, w_out)


def _rearranged_w_in(w_in):
    sizes = (D_MODEL, D_MODEL, D_MODEL, KV_W, KV_W, KV_W, 3 * N_HEADS, D_MODEL, D_MODEL, D_MODEL)
    offs = [0]
    for s in sizes:
        offs.append(offs[-1] + s)
    xr, zr, q, kc, ks, kw, bg, zn, ga, gb = [w_in[:, offs[i]:offs[i + 1]] for i in range(len(sizes))]
    pad = jnp.zeros((D_MODEL, C_END - C_BG - 3 * N_HEADS), w_in.dtype)
    return jnp.concatenate([xr, zr, q, kc, ks, kw, zn, ga, gb, bg, pad], axis=1).astype(BF16)


def kernel(x_prompt, x_sample, c_prompt, c_sample, state_conv, state_rglru, cache_cmp_kv, cache_sel_kv,
           state_win_kv, page_table, norm_g, w_ada, b_ada, w_in, conv_w, conv_b, rg_wa, rg_ba, rg_wx, rg_bx,
           rg_lambda, q_norm_g, k_norm_g, cmp_pool_w, cmp_w1, cmp_w2, w_out):
    depth = norm_g.shape[0]
    assert depth == 1 and x_sample.shape[1] == 1
    bsz, seq, _ = x_prompt.shape
    db = x_sample.shape[0]
    n_pages = page_table.shape[1]
    past = n_pages * PAGE_SIZE
    layer = 0

    w_cat = _rearranged_w_in(w_in[layer])
    w_out_b = w_out[layer].astype(BF16)
    wa_b = rg_wa[layer].astype(BF16)
    wx_b = rg_wx[layer].astype(BF16)
    w1_b = cmp_w1[layer].astype(BF16)
    w2_b = cmp_w2[layer].astype(BF16)
    row = lambda v: v.reshape(1, -1)
    rg_args = (conv_w[layer], row(conv_b[layer]), wa_b, row(rg_ba[layer]), wx_b, row(rg_bx[layer]),
               row(rg_lambda[layer]))
    praw = jnp.broadcast_to(
        cmp_pool_w[layer].reshape(2, 2, CMP_STRIDE).transpose(1, 2, 0)[:, :, :, None, None],
        (2, CMP_STRIDE, 2, N_KV_HEADS, HEAD_DIM)).reshape(2 * CHUNK_ROWS, HEAD_DIM)
    kg = k_norm_g[layer]

    mod = _modulation(jnp.concatenate([c_prompt, c_sample], axis=0), w_ada[layer], b_ada[layer])
    shift, scale, gate = mod[:, :D_MODEL], mod[:, D_MODEL:2 * D_MODEL], mod[:, 2 * D_MODEL:]

    xp2 = x_prompt.reshape(bsz * seq, D_MODEL)
    pm = lambda v: v[:bsz].reshape(bsz, 1, D_MODEL)
    (xr_p, a_p, b_p, q_p, kc_p, ks_p, kw_p, skb_p, svt_p, wkb_p, wvt_p, g_p) = _project(
        xp2, pm(shift), pm(scale), norm_g[layer], w_cat, q_norm_g[layer], kg, tm=256, rows_per_mod=seq)
    ra_p, h_p = _rglru_prompt(xr_p, bsz, seq, *rg_args)
    ck_p, cvt_p = _compress_prompt(kc_p, praw, w1_b, w2_b, kg, bsz, seq)
    on_p = _nsa_prompt(q_p, ck_p, cvt_p, skb_p, svt_p, wkb_p, wvt_p, g_p, bsz, seq)
    y_p = _merge(xp2, a_p, ra_p, b_p, on_p, pm(gate), w_out_b, tm=256, rows_per_mod=seq)

    xs2 = x_sample.reshape(db, D_MODEL)
    sm = lambda v: v[bsz:].reshape(1, db, D_MODEL)
    (xr_s, a_s, b_s, q_s, kc_s, ks_s, kw_s, _, _, _, _, g_s) = _project(
        xs2, sm(shift), sm(scale), norm_g[layer], w_cat, q_norm_g[layer], kg, tm=db, rows_per_mod=db)
    h_s = _rglru_step(xr_s, state_conv[layer].reshape(db, (CONV_W - 1) * D_MODEL), state_rglru[layer], *rg_args)
    q8 = q_s.reshape(db, N_HEADS, HEAD_DIM)
    new_rows = lambda v: v.reshape(db, KV_ROWS, HEAD_DIM)
    n_phys = cache_cmp_kv.shape[1]
    oc_s, imp = _nsa_sample_cmp(page_table, cache_cmp_kv.reshape(depth * n_phys, PAGE_SIZE * KV_ROWS, HEAD_DIM),
                                new_rows(kc_s), praw, w1_b, w2_b, kg, q8)
    nsel = past // SEL_BLOCK + 1
    sel_idx = _select(imp.reshape(db * N_KV_HEADS, 256), past, nsel)[:, :min(TOP_N, nsel)]
    os_s = _nsa_sample_sel(sel_idx, page_table,
                           cache_sel_kv.reshape(depth * n_phys, PAGE_SIZE * KV_ROWS, HEAD_DIM), q8, new_rows(ks_s))
    wb = state_win_kv.shape[2]
    win_s, ow_s = _nsa_sample_win(state_win_kv.reshape(depth * db, wb * KV_ROWS, HEAD_DIM), new_rows(kw_s), q8, past)
    on_s = pl.pallas_call(
        _gate_mix_kernel, out_shape=jax.ShapeDtypeStruct((db, D_MODEL), F32), name="nsa_sample_mix",
    )(oc_s.reshape(db, D_MODEL), os_s.reshape(db, D_MODEL), ow_s.reshape(db, D_MODEL), g_s)
    y_s = _merge(xs2, a_s, h_s, b_s, on_s, sm(gate), w_out_b, tm=db, rows_per_mod=db)

    kv_shape = (2, N_KV_HEADS, HEAD_DIM)
    xr_p3 = xr_p.reshape(bsz, seq, D_MODEL)
    conv_prompt = xr_p3[:, seq - (CONV_W - 1):][None]
    conv_sample = jnp.concatenate([state_conv[layer][:, 1:], xr_s[:, None, :]], axis=1)[None]
    win_len = min(WINDOW, seq)
    return (y_p.reshape(bsz, seq, D_MODEL), y_s.reshape(db, 1, D_MODEL),
            conv_prompt, conv_sample,
            h_p.reshape(1, bsz, D_MODEL), h_s.reshape(1, db, D_MODEL),
            kc_p.reshape(1, bsz, seq, *kv_shape), kc_s.reshape(1, db, 1, *kv_shape),
            ks_p.reshape(1, bsz, seq, *kv_shape), ks_s.reshape(1, db, 1, *kv_shape),
            kw_p.reshape(bsz, seq * KV_ROWS, HEAD_DIM)[:, (seq - win_len) * KV_ROWS:].reshape(
                1, bsz, win_len, *kv_shape),
            win_s.reshape(1, db, wb, *kv_shape))
```

```python
import functools

import jax
import jax.numpy as jnp
from jax import lax
from jax.experimental import pallas as pl
from jax.experimental.pallas import tpu as pltpu

F32 = jnp.float32
BF16 = jnp.bfloat16

D_MODEL = 1024
RG_BLOCKS = 8
RG_BW = D_MODEL // RG_BLOCKS
RG_C = 8.0
CONV_W = 4
N_HEADS = 8
N_KV_HEADS = 2
GROUP = N_HEADS // N_KV_HEADS
HEAD_DIM = D_MODEL // N_HEADS
KV_W = 2 * N_KV_HEADS * HEAD_DIM
KV_ROWS = 2 * N_KV_HEADS
CMP_BLOCK = 32
CMP_STRIDE = 16
SEL_BLOCK = 64
TOP_N = 16
N_LOCAL_BLOCKS = 2
WINDOW = 512
Q_BLOCK = 128
PAGE_SIZE = 128
EPS = 1e-6
NEG = -1e30
FORCE = 1e4

C_XR, C_ZR, C_Q, C_KC, C_KS, C_KW, C_ZN, C_GA, C_GB, C_BG, C_END = (
    0, 1024, 2048, 3072, 3584, 4096, 4608, 5632, 6656, 7680, 7808)
VMEM_LIMIT = 56 * 1024 * 1024
PAGES_PER_STEP = 64

SEL_SHIFT = SEL_BLOCK.bit_length() - 1
HEAD_SHIFT = HEAD_DIM.bit_length() - 1


def _shr(x, k):
    return lax.shift_right_arithmetic(x, jnp.int32(k))


def _sigmoid(x):
    return jax.nn.sigmoid(x)


def _dot(a, b):
    return jnp.dot(a, b, preferred_element_type=F32)


def _dot_nt(a, b):
    return lax.dot_general(a, b, (((1,), (1,)), ((), ())), preferred_element_type=F32)


def _rms_rows(x, g):
    return x * lax.rsqrt(jnp.mean(x * x, axis=-1, keepdims=True) + EPS) * g


def _mod_kernel(c_ref, w_ref, b_ref, o_ref):
    c = c_ref[...]
    s = c * _sigmoid(c)
    o_ref[...] = _dot(s.astype(BF16), w_ref[...].astype(BF16)) + b_ref[...]


def _modulation(c_all, w_ada, b_ada):
    n = c_all.shape[0]
    return pl.pallas_call(
        _mod_kernel,
        grid=(3,),
        in_specs=[pl.BlockSpec((n, D_MODEL), lambda j: (0, 0)),
                  pl.BlockSpec((D_MODEL, D_MODEL), lambda j: (0, j)),
                  pl.BlockSpec((1, D_MODEL), lambda j: (0, j))],
        out_specs=pl.BlockSpec((n, D_MODEL), lambda j: (0, j)),
        out_shape=jax.ShapeDtypeStruct((n, 3 * D_MODEL), F32),
        name="adaln_mod",
    )(c_all, w_ada, b_ada.reshape(1, 3 * D_MODEL))


def _proj_kernel(x_ref, shift_ref, scale_ref, ng_ref, w_ref, qg_ref, kg_ref,
                 xr_ref, a_ref, b_ref, q_ref, kc_ref, ks_ref, kw_ref, skb_ref, svt_ref, wkb_ref, wvt_ref, g_ref):
    x = x_ref[...]
    h = _rms_rows(x, ng_ref[...])
    h = h * (1.0 + scale_ref[0]) + shift_ref[0]
    hb = h.astype(BF16)

    def mm(lo, hi):
        return _dot(hb, w_ref[:, lo:hi])

    xr_ref[...] = mm(C_XR, C_ZR)
    zr = mm(C_ZR, C_Q)
    ga = mm(C_GA, C_GB)
    a_ref[...] = _sigmoid(ga) * (zr * _sigmoid(zr))
    zn = mm(C_ZN, C_GA)
    gb = mm(C_GB, C_BG)
    b_ref[...] = _sigmoid(gb) * (zn * _sigmoid(zn))
    q = mm(C_Q, C_KC)
    for hd in range(N_HEADS):
        sl = slice(hd * HEAD_DIM, (hd + 1) * HEAD_DIM)
        q_ref[:, sl] = (_rms_rows(q[:, sl], qg_ref[...]) * (HEAD_DIM ** -0.5)).astype(BF16)
    tm = x.shape[0]

    def store_kv(o_ref, eh, val):
        o_ref[pl.ds(eh, tm, stride=KV_ROWS), :] = val

    kc = mm(C_KC, C_KS)
    for eh in range(KV_ROWS):
        store_kv(kc_ref, eh, kc[:, eh * HEAD_DIM:(eh + 1) * HEAD_DIM])
    for lo, hi, o_ref, kb_ref, vt_ref, gi in ((C_KS, C_KW, ks_ref, skb_ref, svt_ref, 1),
                                              (C_KW, C_ZN, kw_ref, wkb_ref, wvt_ref, 2)):
        kv = mm(lo, hi)
        for hd in range(N_KV_HEADS):
            sl = slice(hd * HEAD_DIM, (hd + 1) * HEAD_DIM)
            kn = _rms_rows(kv[:, sl], kg_ref[gi:gi + 1, :])
            store_kv(o_ref, hd, kn)
            kb_ref[:, sl] = kn.astype(BF16)
            v = kv[:, (N_KV_HEADS + hd) * HEAD_DIM:(N_KV_HEADS + hd + 1) * HEAD_DIM]
            store_kv(o_ref, N_KV_HEADS + hd, v)
            v_t = v.T.astype(BF16)
            for j in range(tm // Q_BLOCK):
                vt_ref[j, sl, :] = v_t[:, j * Q_BLOCK:(j + 1) * Q_BLOCK]
    g_ref[...] = _sigmoid(mm(C_BG, C_END))


def _project(x2d, shift, scale, norm_g, w_cat, q_norm_g, k_norm_g, tm, rows_per_mod):
    m = x2d.shape[0]
    tiles_per_mod = rows_per_mod // tm
    mod_rows = shift.shape[1]
    mod_spec = pl.BlockSpec((1, mod_rows, D_MODEL), lambda i: (i // tiles_per_mod, 0, 0))

    def row_spec(width, mult=1):
        return pl.BlockSpec((tm * mult, width), lambda i: (i, 0))

    def const_spec(shape):
        return pl.BlockSpec(shape, lambda i: (0,) * len(shape))

    kv_heads_w = N_KV_HEADS * HEAD_DIM
    k_bf16 = (pl.BlockSpec((tm, kv_heads_w), lambda i: (i, 0)), jax.ShapeDtypeStruct((m, kv_heads_w), BF16))
    vt_bf16 = (pl.BlockSpec((tm // Q_BLOCK, kv_heads_w, Q_BLOCK), lambda i: (i, 0, 0)),
               jax.ShapeDtypeStruct((m // Q_BLOCK, kv_heads_w, Q_BLOCK), BF16))

    def rows(width, dtype, mult=1):
        return row_spec(width, mult), jax.ShapeDtypeStruct((m * mult, width), dtype)

    outs = (rows(D_MODEL, F32), rows(D_MODEL, F32), rows(D_MODEL, F32), rows(D_MODEL, BF16),
            rows(HEAD_DIM, F32, KV_ROWS), rows(HEAD_DIM, F32, KV_ROWS), rows(HEAD_DIM, F32, KV_ROWS),
            k_bf16, vt_bf16, k_bf16, vt_bf16, rows(128, F32))
    return pl.pallas_call(
        _proj_kernel,
        grid=(m // tm,),
        in_specs=[row_spec(D_MODEL), mod_spec, mod_spec, const_spec((1, D_MODEL)),
                  pl.BlockSpec((D_MODEL, C_END), lambda i: (0, 0), pipeline_mode=pl.Buffered(1)),
                  const_spec((1, HEAD_DIM)), const_spec((3, HEAD_DIM))],
        out_specs=[spec for spec, _ in outs],
        out_shape=[shape for _, shape in outs],
        compiler_params=pltpu.CompilerParams(vmem_limit_bytes=VMEM_LIMIT),
        name="in_proj",
    )(x2d, shift, scale, norm_g.reshape(1, D_MODEL), w_cat, q_norm_g.reshape(1, HEAD_DIM), k_norm_g)


def _softplus(z):
    return jnp.maximum(z, 0.0) + jnp.log1p(jnp.exp(-jnp.abs(z)))


def _rglru_coeffs(xc, wa_ref, ba_ref, wx_ref, bx_ref, lam_ref, a_out, b_out):
    xcb = xc.astype(BF16)
    sp = _softplus(-lam_ref[...])
    for k in range(RG_BLOCKS):
        sl = slice(k * RG_BW, (k + 1) * RG_BW)
        r = _sigmoid(_dot(xcb[:, sl], wa_ref[k]) + ba_ref[:, sl])
        i = _sigmoid(_dot(xcb[:, sl], wx_ref[k]) + bx_ref[:, sl])
        log_a = -RG_C * r * sp[:, sl]
        a = jnp.exp(log_a)
        a_out[:, sl] = a
        b_out[:, sl] = jnp.sqrt(-jnp.tanh(log_a) * (a * a + 1.0)) * i * xc[:, sl]


def _rglru_prompt_kernel(xr_ref, cw_ref, cb_ref, wa_ref, ba_ref, wx_ref, bx_ref, lam_ref,
                         ra_ref, hl_ref, ext_s, a_s, b_s, h_s):
    t_len = xr_ref.shape[0]

    @pl.when(pl.program_id(1) == 0)
    def _():
        ext_s[0:8, :] = jnp.zeros((8, D_MODEL), F32)
        h_s[...] = jnp.zeros((1, D_MODEL), F32)

    x = xr_ref[...]
    ext_s[8:8 + t_len, :] = x
    xc = ext_s[pl.ds(5, t_len), :] * cw_ref[0:1, :] + cb_ref[...]
    xc = xc + ext_s[pl.ds(6, t_len), :] * cw_ref[1:2, :]
    xc = xc + ext_s[pl.ds(7, t_len), :] * cw_ref[2:3, :]
    xc = xc + x * cw_ref[3:4, :]
    ext_s[0:8, :] = x[t_len - 8:t_len, :]
    _rglru_coeffs(xc, wa_ref, ba_ref, wx_ref, bx_ref, lam_ref, a_s, b_s)

    row = lax.broadcasted_iota(jnp.int32, (8, D_MODEL), 0)

    def tile(i, h):
        r0 = pl.multiple_of(i * 8, 8)
        a = a_s[pl.ds(r0, 8), :]
        b = b_s[pl.ds(r0, 8), :]
        for s in (1, 2, 4):
            keep = row >= s
            b = jnp.where(keep, a * pltpu.roll(b, s, axis=0) + b, b)
            a = jnp.where(keep, a * pltpu.roll(a, s, axis=0), a)
        hh = a * h + b
        ra_ref[pl.ds(r0, 8), :] = hh
        return hh[7:8, :]

    h = lax.fori_loop(0, t_len // 8, tile, h_s[...], unroll=2)
    h_s[...] = h
    hl_ref[...] = h


def _rglru_prompt(xr, bsz, seq, cw, cb, wa, ba, wx, bx, lam, t_chunk=512):
    nchunk = seq // t_chunk

    def const_spec(shape):
        return pl.BlockSpec(shape, lambda b, c: (0,) * len(shape))

    return pl.pallas_call(
        _rglru_prompt_kernel,
        grid=(bsz, nchunk),
        in_specs=[pl.BlockSpec((t_chunk, D_MODEL), lambda b, c: (b * nchunk + c, 0)),
                  const_spec((CONV_W, D_MODEL)), const_spec((1, D_MODEL)),
                  const_spec((RG_BLOCKS, RG_BW, RG_BW)), const_spec((1, D_MODEL)),
                  const_spec((RG_BLOCKS, RG_BW, RG_BW)), const_spec((1, D_MODEL)),
                  const_spec((1, D_MODEL))],
        out_specs=[pl.BlockSpec((t_chunk, D_MODEL), lambda b, c: (b * nchunk + c, 0)),
                   pl.BlockSpec((None, 1, D_MODEL), lambda b, c: (b, 0, 0))],
        out_shape=[jax.ShapeDtypeStruct((bsz * seq, D_MODEL), F32),
                   jax.ShapeDtypeStruct((bsz, 1, D_MODEL), F32)],
        scratch_shapes=[pltpu.VMEM((t_chunk + 8, D_MODEL), F32), pltpu.VMEM((t_chunk, D_MODEL), F32),
                        pltpu.VMEM((t_chunk, D_MODEL), F32), pltpu.VMEM((1, D_MODEL), F32)],
        compiler_params=pltpu.CompilerParams(vmem_limit_bytes=VMEM_LIMIT),
        name="rglru_prompt",
    )(xr, cw, cb, wa, ba, wx, bx, lam)


def _rglru_step_kernel(xr_ref, cbuf_ref, h0_ref, cw_ref, cb_ref, wa_ref, ba_ref, wx_ref, bx_ref, lam_ref,
                       h_ref, a_s, b_s):
    xc = cbuf_ref[:, 0:D_MODEL] * cw_ref[0:1, :] + cb_ref[...]
    xc = xc + cbuf_ref[:, D_MODEL:2 * D_MODEL] * cw_ref[1:2, :]
    xc = xc + cbuf_ref[:, 2 * D_MODEL:3 * D_MODEL] * cw_ref[2:3, :]
    xc = xc + xr_ref[...] * cw_ref[3:4, :]
    _rglru_coeffs(xc, wa_ref, ba_ref, wx_ref, bx_ref, lam_ref, a_s, b_s)
    h_ref[...] = a_s[...] * h0_ref[...] + b_s[...]


def _rglru_step(xr, cbuf, h0, cw, cb, wa, ba, wx, bx, lam):
    n = xr.shape[0]
    return pl.pallas_call(
        _rglru_step_kernel,
        out_shape=jax.ShapeDtypeStruct((n, D_MODEL), F32),
        scratch_shapes=[pltpu.VMEM((n, D_MODEL), F32), pltpu.VMEM((n, D_MODEL), F32)],
        name="rglru_step",
    )(xr, cbuf, h0, cw, cb, wa, ba, wx, bx, lam)


CHUNK_ROWS = CMP_STRIDE * KV_ROWS


def _pool_weight_rows(praw_ref):
    praw = praw_ref[...]
    row = lax.broadcasted_iota(jnp.int32, praw.shape, 0)
    is_k = (row & (KV_ROWS - 1)) < N_KV_HEADS
    m_k = jnp.max(jnp.where(is_k, praw, -3e38), axis=0, keepdims=True)
    m_v = jnp.max(jnp.where(is_k, -3e38, praw), axis=0, keepdims=True)
    ex = jnp.exp(praw - jnp.where(is_k, m_k, m_v))
    s_k = jnp.sum(jnp.where(is_k, ex, 0.0), axis=0, keepdims=True) * (1.0 / N_KV_HEADS)
    s_v = jnp.sum(jnp.where(is_k, 0.0, ex), axis=0, keepdims=True) * (1.0 / N_KV_HEADS)
    return ex / jnp.where(is_k, s_k, s_v)


def _pool_rows(x, wv):
    n = x.shape[0] // CHUNK_ROWS
    x4 = x.reshape(n, CHUNK_ROWS // 8, 8, HEAD_DIM)
    w4 = wv.reshape(2, CHUNK_ROWS // 8, 8, HEAD_DIM)
    p0 = x4[:, 0] * w4[0, 0]
    p1 = x4[:, 0] * w4[1, 0]
    for v in range(1, CHUNK_ROWS // 8):
        p0 = p0 + x4[:, v] * w4[0, v]
        p1 = p1 + x4[:, v] * w4[1, v]
    return p0.reshape(n * 8, HEAD_DIM), p1.reshape(n * 8, HEAD_DIM)


def _pooled_head(p0_s, p1_s, eh, nc):
    def col(ref, start):
        return ref[pl.ds(start, nc, stride=8), :]
    return (col(p0_s, eh) + col(p0_s, eh + KV_ROWS)) + (col(p1_s, 8 + eh) + col(p1_s, 8 + eh + KV_ROWS))


def _compress_mlp(p0_s, p1_s, nc, w1_ref, w2_ref, kg_ref):
    outs = []
    for e in range(2):
        per_head = []
        for hd in range(N_KV_HEADS):
            p = _pooled_head(p0_s, p1_s, e * N_KV_HEADS + hd, nc)
            hid = _dot(p.astype(BF16), w1_ref[e])
            hid = hid * _sigmoid(hid)
            comp = p + _dot(hid.astype(BF16), w2_ref[e])
            if e == 0:
                comp = _rms_rows(comp, kg_ref[0:1, :])
            per_head.append(comp)
        outs.append(jnp.concatenate(per_head, axis=1))
    return outs[0], outs[1]


def _masked_softmax_parts(s, mask):
    sm = jnp.where(mask, s, NEG)
    p = jnp.exp(sm - jnp.max(sm, axis=-1, keepdims=True)) * mask.astype(F32)
    return p, jnp.maximum(jnp.sum(p, axis=-1, keepdims=True), 1e-30)


POOL_SLAB = 8


def _compress_prompt_kernel(x_ref, praw_ref, w1_ref, w2_ref, kg_ref, ck_ref, cvt_ref, p0_s, p1_s):
    nch = x_ref.shape[0] // CHUNK_ROWS
    wv = _pool_weight_rows(praw_ref)

    def slab(i, _):
        x = x_ref[pl.ds(pl.multiple_of(i * (POOL_SLAB * CHUNK_ROWS), POOL_SLAB * CHUNK_ROWS),
                        POOL_SLAB * CHUNK_ROWS), :]
        p0, p1 = _pool_rows(x, wv)
        rows = pl.ds(pl.multiple_of(i * (POOL_SLAB * 8), POOL_SLAB * 8), POOL_SLAB * 8)
        p0_s[rows, :] = p0
        p1_s[rows, :] = p1
        return 0

    lax.fori_loop(0, nch // POOL_SLAB, slab, 0)
    p1_s[pl.ds(nch * 8, 8), :] = jnp.zeros((8, HEAD_DIM), F32)
    ck, cv = _compress_mlp(p0_s, p1_s, nch, w1_ref, w2_ref, kg_ref)
    ck_ref[...] = ck
    cvt_ref[...] = cv.T


def _compress_prompt(kc_rows, praw, w1, w2, kg, bsz, seq):
    nch = seq // CMP_STRIDE

    def const_spec(shape):
        return pl.BlockSpec(shape, lambda b: (0,) * len(shape))

    return pl.pallas_call(
        _compress_prompt_kernel,
        grid=(bsz,),
        in_specs=[pl.BlockSpec((seq * KV_ROWS, HEAD_DIM), lambda b: (b, 0)),
                  const_spec((2 * CHUNK_ROWS, HEAD_DIM)), const_spec((2, HEAD_DIM, HEAD_DIM)),
                  const_spec((2, HEAD_DIM, HEAD_DIM)), const_spec((3, HEAD_DIM))],
        out_specs=[pl.BlockSpec((None, nch, 2 * HEAD_DIM), lambda b: (b, 0, 0)),
                   pl.BlockSpec((None, 2 * HEAD_DIM, nch), lambda b: (b, 0, 0))],
        out_shape=[jax.ShapeDtypeStruct((bsz, nch, 2 * HEAD_DIM), F32),
                   jax.ShapeDtypeStruct((bsz, 2 * HEAD_DIM, nch), F32)],
        scratch_shapes=[pltpu.VMEM(((nch + 1) * 8, HEAD_DIM), F32)] * 2,
        name="compress_prompt",
    )(kc_rows, praw, w1, w2, kg)


GQ = GROUP * Q_BLOCK
WIN_BLOCKS = WINDOW // Q_BLOCK + 1


def _bias_groups(s_t, bias):
    return jnp.concatenate([s_t[:, g * Q_BLOCK:(g + 1) * Q_BLOCK] + bias for g in range(GROUP)], axis=1)


def _fold8(x, op):
    parts = [x[i * 8:(i + 1) * 8, :] for i in range(x.shape[0] // 8)]
    while len(parts) > 1:
        parts = [op(parts[i], parts[i + 1]) for i in range(0, len(parts), 2)]
    return parts[0]


def _finish_t(o_t, m, l8):
    den = jnp.maximum(jnp.sum(l8, axis=0, keepdims=True), 1e-30)
    return jnp.where(m > 0.5 * NEG, o_t / den, 0.0)


def _nsa_prompt_kernel(q_ref, ck_ref, cvt_ref, sk_ref, svt_ref, wk_ref, wvt_ref, g_ref, o_ref,
                       bias_s, ps_s, pw_s):
    qi = pl.program_id(1)
    nq = Q_BLOCK
    ncp = ck_ref.shape[0]
    nkb = svt_ref.shape[0]
    nsel = nkb * (Q_BLOCK // SEL_BLOCK)
    n_top = min(TOP_N, nsel)
    start = qi * nq
    t_row = start + lax.broadcasted_iota(jnp.int32, (1, nq), 1)
    gates_t = g_ref[...].T

    n_sub = lax.broadcasted_iota(jnp.int32, (ncp, nq), 0)
    cmp_mask = (n_sub * CMP_STRIDE + (CMP_BLOCK - 1) <= t_row) & (n_sub < ncp - 1)
    jj = lax.broadcasted_iota(jnp.int32, (128, ncp), 0)
    nn = lax.broadcasted_iota(jnp.int32, (128, ncp), 1)
    ov_t = ((nn * CMP_STRIDE <= jj * SEL_BLOCK + SEL_BLOCK - 1)
            & (nn * CMP_STRIDE + CMP_BLOCK - 1 >= jj * SEL_BLOCK)
            & (nn < ncp - 1) & (jj < nsel)).astype(F32)
    j_col = lax.broadcasted_iota(jnp.int32, (128, nq), 0)
    cur = _shr(t_row, SEL_SHIFT)
    valid_t = (j_col <= cur) & (j_col < nsel)
    forced_t = (j_col == 0) | (valid_t & (j_col > cur - N_LOCAL_BLOCKS))
    key_sub = lax.broadcasted_iota(jnp.int32, (Q_BLOCK, nq), 0)
    q_lane = lax.broadcasted_iota(jnp.int32, (Q_BLOCK, nq), 1)
    win_bias = {0: jnp.where(q_lane < key_sub, 0.0, NEG), WIN_BLOCKS - 1: jnp.where(q_lane >= key_sub, 0.0, NEG)}

    for h in range(N_KV_HEADS):
        qh = jnp.concatenate(
            [q_ref[:, (h * GROUP + g) * HEAD_DIM:(h * GROUP + g + 1) * HEAD_DIM] for g in range(GROUP)], axis=0)
        hs = slice(h * HEAD_DIM, (h + 1) * HEAD_DIM)

        s_c = _dot_nt(ck_ref[:, hs].astype(BF16), qh)
        p_groups = []
        for g in range(GROUP):
            sm = jnp.where(cmp_mask, s_c[:, g * nq:(g + 1) * nq], NEG)
            p = jnp.exp(sm - jnp.max(sm, axis=0, keepdims=True)) * cmp_mask.astype(F32)
            p_groups.append(p / jnp.maximum(jnp.sum(p, axis=0, keepdims=True), 1e-30))
        o_c = _dot(cvt_ref[hs, :].astype(BF16), jnp.concatenate(p_groups, axis=1).astype(BF16))
        psum = p_groups[0] + p_groups[1] + p_groups[2] + p_groups[3]
        imp_t = jnp.dot(ov_t, psum, precision=lax.Precision.HIGHEST, preferred_element_type=F32)
        score_t = jnp.where(valid_t, imp_t + FORCE * forced_t.astype(F32), NEG)
        rank = jnp.zeros((128, nq), jnp.int32)
        for k in range(nsel):
            rk = score_t[k:k + 1, :]
            beats = (rk > score_t) | ((rk == score_t) & (j_col > k))
            rank = rank + beats.astype(jnp.int32)
        sel_t = (rank < n_top) & (score_t > 0.5 * NEG)
        sel_f = sel_t.astype(F32)

        def sel_branch(nblk):
            def run():
                for kb in range(nblk):
                    lo = sel_f[2 * kb:2 * kb + 1, :]
                    hi = sel_f[2 * kb + 1:2 * kb + 2, :]
                    picked = jnp.where(key_sub < SEL_BLOCK, lo, hi) > 0.5
                    bias_s[kb] = jnp.where(picked & (kb * Q_BLOCK + key_sub <= t_row), 0.0, NEG)

                def scores(kb):
                    return _bias_groups(_dot_nt(sk_ref[kb * Q_BLOCK:(kb + 1) * Q_BLOCK, hs], qh), bias_s[kb])

                m8 = jnp.full((8, GQ), NEG, F32)
                for kb in range(nblk):
                    m8 = jnp.maximum(m8, _fold8(scores(kb), jnp.maximum))
                m_s = jnp.max(m8, axis=0, keepdims=True)
                l_s = jnp.zeros((8, GQ), F32)
                for kb in range(nblk):
                    p = jnp.exp(scores(kb) - m_s)
                    ps_s[kb * Q_BLOCK:(kb + 1) * Q_BLOCK, :] = p.astype(BF16)
                    l_s = l_s + _fold8(p, jnp.add)
                v_sel = jnp.concatenate([svt_ref[kb, hs, :] for kb in range(nblk)], axis=1)
                return _finish_t(_dot(v_sel, ps_s[0:nblk * Q_BLOCK, :]), m_s, l_s)
            return run

        quarter = nkb // 4
        o_s = lax.switch(qi // quarter, [sel_branch((v + 1) * quarter) for v in range(4)])

        def win_scores(r):
            kb = qi - (WIN_BLOCKS - 1) + r
            kbc = jnp.maximum(kb, 0)
            off = pl.multiple_of(kbc * Q_BLOCK, Q_BLOCK)
            s_t = _dot_nt(wk_ref[pl.ds(off, Q_BLOCK), hs], qh) + jnp.where(kb < 0, NEG, 0.0)
            if r in win_bias:
                s_t = _bias_groups(s_t, win_bias[r])
            return s_t, kbc

        m8 = jnp.full((8, GQ), NEG, F32)
        for r in range(WIN_BLOCKS):
            m8 = jnp.maximum(m8, _fold8(win_scores(r)[0], jnp.maximum))
        m_w = jnp.max(m8, axis=0, keepdims=True)
        l_w = jnp.zeros((8, GQ), F32)
        v_blocks = []
        for r in range(WIN_BLOCKS):
            s_t, kbc = win_scores(r)
            p = jnp.exp(s_t - m_w)
            pw_s[r * Q_BLOCK:(r + 1) * Q_BLOCK, :] = p.astype(BF16)
            l_w = l_w + _fold8(p, jnp.add)
            v_blocks.append(wvt_ref[kbc, hs, :])
        o_w = _finish_t(_dot(jnp.concatenate(v_blocks, axis=1), pw_s[...]), m_w, l_w)

        for g in range(GROUP):
            head = h * GROUP + g
            cs = slice(g * nq, (g + 1) * nq)
            o = (o_c[:, cs] * gates_t[head:head + 1, :]
                 + o_s[:, cs] * gates_t[N_HEADS + head:N_HEADS + head + 1, :]
                 + o_w[:, cs] * gates_t[2 * N_HEADS + head:2 * N_HEADS + head + 1, :])
            o_ref[:, head * HEAD_DIM:(head + 1) * HEAD_DIM] = o.T


def _nsa_prompt(q, ck, cvt, sk, svt, wk, wvt, gates, bsz, seq):
    nqb = seq // Q_BLOCK
    ncp = ck.shape[1]
    kv_heads_w = N_KV_HEADS * HEAD_DIM
    k_spec = pl.BlockSpec((seq, kv_heads_w), lambda b, i: (b, 0))
    vt_spec = pl.BlockSpec((nqb, kv_heads_w, Q_BLOCK), lambda b, i: (b, 0, 0))
    return pl.pallas_call(
        _nsa_prompt_kernel,
        grid=(bsz, nqb),
        in_specs=[pl.BlockSpec((Q_BLOCK, D_MODEL), lambda b, i: (b * nqb + i, 0)),
                  pl.BlockSpec((None, ncp, kv_heads_w), lambda b, i: (b, 0, 0)),
                  pl.BlockSpec((None, kv_heads_w, ncp), lambda b, i: (b, 0, 0)),
                  k_spec, vt_spec, k_spec, vt_spec,
                  pl.BlockSpec((Q_BLOCK, 128), lambda b, i: (b * nqb + i, 0))],
        out_specs=pl.BlockSpec((Q_BLOCK, D_MODEL), lambda b, i: (b * nqb + i, 0)),
        out_shape=jax.ShapeDtypeStruct((bsz * seq, D_MODEL), F32),
        scratch_shapes=[pltpu.VMEM((nqb, Q_BLOCK, Q_BLOCK), F32),
                        pltpu.VMEM((seq, GQ), BF16),
                        pltpu.VMEM((WIN_BLOCKS * Q_BLOCK, GQ), BF16)],
        compiler_params=pltpu.CompilerParams(vmem_limit_bytes=VMEM_LIMIT),
        name="nsa_prompt",
    )(q, ck, cvt, sk, svt, wk, wvt, gates)


def _nsa_sample_cmp_kernel(pt_ref, *refs, past, n_steps, pps):
    page_refs = refs[:pps]
    (kcn_ref, praw_ref, w1_ref, w2_ref, kg_ref, q_ref, oc_ref, imp_ref, wv_s, p0_s, p1_s) = refs[pps:]
    del pt_ref
    g = pl.program_id(1)
    chunks_per_page = PAGE_SIZE // CMP_STRIDE
    nchunk = past // CMP_STRIDE
    nsel = past // SEL_BLOCK + 1
    t_now = past

    @pl.when(g == 0)
    def _():
        wv_s[...] = _pool_weight_rows(praw_ref)

    wv = wv_s[...]
    for k in range(pps):
        p0, p1 = _pool_rows(page_refs[k][...], wv)
        rows = pl.ds(pl.multiple_of((g * pps + k) * (chunks_per_page * 8), chunks_per_page * 8),
                     chunks_per_page * 8)
        p0_s[rows, :] = p0
        p1_s[rows, :] = p1

    @pl.when(g == n_steps - 1)
    def _():
        p1_s[pl.ds(nchunk * 8, KV_ROWS), :] = kcn_ref[...] * wv_s[pl.ds(CHUNK_ROWS, KV_ROWS), :]
        p1_s[pl.ds(nchunk * 8 + KV_ROWS, KV_ROWS), :] = jnp.zeros((KV_ROWS, HEAD_DIM), F32)
        ck, cv = _compress_mlp(p0_s, p1_s, nchunk, w1_ref, w2_ref, kg_ref)
        q8 = q_ref[...]
        row = lax.broadcasted_iota(jnp.int32, (N_HEADS, 1), 0)
        first = row < GROUP
        n_row = lax.broadcasted_iota(jnp.int32, (1, nchunk), 1)
        mask = jnp.broadcast_to(n_row * CMP_STRIDE + (CMP_BLOCK - 1) <= t_now, (N_HEADS, nchunk))
        ckb = ck.astype(BF16)
        cvb = cv.astype(BF16)
        s = jnp.where(first, _dot_nt(q8, ckb[:, 0:HEAD_DIM]), _dot_nt(q8, ckb[:, HEAD_DIM:2 * HEAD_DIM]))
        p, den = _masked_softmax_parts(s, mask)
        p = p / den
        pb = p.astype(BF16)
        oc_ref[...] = jnp.where(first, _dot(pb, cvb[:, 0:HEAD_DIM]), _dot(pb, cvb[:, HEAD_DIM:2 * HEAD_DIM]))
        nn = lax.broadcasted_iota(jnp.int32, (nchunk, 256), 0)
        jj = lax.broadcasted_iota(jnp.int32, (nchunk, 256), 1)
        ov = ((nn * CMP_STRIDE <= jj * SEL_BLOCK + SEL_BLOCK - 1)
              & (nn * CMP_STRIDE + CMP_BLOCK - 1 >= jj * SEL_BLOCK) & (jj < nsel)).astype(F32)
        imp8 = jnp.dot(p, ov, precision=lax.Precision.HIGHEST, preferred_element_type=F32)
        imp_ref[0:1, :] = jnp.sum(jnp.where(first, imp8, 0.0), axis=0, keepdims=True)
        imp_ref[1:2, :] = jnp.sum(jnp.where(first, 0.0, imp8), axis=0, keepdims=True)


def _nsa_sample_cmp(page_table, cache_cmp, kc_new, pwt, w1, w2, kg, q8):
    db, n_pages = page_table.shape
    past = n_pages * PAGE_SIZE
    pps = min(PAGES_PER_STEP, n_pages)
    n_steps = n_pages // pps
    pages = cache_cmp.reshape(cache_cmp.shape[0], PAGE_SIZE * KV_ROWS, HEAD_DIM)

    def page_spec(k):
        return pl.BlockSpec((None, PAGE_SIZE * KV_ROWS, HEAD_DIM),
                            lambda b, g, pt: (pt[b * n_pages + g * pps + k], 0, 0))

    def const_spec(shape):
        return pl.BlockSpec(shape, lambda b, g, pt: (0,) * len(shape))

    nchunk = past // CMP_STRIDE
    grid_spec = pltpu.PrefetchScalarGridSpec(
        num_scalar_prefetch=1,
        grid=(db, n_steps),
        in_specs=[page_spec(k) for k in range(pps)] + [
            pl.BlockSpec((None, KV_ROWS, HEAD_DIM), lambda b, g, pt: (b, 0, 0)),
            const_spec((2 * CHUNK_ROWS, HEAD_DIM)), const_spec((2, HEAD_DIM, HEAD_DIM)),
            const_spec((2, HEAD_DIM, HEAD_DIM)), const_spec((3, HEAD_DIM)),
            pl.BlockSpec((None, N_HEADS, HEAD_DIM), lambda b, g, pt: (b, 0, 0))],
        out_specs=[pl.BlockSpec((None, N_HEADS, HEAD_DIM), lambda b, g, pt: (b, 0, 0)),
                   pl.BlockSpec((None, 2, 256), lambda b, g, pt: (b, 0, 0))],
        scratch_shapes=[pltpu.VMEM((2 * CHUNK_ROWS, HEAD_DIM), F32),
                        pltpu.VMEM(((nchunk + 1) * 8, HEAD_DIM), F32),
                        pltpu.VMEM(((nchunk + 1) * 8, HEAD_DIM), F32)])
    return pl.pallas_call(
        functools.partial(_nsa_sample_cmp_kernel, past=past, n_steps=n_steps, pps=pps),
        grid_spec=grid_spec,
        out_shape=[jax.ShapeDtypeStruct((db, N_HEADS, HEAD_DIM), F32),
                   jax.ShapeDtypeStruct((db, 2, 256), F32)],
        compiler_params=pltpu.CompilerParams(vmem_limit_bytes=VMEM_LIMIT),
        name="nsa_sample_cmp",
    )(page_table.reshape(-1), *([pages] * pps), kc_new, pwt, w1, w2, kg, q8)


def _select_kernel(imp_ref, idx_ref, *, t_now, nsel):
    imp = imp_ref[...]
    rows, width = imp.shape
    j = lax.broadcasted_iota(jnp.int32, (rows, width), 1)
    jf = j.astype(F32)
    cur = t_now // SEL_BLOCK
    valid = (j <= cur) & (j < nsel)
    forced = (j == 0) | (valid & (j > cur - N_LOCAL_BLOCKS))
    score = jnp.where(valid, imp + FORCE * forced.astype(F32), NEG)
    col = lax.broadcasted_iota(jnp.int32, (rows, 128), 1)
    out = jnp.full((rows, 128), -1, jnp.int32)
    for it in range(min(TOP_N, nsel)):
        m = jnp.max(score, axis=-1, keepdims=True)
        idx = jnp.min(jnp.where(score == m, jf, 1e9), axis=-1, keepdims=True)
        out = jnp.where(col == it, jnp.where(m > 0.5 * NEG, idx.astype(jnp.int32), -1), out)
        score = jnp.where(jf == idx, -3e38, score)
    idx_ref[...] = out


def _select(imp2d, t_now, nsel):
    rows = imp2d.shape[0]
    return pl.pallas_call(
        functools.partial(_select_kernel, t_now=t_now, nsel=nsel),
        out_shape=jax.ShapeDtypeStruct((rows, 128), jnp.int32),
        name="nsa_sample_select",
    )(imp2d)


def _nsa_sample_sel_kernel(idx_ref, pt_ref, *refs, n_past_blk, n_top):
    blk_refs = refs[:N_KV_HEADS * n_top]
    q_ref, ksn_ref, os_ref = refs[N_KV_HEADS * n_top:]
    del pt_ref
    b = pl.program_id(0)
    q8 = q_ref[...]
    q8f = q8.astype(F32)
    ksn = ksn_ref[...]
    blk_rows = SEL_BLOCK * KV_ROWS
    nrows = n_top * blk_rows
    lane = lax.broadcasted_iota(jnp.int32, (1, nrows), 1)
    row = lax.broadcasted_iota(jnp.int32, (N_HEADS, 1), 0)
    first = row < GROUP
    per_head = []
    for h in range(N_KV_HEADS):
        rows_b = jnp.concatenate([blk_refs[h * n_top + n][...] for n in range(n_top)], axis=0).astype(BF16)
        k_new = ksn[h:h + 1, :]
        v_new = ksn[N_KV_HEADS + h:N_KV_HEADS + h + 1, :]
        slot_ok = jnp.zeros((1, nrows), F32)
        new_ok = jnp.zeros((1, 1), F32)
        for n in range(n_top):
            ix = idx_ref[(b * N_KV_HEADS + h) * n_top + n]
            past_ok = jnp.where((ix >= 0) & (ix < n_past_blk), 1.0, 0.0)
            slot_ok = jnp.where((lane >= n * blk_rows) & (lane < (n + 1) * blk_rows), past_ok, slot_ok)
            new_ok = jnp.maximum(new_ok, jnp.where(ix >= n_past_blk, 1.0, 0.0))
        mask = jnp.broadcast_to((slot_ok > 0.5) & ((lane & (KV_ROWS - 1)) == h), (N_HEADS, nrows))
        new_mask = jnp.broadcast_to(new_ok > 0.5, (N_HEADS, 1))
        s = jnp.where(mask, _dot_nt(q8, rows_b), NEG)
        s_new = jnp.where(new_mask, jnp.sum(q8f * k_new, axis=-1, keepdims=True), NEG)
        mx = jnp.maximum(jnp.max(s, axis=-1, keepdims=True), s_new)
        p = jnp.exp(s - mx) * mask.astype(F32)
        p_new = jnp.exp(s_new - mx) * new_mask.astype(F32)
        den = jnp.maximum(jnp.sum(p, axis=-1, keepdims=True) + p_new, 1e-30)
        p_on_v = pltpu.roll(p, N_KV_HEADS, axis=1).astype(BF16)
        per_head.append((_dot(p_on_v, rows_b) + p_new * v_new) / den)
    os_ref[...] = jnp.where(first, per_head[0], per_head[1])


def _nsa_sample_sel(sel_idx, page_table, cache_sel, q8, ks_new):
    db, n_pages = page_table.shape
    n_past_blk = n_pages * PAGE_SIZE // SEL_BLOCK
    n_top = sel_idx.shape[-1]
    halves = PAGE_SIZE // SEL_BLOCK
    blocks = cache_sel.reshape(cache_sel.shape[0] * halves, SEL_BLOCK * KV_ROWS, HEAD_DIM)
    blk = jnp.clip(sel_idx.reshape(db, N_KV_HEADS * n_top), 0, n_past_blk - 1)
    phys = jnp.take_along_axis(page_table, blk // halves, axis=1) * halves + blk % halves

    def blk_spec(h, n):
        return pl.BlockSpec((None, SEL_BLOCK * KV_ROWS, HEAD_DIM),
                            lambda b, idx, pb: (pb[(b * N_KV_HEADS + h) * n_top + n], 0, 0))

    grid_spec = pltpu.PrefetchScalarGridSpec(
        num_scalar_prefetch=2,
        grid=(db,),
        in_specs=[blk_spec(h, n) for h in range(N_KV_HEADS) for n in range(n_top)] + [
            pl.BlockSpec((None, N_HEADS, HEAD_DIM), lambda b, idx, pt: (b, 0, 0)),
            pl.BlockSpec((None, KV_ROWS, HEAD_DIM), lambda b, idx, pt: (b, 0, 0))],
        out_specs=pl.BlockSpec((None, N_HEADS, HEAD_DIM), lambda b, idx, pt: (b, 0, 0)))
    return pl.pallas_call(
        functools.partial(_nsa_sample_sel_kernel, n_past_blk=n_past_blk, n_top=n_top),
        grid_spec=grid_spec,
        out_shape=jax.ShapeDtypeStruct((db, N_HEADS, HEAD_DIM), F32),
        compiler_params=pltpu.CompilerParams(vmem_limit_bytes=VMEM_LIMIT),
        name="nsa_sample_sel",
    )(sel_idx.reshape(-1), phys.reshape(-1), *([blocks] * (N_KV_HEADS * n_top)), q8, ks_new)


def _nsa_sample_win_kernel(win_ref, kwn_ref, q_ref, wout_ref, ow_ref, *, past):
    nrows = win_ref.shape[0]
    wb = nrows // KV_ROWS
    kwn = kwn_ref[...]
    x = win_ref[...]
    wout_ref[...] = pltpu.roll(x, nrows - KV_ROWS, axis=0)
    wout_ref[pl.ds(nrows - KV_ROWS, KV_ROWS), :] = kwn
    q8 = q_ref[...]
    q8f = q8.astype(F32)
    t_now = past
    lane = lax.broadcasted_iota(jnp.int32, (N_HEADS, nrows), 1)
    row = lax.broadcasted_iota(jnp.int32, (N_HEADS, nrows), 0)
    pos = past - wb + _shr(lane, KV_ROWS.bit_length() - 1)
    d = t_now - pos
    mask = ((d >= 0) & (d < WINDOW) & (pos >= 0)
            & ((lane & (KV_ROWS - 1)) == _shr(row, GROUP.bit_length() - 1)))
    first = lax.broadcasted_iota(jnp.int32, (N_HEADS, 1), 0) < GROUP
    k_new = jnp.where(first, kwn[0:1, :], kwn[1:2, :])
    v_new = jnp.where(first, kwn[N_KV_HEADS:N_KV_HEADS + 1, :], kwn[N_KV_HEADS + 1:N_KV_HEADS + 2, :])
    xb = x.astype(BF16)
    s = jnp.where(mask, _dot_nt(q8, xb), NEG)
    s_new = jnp.sum(q8f * k_new, axis=-1, keepdims=True)
    mx = jnp.maximum(jnp.max(s, axis=-1, keepdims=True), s_new)
    p = jnp.exp(s - mx) * mask.astype(F32)
    p_new = jnp.exp(s_new - mx)
    den = jnp.maximum(jnp.sum(p, axis=-1, keepdims=True) + p_new, 1e-30)
    p_on_v = pltpu.roll(p, N_KV_HEADS, axis=1).astype(BF16)
    ow_ref[...] = (_dot(p_on_v, xb) + p_new * v_new) / den


def _nsa_sample_win(win_rows, kw_new, q8, past):
    db, nrows, _ = win_rows.shape
    return pl.pallas_call(
        functools.partial(_nsa_sample_win_kernel, past=past),
        grid=(db,),
        in_specs=[pl.BlockSpec((None, nrows, HEAD_DIM), lambda b: (b, 0, 0)),
                  pl.BlockSpec((None, KV_ROWS, HEAD_DIM), lambda b: (b, 0, 0)),
                  pl.BlockSpec((None, N_HEADS, HEAD_DIM), lambda b: (b, 0, 0))],
        out_specs=[pl.BlockSpec((None, nrows, HEAD_DIM), lambda b: (b, 0, 0)),
                   pl.BlockSpec((None, N_HEADS, HEAD_DIM), lambda b: (b, 0, 0))],
        out_shape=[jax.ShapeDtypeStruct((db, nrows, HEAD_DIM), F32),
                   jax.ShapeDtypeStruct((db, N_HEADS, HEAD_DIM), F32)],
        name="nsa_sample_win",
    )(win_rows, kw_new, q8)


def _gate_mix_kernel(oc_ref, os_ref, ow_ref, g_ref, o_ref):
    g = g_ref[...]
    c = lax.broadcasted_iota(jnp.int32, (128, D_MODEL), 0)
    head = _shr(lax.broadcasted_iota(jnp.int32, (128, D_MODEL), 1), HEAD_SHIFT)
    acc = jnp.zeros(o_ref.shape, F32)
    for r, ref in enumerate((oc_ref, os_ref, ow_ref)):
        expand = (c == r * N_HEADS + head).astype(F32)
        acc = acc + ref[...] * jnp.dot(g, expand, precision=lax.Precision.HIGHEST, preferred_element_type=F32)
    o_ref[...] = acc


def _merge_kernel(x_ref, a_ref, ra_ref, b_ref, on_ref, gate_ref, w_ref, y_ref):
    u = a_ref[...] * ra_ref[...] + b_ref[...] * on_ref[...]
    y_ref[...] = x_ref[...] + gate_ref[0] * _dot(u.astype(BF16), w_ref[...])


def _merge(x2d, a, ra, b, o_nsa, gate, w_out, tm, rows_per_mod):
    m = x2d.shape[0]
    tiles_per_mod = rows_per_mod // tm
    row_spec = pl.BlockSpec((tm, D_MODEL), lambda i: (i, 0))
    return pl.pallas_call(
        _merge_kernel,
        grid=(m // tm,),
        in_specs=[row_spec] * 5 + [
            pl.BlockSpec((1, gate.shape[1], D_MODEL), lambda i: (i // tiles_per_mod, 0, 0)),
            pl.BlockSpec((D_MODEL, D_MODEL), lambda i: (0, 0))],
        out_specs=row_spec,
        out_shape=jax.ShapeDtypeStruct((m, D_MODEL), F32),
        compiler_params=pltpu.CompilerParams(vmem_limit_bytes=VMEM_LIMIT),
        name="merge_out_proj",
    )(x2d, a, ra, b, o_nsa, gate, w_out)


def _rearranged_w_in(w_in):
    sizes = (D_MODEL, D_MODEL, D_MODEL, KV_W, KV_W, KV_W, 3 * N_HEADS, D_MODEL, D_MODEL, D_MODEL)
    offs = [0]
    for s in sizes:
        offs.append(offs[-1] + s)
    xr, zr, q, kc, ks, kw, bg, zn, ga, gb = [w_in[:, offs[i]:offs[i + 1]] for i in range(len(sizes))]
    pad = jnp.zeros((D_MODEL, C_END - C_BG - 3 * N_HEADS), w_in.dtype)
    return jnp.concatenate([xr, zr, q, kc, ks, kw, zn, ga, gb, bg, pad], axis=1).astype(BF16)


def kernel(x_prompt, x_sample, c_prompt, c_sample, state_conv, state_rglru, cache_cmp_kv, cache_sel_kv,
           state_win_kv, page_table, norm_g, w_ada, b_ada, w_in, conv_w, conv_b, rg_wa, rg_ba, rg_wx, rg_bx,
           rg_lambda, q_norm_g, k_norm_g, cmp_pool_w, cmp_w1, cmp_w2, w_out):
    depth = norm_g.shape[0]
    assert depth == 1 and x_sample.shape[1] == 1
    bsz, seq, _ = x_prompt.shape
    db = x_sample.shape[0]
    n_pages = page_table.shape[1]
    past = n_pages * PAGE_SIZE
    layer = 0

    w_cat = _rearranged_w_in(w_in[layer])
    w_out_b = w_out[layer].astype(BF16)
    wa_b = rg_wa[layer].astype(BF16)
    wx_b = rg_wx[layer].astype(BF16)
    w1_b = cmp_w1[layer].astype(BF16)
    w2_b = cmp_w2[layer].astype(BF16)
    row = lambda v: v.reshape(1, -1)
    rg_args = (conv_w[layer], row(conv_b[layer]), wa_b, row(rg_ba[layer]), wx_b, row(rg_bx[layer]),
               row(rg_lambda[layer]))
    praw = jnp.broadcast_to(
        cmp_pool_w[layer].reshape(2, 2, CMP_STRIDE).transpose(1, 2, 0)[:, :, :, None, None],
        (2, CMP_STRIDE, 2, N_KV_HEADS, HEAD_DIM)).reshape(2 * CHUNK_ROWS, HEAD_DIM)
    kg = k_norm_g[layer]

    mod = _modulation(jnp.concatenate([c_prompt, c_sample], axis=0), w_ada[layer], b_ada[layer])
    shift, scale, gate = mod[:, :D_MODEL], mod[:, D_MODEL:2 * D_MODEL], mod[:, 2 * D_MODEL:]

    xp2 = x_prompt.reshape(bsz * seq, D_MODEL)
    pm = lambda v: v[:bsz].reshape(bsz, 1, D_MODEL)
    (xr_p, a_p, b_p, q_p, kc_p, ks_p, kw_p, skb_p, svt_p, wkb_p, wvt_p, g_p) = _project(
        xp2, pm(shift), pm(scale), norm_g[layer], w_cat, q_norm_g[layer], kg, tm=256, rows_per_mod=seq)
    ra_p, h_p = _rglru_prompt(xr_p, bsz, seq, *rg_args)
    ck_p, cvt_p = _compress_prompt(kc_p, praw, w1_b, w2_b, kg, bsz, seq)
    on_p = _nsa_prompt(q_p, ck_p, cvt_p, skb_p, svt_p, wkb_p, wvt_p, g_p, bsz, seq)
    y_p = _merge(xp2, a_p, ra_p, b_p, on_p, pm(gate), w_out_b, tm=256, rows_per_mod=seq)

    xs2 = x_sample.reshape(db, D_MODEL)
    sm = lambda v: v[bsz:].reshape(1, db, D_MODEL)
    (xr_s, a_s, b_s, q_s, kc_s, ks_s, kw_s, _, _, _, _, g_s) = _project(
        xs2, sm(shift), sm(scale), norm_g[layer], w_cat, q_norm_g[layer], kg, tm=db, rows_per_mod=db)
    h_s = _rglru_step(xr_s, state_conv[layer].reshape(db, (CONV_W - 1) * D_MODEL), state_rglru[layer], *rg_args)
    q8 = q_s.reshape(db, N_HEADS, HEAD_DIM)
    new_rows = lambda v: v.reshape(db, KV_ROWS, HEAD_DIM)
    n_phys = cache_cmp_kv.shape[1]
    oc_s, imp = _nsa_sample_cmp(page_table, cache_cmp_kv.reshape(depth * n_phys, PAGE_SIZE * KV_ROWS, HEAD_DIM),
                                new_rows(kc_s), praw, w1_b, w2_b, kg, q8)
    nsel = past // SEL_BLOCK + 1
    sel_idx = _select(imp.reshape(db * N_KV_HEADS, 256), past, nsel)[:, :min(TOP_N, nsel)]
    os_s = _nsa_sample_sel(sel_idx, page_table,
                           cache_sel_kv.reshape(depth * n_phys, PAGE_SIZE * KV_ROWS, HEAD_DIM), q8, new_rows(ks_s))
    wb = state_win_kv.shape[2]
    win_s, ow_s = _nsa_sample_win(state_win_kv.reshape(depth * db, wb * KV_ROWS, HEAD_DIM), new_rows(kw_s), q8, past)
    on_s = pl.pallas_call(
        _gate_mix_kernel, out_shape=jax.ShapeDtypeStruct((db, D_MODEL), F32), name="nsa_sample_mix",
    )(oc_s.reshape(db, D_MODEL), os_s.reshape(db, D_MODEL), ow_s.reshape(db, D_MODEL), g_s)
    y_s = _merge(xs2, a_s, h_s, b_s, on_s, sm(gate), w_out_b, tm=db, rows_per_mod=db)

    kv_shape = (2, N_KV_HEADS, HEAD_DIM)
    xr_p3 = xr_p.reshape(bsz, seq, D_MODEL)
    conv_prompt = xr_p3[:, seq - (CONV_W - 1):][None]
    conv_sample = jnp.concatenate([state_conv[layer][:, 1:], xr_s[:, None, :]], axis=1)[None]
    win_len = min(WINDOW, seq)
    return (y_p.reshape(bsz, seq, D_MODEL), y_s.reshape(db, 1, D_MODEL),
            conv_prompt, conv_sample,
            h_p.reshape(1, bsz, D_MODEL), h_s.reshape(1, db, D_MODEL),
            kc_p.reshape(1, bsz, seq, *kv_shape), kc_s.reshape(1, db, 1, *kv_shape),
            ks_p.reshape(1, bsz, seq, *kv_shape), ks_s.reshape(1, db, 1, *kv_shape),
            kw_p.reshape(bsz, seq * KV_ROWS, HEAD_DIM)[:, (seq - win_len) * KV_ROWS:].reshape(
                1, bsz, win_len, *kv_shape),
            win_s.reshape(1, db, wb, *kv_shape))
```

```python
import functools

import jax
import jax.numpy as jnp
from jax import lax
from jax.experimental import pallas as pl
from jax.experimental.pallas import tpu as pltpu

F32 = jnp.float32
BF16 = jnp.bfloat16

D_MODEL = 1024
RG_BLOCKS = 8
RG_BW = D_MODEL // RG_BLOCKS
RG_C = 8.0
CONV_W = 4
N_HEADS = 8
N_KV_HEADS = 2
GROUP = N_HEADS // N_KV_HEADS
HEAD_DIM = D_MODEL // N_HEADS
KV_W = 2 * N_KV_HEADS * HEAD_DIM
KV_ROWS = 2 * N_KV_HEADS
CMP_BLOCK = 32
CMP_STRIDE = 16
SEL_BLOCK = 64
TOP_N = 16
N_LOCAL_BLOCKS = 2
WINDOW = 512
Q_BLOCK = 128
PAGE_SIZE = 128
EPS = 1e-6
NEG = -1e30
FORCE = 1e4

C_XR, C_ZR, C_Q, C_KC, C_KS, C_KW, C_ZN, C_GA, C_GB, C_BG, C_END = (
    0, 1024, 2048, 3072, 3584, 4096, 4608, 5632, 6656, 7680, 7808)
VMEM_LIMIT = 56 * 1024 * 1024
PAGES_PER_STEP = 64
WIN_SEQS_PER_STEP = 4
SEL_SEQS_PER_STEP = 2

SEL_SHIFT = SEL_BLOCK.bit_length() - 1
HEAD_SHIFT = HEAD_DIM.bit_length() - 1


def _shr(x, k):
    return lax.shift_right_arithmetic(x, jnp.int32(k))


def _sigmoid(x):
    return jax.nn.sigmoid(x)


def _dot(a, b):
    return jnp.dot(a, b, preferred_element_type=F32)


def _dot_nt(a, b):
    return lax.dot_general(a, b, (((1,), (1,)), ((), ())), preferred_element_type=F32)


def _rms_rows(x, g):
    return x * lax.rsqrt(jnp.mean(x * x, axis=-1, keepdims=True) + EPS) * g


def _mod_kernel(c_ref, w_ref, b_ref, o_ref):
    c = c_ref[...]
    s = c * _sigmoid(c)
    o_ref[...] = _dot(s.astype(BF16), w_ref[...].astype(BF16)) + b_ref[...]


def _modulation(c_all, w_ada, b_ada):
    n = c_all.shape[0]
    return pl.pallas_call(
        _mod_kernel,
        grid=(3,),
        in_specs=[pl.BlockSpec((n, D_MODEL), lambda j: (0, 0)),
                  pl.BlockSpec((D_MODEL, D_MODEL), lambda j: (0, j)),
                  pl.BlockSpec((1, D_MODEL), lambda j: (0, j))],
        out_specs=pl.BlockSpec((n, D_MODEL), lambda j: (0, j)),
        out_shape=jax.ShapeDtypeStruct((n, 3 * D_MODEL), F32),
        name="adaln_mod",
    )(c_all, w_ada, b_ada.reshape(1, 3 * D_MODEL))


def _proj_kernel(x_ref, shift_ref, scale_ref, ng_ref, w_lo_ref, w_hi_ref, w_bg_ref, qg_ref, kg_ref,
                 xr_ref, a_ref, b_ref, q_ref, kc_ref, ks_ref, kw_ref, skb_ref, svt_ref, wkb_ref, wvt_ref, g_ref):
    x = x_ref[...]
    h = _rms_rows(x, ng_ref[...])
    h = h * (1.0 + scale_ref[0]) + shift_ref[0]
    hb = h.astype(BF16)

    def mm(lo, hi):
        if hi <= C_ZN:
            return _dot(hb, w_lo_ref[:, lo:hi])
        if hi <= C_BG:
            return _dot(hb, w_hi_ref[:, lo - C_ZN:hi - C_ZN])
        return _dot(hb, w_bg_ref[...])

    xr_ref[...] = mm(C_XR, C_ZR)
    zr = mm(C_ZR, C_Q)
    ga = mm(C_GA, C_GB)
    a_ref[...] = _sigmoid(ga) * (zr * _sigmoid(zr))
    zn = mm(C_ZN, C_GA)
    gb = mm(C_GB, C_BG)
    b_ref[...] = _sigmoid(gb) * (zn * _sigmoid(zn))
    q = mm(C_Q, C_KC)
    for hd in range(N_HEADS):
        sl = slice(hd * HEAD_DIM, (hd + 1) * HEAD_DIM)
        q_ref[:, sl] = (_rms_rows(q[:, sl], qg_ref[...]) * (HEAD_DIM ** -0.5)).astype(BF16)
    tm = x.shape[0]

    def store_kv(o_ref, eh, val):
        o_ref[pl.ds(eh, tm, stride=KV_ROWS), :] = val

    kc = mm(C_KC, C_KS)
    for eh in range(KV_ROWS):
        store_kv(kc_ref, eh, kc[:, eh * HEAD_DIM:(eh + 1) * HEAD_DIM])
    for lo, hi, o_ref, kb_ref, vt_ref, gi in ((C_KS, C_KW, ks_ref, skb_ref, svt_ref, 1),
                                              (C_KW, C_ZN, kw_ref, wkb_ref, wvt_ref, 2)):
        kv = mm(lo, hi)
        for hd in range(N_KV_HEADS):
            sl = slice(hd * HEAD_DIM, (hd + 1) * HEAD_DIM)
            kn = _rms_rows(kv[:, sl], kg_ref[gi:gi + 1, :])
            store_kv(o_ref, hd, kn)
            kb_ref[:, sl] = kn.astype(BF16)
            v = kv[:, (N_KV_HEADS + hd) * HEAD_DIM:(N_KV_HEADS + hd + 1) * HEAD_DIM]
            store_kv(o_ref, N_KV_HEADS + hd, v)
            v_t = v.T.astype(BF16)
            for j in range(tm // Q_BLOCK):
                vt_ref[j, sl, :] = v_t[:, j * Q_BLOCK:(j + 1) * Q_BLOCK]
    g_ref[...] = _sigmoid(mm(C_BG, C_END))


def _project(x2d, shift, scale, norm_g, w_parts, q_norm_g, k_norm_g, tm, rows_per_mod):
    m = x2d.shape[0]
    tiles_per_mod = rows_per_mod // tm
    mod_rows = shift.shape[1]
    mod_spec = pl.BlockSpec((1, mod_rows, D_MODEL), lambda i: (i // tiles_per_mod, 0, 0))

    def row_spec(width, mult=1):
        return pl.BlockSpec((tm * mult, width), lambda i: (i, 0))

    def const_spec(shape):
        return pl.BlockSpec(shape, lambda i: (0,) * len(shape))

    kv_heads_w = N_KV_HEADS * HEAD_DIM
    k_bf16 = (pl.BlockSpec((tm, kv_heads_w), lambda i: (i, 0)), jax.ShapeDtypeStruct((m, kv_heads_w), BF16))
    vt_bf16 = (pl.BlockSpec((tm // Q_BLOCK, kv_heads_w, Q_BLOCK), lambda i: (i, 0, 0)),
               jax.ShapeDtypeStruct((m // Q_BLOCK, kv_heads_w, Q_BLOCK), BF16))

    def rows(width, dtype, mult=1):
        return row_spec(width, mult), jax.ShapeDtypeStruct((m * mult, width), dtype)

    outs = (rows(D_MODEL, F32), rows(D_MODEL, F32), rows(D_MODEL, F32), rows(D_MODEL, BF16),
            rows(HEAD_DIM, F32, KV_ROWS), rows(HEAD_DIM, F32, KV_ROWS), rows(HEAD_DIM, F32, KV_ROWS),
            k_bf16, vt_bf16, k_bf16, vt_bf16, rows(128, F32))
    return pl.pallas_call(
        _proj_kernel,
        grid=(m // tm,),
        in_specs=[row_spec(D_MODEL), mod_spec, mod_spec, const_spec((1, D_MODEL)),
                  pl.BlockSpec((D_MODEL, C_ZN), lambda i: (0, 0), pipeline_mode=pl.Buffered(1)),
                  pl.BlockSpec((D_MODEL, C_BG - C_ZN), lambda i: (0, 0), pipeline_mode=pl.Buffered(1)),
                  const_spec((D_MODEL, C_END - C_BG)),
                  const_spec((1, HEAD_DIM)), const_spec((3, HEAD_DIM))],
        out_specs=[spec for spec, _ in outs],
        out_shape=[shape for _, shape in outs],
        compiler_params=pltpu.CompilerParams(vmem_limit_bytes=VMEM_LIMIT),
        name="in_proj",
    )(x2d, shift, scale, norm_g.reshape(1, D_MODEL), *w_parts, q_norm_g.reshape(1, HEAD_DIM), k_norm_g)


def _softplus(z):
    return jnp.maximum(z, 0.0) + jnp.log1p(jnp.exp(-jnp.abs(z)))


def _rglru_coeffs(xc, wa_ref, ba_ref, wx_ref, bx_ref, lam_ref, a_out, b_out):
    xcb = xc.astype(BF16)
    sp = _softplus(-lam_ref[...])
    for k in range(RG_BLOCKS):
        sl = slice(k * RG_BW, (k + 1) * RG_BW)
        r = _sigmoid(_dot(xcb[:, sl], wa_ref[k]) + ba_ref[:, sl])
        i = _sigmoid(_dot(xcb[:, sl], wx_ref[k]) + bx_ref[:, sl])
        log_a = -RG_C * r * sp[:, sl]
        a = jnp.exp(log_a)
        a_out[:, sl] = a
        b_out[:, sl] = jnp.sqrt(-jnp.tanh(log_a) * (a * a + 1.0)) * i * xc[:, sl]


def _rglru_prompt_kernel(xr_ref, cw_ref, cb_ref, wa_ref, ba_ref, wx_ref, bx_ref, lam_ref,
                         ra_ref, hl_ref, ext_s, a_s, b_s, h_s):
    t_len = xr_ref.shape[0]

    @pl.when(pl.program_id(1) == 0)
    def _():
        ext_s[0:8, :] = jnp.zeros((8, D_MODEL), F32)
        h_s[...] = jnp.zeros((1, D_MODEL), F32)

    x = xr_ref[...]
    ext_s[8:8 + t_len, :] = x
    xc = ext_s[pl.ds(5, t_len), :] * cw_ref[0:1, :] + cb_ref[...]
    xc = xc + ext_s[pl.ds(6, t_len), :] * cw_ref[1:2, :]
    xc = xc + ext_s[pl.ds(7, t_len), :] * cw_ref[2:3, :]
    xc = xc + x * cw_ref[3:4, :]
    ext_s[0:8, :] = x[t_len - 8:t_len, :]
    _rglru_coeffs(xc, wa_ref, ba_ref, wx_ref, bx_ref, lam_ref, a_s, b_s)

    row = lax.broadcasted_iota(jnp.int32, (8, D_MODEL), 0)

    def tile(i, h):
        r0 = pl.multiple_of(i * 8, 8)
        a = a_s[pl.ds(r0, 8), :]
        b = b_s[pl.ds(r0, 8), :]
        for s in (1, 2, 4):
            keep = row >= s
            b = jnp.where(keep, a * pltpu.roll(b, s, axis=0) + b, b)
            a = jnp.where(keep, a * pltpu.roll(a, s, axis=0), a)
        hh = a * h + b
        ra_ref[pl.ds(r0, 8), :] = hh
        return hh[7:8, :]

    h = lax.fori_loop(0, t_len // 8, tile, h_s[...], unroll=2)
    h_s[...] = h
    hl_ref[...] = h


def _rglru_prompt(xr, bsz, seq, cw, cb, wa, ba, wx, bx, lam, t_chunk=512):
    nchunk = seq // t_chunk

    def const_spec(shape):
        return pl.BlockSpec(shape, lambda b, c: (0,) * len(shape))

    return pl.pallas_call(
        _rglru_prompt_kernel,
        grid=(bsz, nchunk),
        in_specs=[pl.BlockSpec((t_chunk, D_MODEL), lambda b, c: (b * nchunk + c, 0)),
                  const_spec((CONV_W, D_MODEL)), const_spec((1, D_MODEL)),
                  const_spec((RG_BLOCKS, RG_BW, RG_BW)), const_spec((1, D_MODEL)),
                  const_spec((RG_BLOCKS, RG_BW, RG_BW)), const_spec((1, D_MODEL)),
                  const_spec((1, D_MODEL))],
        out_specs=[pl.BlockSpec((t_chunk, D_MODEL), lambda b, c: (b * nchunk + c, 0)),
                   pl.BlockSpec((None, 1, D_MODEL), lambda b, c: (b, 0, 0))],
        out_shape=[jax.ShapeDtypeStruct((bsz * seq, D_MODEL), F32),
                   jax.ShapeDtypeStruct((bsz, 1, D_MODEL), F32)],
        scratch_shapes=[pltpu.VMEM((t_chunk + 8, D_MODEL), F32), pltpu.VMEM((t_chunk, D_MODEL), F32),
                        pltpu.VMEM((t_chunk, D_MODEL), F32), pltpu.VMEM((1, D_MODEL), F32)],
        compiler_params=pltpu.CompilerParams(vmem_limit_bytes=VMEM_LIMIT),
        name="rglru_prompt",
    )(xr, cw, cb, wa, ba, wx, bx, lam)


def _rglru_step_kernel(xr_ref, cbuf_ref, h0_ref, cw_ref, cb_ref, wa_ref, ba_ref, wx_ref, bx_ref, lam_ref,
                       h_ref, a_s, b_s):
    xc = cbuf_ref[:, 0:D_MODEL] * cw_ref[0:1, :] + cb_ref[...]
    xc = xc + cbuf_ref[:, D_MODEL:2 * D_MODEL] * cw_ref[1:2, :]
    xc = xc + cbuf_ref[:, 2 * D_MODEL:3 * D_MODEL] * cw_ref[2:3, :]
    xc = xc + xr_ref[...] * cw_ref[3:4, :]
    _rglru_coeffs(xc, wa_ref, ba_ref, wx_ref, bx_ref, lam_ref, a_s, b_s)
    h_ref[...] = a_s[...] * h0_ref[...] + b_s[...]


def _rglru_step(xr, cbuf, h0, cw, cb, wa, ba, wx, bx, lam):
    n = xr.shape[0]
    return pl.pallas_call(
        _rglru_step_kernel,
        out_shape=jax.ShapeDtypeStruct((n, D_MODEL), F32),
        scratch_shapes=[pltpu.VMEM((n, D_MODEL), F32), pltpu.VMEM((n, D_MODEL), F32)],
        name="rglru_step",
    )(xr, cbuf, h0, cw, cb, wa, ba, wx, bx, lam)


CHUNK_ROWS = CMP_STRIDE * KV_ROWS


def _pool_weight_rows(praw_ref):
    praw = praw_ref[...]
    row = lax.broadcasted_iota(jnp.int32, praw.shape, 0)
    is_k = (row & (KV_ROWS - 1)) < N_KV_HEADS
    m_k = jnp.max(jnp.where(is_k, praw, -3e38), axis=0, keepdims=True)
    m_v = jnp.max(jnp.where(is_k, -3e38, praw), axis=0, keepdims=True)
    ex = jnp.exp(praw - jnp.where(is_k, m_k, m_v))
    s_k = jnp.sum(jnp.where(is_k, ex, 0.0), axis=0, keepdims=True) * (1.0 / N_KV_HEADS)
    s_v = jnp.sum(jnp.where(is_k, 0.0, ex), axis=0, keepdims=True) * (1.0 / N_KV_HEADS)
    return ex / jnp.where(is_k, s_k, s_v)


def _pool_rows(x, wv):
    n = x.shape[0] // CHUNK_ROWS
    x4 = x.reshape(n, CHUNK_ROWS // 8, 8, HEAD_DIM)
    w4 = wv.reshape(2, CHUNK_ROWS // 8, 8, HEAD_DIM)
    p0 = x4[:, 0] * w4[0, 0]
    p1 = x4[:, 0] * w4[1, 0]
    for v in range(1, CHUNK_ROWS // 8):
        p0 = p0 + x4[:, v] * w4[0, v]
        p1 = p1 + x4[:, v] * w4[1, v]
    return p0.reshape(n * 8, HEAD_DIM), p1.reshape(n * 8, HEAD_DIM)


def _pooled_head(p0_s, p1_s, eh, nc):
    def col(ref, start):
        return ref[pl.ds(start, nc, stride=8), :]
    return (col(p0_s, eh) + col(p0_s, eh + KV_ROWS)) + (col(p1_s, 8 + eh) + col(p1_s, 8 + eh + KV_ROWS))


def _compress_mlp(p0_s, p1_s, nc, w1_ref, w2_ref, kg_ref):
    outs = []
    for e in range(2):
        per_head = []
        for hd in range(N_KV_HEADS):
            p = _pooled_head(p0_s, p1_s, e * N_KV_HEADS + hd, nc)
            hid = _dot(p.astype(BF16), w1_ref[e])
            hid = hid * _sigmoid(hid)
            comp = p + _dot(hid.astype(BF16), w2_ref[e])
            if e == 0:
                comp = _rms_rows(comp, kg_ref[0:1, :])
            per_head.append(comp)
        outs.append(jnp.concatenate(per_head, axis=1))
    return outs[0], outs[1]


def _masked_softmax_parts(s, mask):
    sm = jnp.where(mask, s, NEG)
    p = jnp.exp(sm - jnp.max(sm, axis=-1, keepdims=True)) * mask.astype(F32)
    return p, jnp.maximum(jnp.sum(p, axis=-1, keepdims=True), 1e-30)


POOL_SLAB = 8


def _compress_prompt_kernel(x_ref, praw_ref, w1_ref, w2_ref, kg_ref, ck_ref, cvt_ref, p0_s, p1_s):
    nch = x_ref.shape[0] // CHUNK_ROWS
    wv = _pool_weight_rows(praw_ref)

    def slab(i, _):
        x = x_ref[pl.ds(pl.multiple_of(i * (POOL_SLAB * CHUNK_ROWS), POOL_SLAB * CHUNK_ROWS),
                        POOL_SLAB * CHUNK_ROWS), :]
        p0, p1 = _pool_rows(x, wv)
        rows = pl.ds(pl.multiple_of(i * (POOL_SLAB * 8), POOL_SLAB * 8), POOL_SLAB * 8)
        p0_s[rows, :] = p0
        p1_s[rows, :] = p1
        return 0

    lax.fori_loop(0, nch // POOL_SLAB, slab, 0)
    p1_s[pl.ds(nch * 8, 8), :] = jnp.zeros((8, HEAD_DIM), F32)
    ck, cv = _compress_mlp(p0_s, p1_s, nch, w1_ref, w2_ref, kg_ref)
    ck_ref[...] = ck
    cvt_ref[...] = cv.T


def _compress_prompt(kc_rows, praw, w1, w2, kg, bsz, seq):
    nch = seq // CMP_STRIDE

    def const_spec(shape):
        return pl.BlockSpec(shape, lambda b: (0,) * len(shape))

    return pl.pallas_call(
        _compress_prompt_kernel,
        grid=(bsz,),
        in_specs=[pl.BlockSpec((seq * KV_ROWS, HEAD_DIM), lambda b: (b, 0)),
                  const_spec((2 * CHUNK_ROWS, HEAD_DIM)), const_spec((2, HEAD_DIM, HEAD_DIM)),
                  const_spec((2, HEAD_DIM, HEAD_DIM)), const_spec((3, HEAD_DIM))],
        out_specs=[pl.BlockSpec((None, nch, 2 * HEAD_DIM), lambda b: (b, 0, 0)),
                   pl.BlockSpec((None, 2 * HEAD_DIM, nch), lambda b: (b, 0, 0))],
        out_shape=[jax.ShapeDtypeStruct((bsz, nch, 2 * HEAD_DIM), F32),
                   jax.ShapeDtypeStruct((bsz, 2 * HEAD_DIM, nch), F32)],
        scratch_shapes=[pltpu.VMEM(((nch + 1) * 8, HEAD_DIM), F32)] * 2,
        name="compress_prompt",
    )(kc_rows, praw, w1, w2, kg)


GQ = GROUP * Q_BLOCK
WIN_BLOCKS = WINDOW // Q_BLOCK + 1


def _bias_groups(s_t, bias):
    return jnp.concatenate([s_t[:, g * Q_BLOCK:(g + 1) * Q_BLOCK] + bias for g in range(GROUP)], axis=1)


def _fold8(x, op):
    parts = [x[i * 8:(i + 1) * 8, :] for i in range(x.shape[0] // 8)]
    while len(parts) > 1:
        parts = [op(parts[i], parts[i + 1]) for i in range(0, len(parts), 2)]
    return parts[0]


def _finish_t(o_t, m, l8):
    den = jnp.maximum(jnp.sum(l8, axis=0, keepdims=True), 1e-30)
    return jnp.where(m > 0.5 * NEG, o_t / den, 0.0)


def _nsa_prompt_kernel(q_ref, ck_ref, cvt_ref, sk_ref, svt_ref, wk_ref, wvt_ref, g_ref, o_ref,
                       bias_s, ps_s, pw_s):
    qi = pl.program_id(1)
    nq = Q_BLOCK
    ncp = ck_ref.shape[0]
    nkb = svt_ref.shape[0]
    nsel = nkb * (Q_BLOCK // SEL_BLOCK)
    n_top = min(TOP_N, nsel)
    start = qi * nq
    t_row = start + lax.broadcasted_iota(jnp.int32, (1, nq), 1)
    gates_t = g_ref[...].T

    n_sub = lax.broadcasted_iota(jnp.int32, (ncp, nq), 0)
    cmp_mask = (n_sub * CMP_STRIDE + (CMP_BLOCK - 1) <= t_row) & (n_sub < ncp - 1)
    jj = lax.broadcasted_iota(jnp.int32, (128, ncp), 0)
    nn = lax.broadcasted_iota(jnp.int32, (128, ncp), 1)
    ov_t = ((nn * CMP_STRIDE <= jj * SEL_BLOCK + SEL_BLOCK - 1)
            & (nn * CMP_STRIDE + CMP_BLOCK - 1 >= jj * SEL_BLOCK)
            & (nn < ncp - 1) & (jj < nsel)).astype(F32)
    j_col = lax.broadcasted_iota(jnp.int32, (128, nq), 0)
    cur = _shr(t_row, SEL_SHIFT)
    valid_t = (j_col <= cur) & (j_col < nsel)
    forced_t = (j_col == 0) | (valid_t & (j_col > cur - N_LOCAL_BLOCKS))
    key_sub = lax.broadcasted_iota(jnp.int32, (Q_BLOCK, nq), 0)
    q_lane = lax.broadcasted_iota(jnp.int32, (Q_BLOCK, nq), 1)
    win_bias = {0: jnp.where(q_lane < key_sub, 0.0, NEG), WIN_BLOCKS - 1: jnp.where(q_lane >= key_sub, 0.0, NEG)}

    for h in range(N_KV_HEADS):
        qh = jnp.concatenate(
            [q_ref[:, (h * GROUP + g) * HEAD_DIM:(h * GROUP + g + 1) * HEAD_DIM] for g in range(GROUP)], axis=0)
        hs = slice(h * HEAD_DIM, (h + 1) * HEAD_DIM)

        s_c = _dot_nt(ck_ref[:, hs].astype(BF16), qh)
        p_groups = []
        for g in range(GROUP):
            sm = jnp.where(cmp_mask, s_c[:, g * nq:(g + 1) * nq], NEG)
            p = jnp.exp(sm - jnp.max(sm, axis=0, keepdims=True)) * cmp_mask.astype(F32)
            p_groups.append(p / jnp.maximum(jnp.sum(p, axis=0, keepdims=True), 1e-30))
        o_c = _dot(cvt_ref[hs, :].astype(BF16), jnp.concatenate(p_groups, axis=1).astype(BF16))
        psum = p_groups[0] + p_groups[1] + p_groups[2] + p_groups[3]
        imp_t = jnp.dot(ov_t, psum, precision=lax.Precision.HIGHEST, preferred_element_type=F32)
        score_t = jnp.where(valid_t, imp_t + FORCE * forced_t.astype(F32), NEG)
        rank = jnp.zeros((128, nq), jnp.int32)
        for k in range(nsel):
            rk = score_t[k:k + 1, :]
            beats = (rk > score_t) | ((rk == score_t) & (j_col > k))
            rank = rank + beats.astype(jnp.int32)
        sel_t = (rank < n_top) & (score_t > 0.5 * NEG)
        sel_f = sel_t.astype(F32)

        def sel_branch(nblk):
            def run():
                for kb in range(nblk):
                    lo = sel_f[2 * kb:2 * kb + 1, :]
                    hi = sel_f[2 * kb + 1:2 * kb + 2, :]
                    picked = jnp.where(key_sub < SEL_BLOCK, lo, hi) > 0.5
                    bias_s[kb] = jnp.where(picked & (kb * Q_BLOCK + key_sub <= t_row), 0.0, NEG)

                def scores(kb):
                    return _bias_groups(_dot_nt(sk_ref[kb * Q_BLOCK:(kb + 1) * Q_BLOCK, hs], qh), bias_s[kb])

                m8 = jnp.full((8, GQ), NEG, F32)
                for kb in range(nblk):
                    m8 = jnp.maximum(m8, _fold8(scores(kb), jnp.maximum))
                m_s = jnp.max(m8, axis=0, keepdims=True)
                l_s = jnp.zeros((8, GQ), F32)
                for kb in range(nblk):
                    p = jnp.exp(scores(kb) - m_s)
                    ps_s[kb * Q_BLOCK:(kb + 1) * Q_BLOCK, :] = p.astype(BF16)
                    l_s = l_s + _fold8(p, jnp.add)
                v_sel = jnp.concatenate([svt_ref[kb, hs, :] for kb in range(nblk)], axis=1)
                return _finish_t(_dot(v_sel, ps_s[0:nblk * Q_BLOCK, :]), m_s, l_s)
            return run

        quarter = nkb // 4
        o_s = lax.switch(qi // quarter, [sel_branch((v + 1) * quarter) for v in range(4)])

        def win_scores(r):
            kb = qi - (WIN_BLOCKS - 1) + r
            kbc = jnp.maximum(kb, 0)
            off = pl.multiple_of(kbc * Q_BLOCK, Q_BLOCK)
            s_t = _dot_nt(wk_ref[pl.ds(off, Q_BLOCK), hs], qh) + jnp.where(kb < 0, NEG, 0.0)
            if r in win_bias:
                s_t = _bias_groups(s_t, win_bias[r])
            return s_t, kbc

        m8 = jnp.full((8, GQ), NEG, F32)
        for r in range(WIN_BLOCKS):
            m8 = jnp.maximum(m8, _fold8(win_scores(r)[0], jnp.maximum))
        m_w = jnp.max(m8, axis=0, keepdims=True)
        l_w = jnp.zeros((8, GQ), F32)
        v_blocks = []
        for r in range(WIN_BLOCKS):
            s_t, kbc = win_scores(r)
            p = jnp.exp(s_t - m_w)
            pw_s[r * Q_BLOCK:(r + 1) * Q_BLOCK, :] = p.astype(BF16)
            l_w = l_w + _fold8(p, jnp.add)
            v_blocks.append(wvt_ref[kbc, hs, :])
        o_w = _finish_t(_dot(jnp.concatenate(v_blocks, axis=1), pw_s[...]), m_w, l_w)

        for g in range(GROUP):
            head = h * GROUP + g
            cs = slice(g * nq, (g + 1) * nq)
            o = (o_c[:, cs] * gates_t[head:head + 1, :]
                 + o_s[:, cs] * gates_t[N_HEADS + head:N_HEADS + head + 1, :]
                 + o_w[:, cs] * gates_t[2 * N_HEADS + head:2 * N_HEADS + head + 1, :])
            o_ref[:, head * HEAD_DIM:(head + 1) * HEAD_DIM] = o.T


def _nsa_prompt(q, ck, cvt, sk, svt, wk, wvt, gates, bsz, seq):
    nqb = seq // Q_BLOCK
    ncp = ck.shape[1]
    kv_heads_w = N_KV_HEADS * HEAD_DIM
    k_spec = pl.BlockSpec((seq, kv_heads_w), lambda b, i: (b, 0))
    vt_spec = pl.BlockSpec((nqb, kv_heads_w, Q_BLOCK), lambda b, i: (b, 0, 0))
    return pl.pallas_call(
        _nsa_prompt_kernel,
        grid=(bsz, nqb),
        in_specs=[pl.BlockSpec((Q_BLOCK, D_MODEL), lambda b, i: (b * nqb + i, 0)),
                  pl.BlockSpec((None, ncp, kv_heads_w), lambda b, i: (b, 0, 0)),
                  pl.BlockSpec((None, kv_heads_w, ncp), lambda b, i: (b, 0, 0)),
                  k_spec, vt_spec, k_spec, vt_spec,
                  pl.BlockSpec((Q_BLOCK, 128), lambda b, i: (b * nqb + i, 0))],
        out_specs=pl.BlockSpec((Q_BLOCK, D_MODEL), lambda b, i: (b * nqb + i, 0)),
        out_shape=jax.ShapeDtypeStruct((bsz * seq, D_MODEL), F32),
        scratch_shapes=[pltpu.VMEM((nqb, Q_BLOCK, Q_BLOCK), F32),
                        pltpu.VMEM((seq, GQ), BF16),
                        pltpu.VMEM((WIN_BLOCKS * Q_BLOCK, GQ), BF16)],
        compiler_params=pltpu.CompilerParams(vmem_limit_bytes=VMEM_LIMIT),
        name="nsa_prompt",
    )(q, ck, cvt, sk, svt, wk, wvt, gates)


def _nsa_sample_cmp_kernel(pt_ref, *refs, past, n_steps, pps):
    page_refs = refs[:pps]
    (kcn_ref, praw_ref, w1_ref, w2_ref, kg_ref, q_ref, oc_ref, imp_ref, wv_s, p0_s, p1_s) = refs[pps:]
    del pt_ref
    g = pl.program_id(1)
    chunks_per_page = PAGE_SIZE // CMP_STRIDE
    nchunk = past // CMP_STRIDE
    nsel = past // SEL_BLOCK + 1
    t_now = past

    @pl.when(g == 0)
    def _():
        wv_s[...] = _pool_weight_rows(praw_ref)

    wv = wv_s[...]
    for k in range(pps):
        p0, p1 = _pool_rows(page_refs[k][...], wv)
        rows = pl.ds(pl.multiple_of((g * pps + k) * (chunks_per_page * 8), chunks_per_page * 8),
                     chunks_per_page * 8)
        p0_s[rows, :] = p0
        p1_s[rows, :] = p1

    @pl.when(g == n_steps - 1)
    def _():
        p1_s[pl.ds(nchunk * 8, KV_ROWS), :] = kcn_ref[...] * wv_s[pl.ds(CHUNK_ROWS, KV_ROWS), :]
        p1_s[pl.ds(nchunk * 8 + KV_ROWS, KV_ROWS), :] = jnp.zeros((KV_ROWS, HEAD_DIM), F32)
        ck, cv = _compress_mlp(p0_s, p1_s, nchunk, w1_ref, w2_ref, kg_ref)
        q8 = q_ref[...]
        row = lax.broadcasted_iota(jnp.int32, (N_HEADS, 1), 0)
        first = row < GROUP
        n_row = lax.broadcasted_iota(jnp.int32, (1, nchunk), 1)
        mask = jnp.broadcast_to(n_row * CMP_STRIDE + (CMP_BLOCK - 1) <= t_now, (N_HEADS, nchunk))
        ckb = ck.astype(BF16)
        cvb = cv.astype(BF16)
        s = jnp.where(first, _dot_nt(q8, ckb[:, 0:HEAD_DIM]), _dot_nt(q8, ckb[:, HEAD_DIM:2 * HEAD_DIM]))
        p, den = _masked_softmax_parts(s, mask)
        p = p / den
        pb = p.astype(BF16)
        oc_ref[...] = jnp.where(first, _dot(pb, cvb[:, 0:HEAD_DIM]), _dot(pb, cvb[:, HEAD_DIM:2 * HEAD_DIM]))
        nn = lax.broadcasted_iota(jnp.int32, (nchunk, 256), 0)
        jj = lax.broadcasted_iota(jnp.int32, (nchunk, 256), 1)
        ov = ((nn * CMP_STRIDE <= jj * SEL_BLOCK + SEL_BLOCK - 1)
              & (nn * CMP_STRIDE + CMP_BLOCK - 1 >= jj * SEL_BLOCK) & (jj < nsel)).astype(F32)
        imp8 = jnp.dot(p, ov, precision=lax.Precision.HIGHEST, preferred_element_type=F32)
        imp_ref[0:1, :] = jnp.sum(jnp.where(first, imp8, 0.0), axis=0, keepdims=True)
        imp_ref[1:2, :] = jnp.sum(jnp.where(first, 0.0, imp8), axis=0, keepdims=True)


def _nsa_sample_cmp(page_table, cache_cmp, kc_new, pwt, w1, w2, kg, q8):
    db, n_pages = page_table.shape
    past = n_pages * PAGE_SIZE
    pps = min(PAGES_PER_STEP, n_pages)
    n_steps = n_pages // pps
    pages = cache_cmp.reshape(cache_cmp.shape[0], PAGE_SIZE * KV_ROWS, HEAD_DIM)

    def page_spec(k):
        return pl.BlockSpec((None, PAGE_SIZE * KV_ROWS, HEAD_DIM),
                            lambda b, g, pt: (pt[b * n_pages + g * pps + k], 0, 0))

    def const_spec(shape):
        return pl.BlockSpec(shape, lambda b, g, pt: (0,) * len(shape))

    nchunk = past // CMP_STRIDE
    grid_spec = pltpu.PrefetchScalarGridSpec(
        num_scalar_prefetch=1,
        grid=(db, n_steps),
        in_specs=[page_spec(k) for k in range(pps)] + [
            pl.BlockSpec((None, KV_ROWS, HEAD_DIM), lambda b, g, pt: (b, 0, 0)),
            const_spec((2 * CHUNK_ROWS, HEAD_DIM)), const_spec((2, HEAD_DIM, HEAD_DIM)),
            const_spec((2, HEAD_DIM, HEAD_DIM)), const_spec((3, HEAD_DIM)),
            pl.BlockSpec((None, N_HEADS, HEAD_DIM), lambda b, g, pt: (b, 0, 0))],
        out_specs=[pl.BlockSpec((None, N_HEADS, HEAD_DIM), lambda b, g, pt: (b, 0, 0)),
                   pl.BlockSpec((None, 2, 256), lambda b, g, pt: (b, 0, 0))],
        scratch_shapes=[pltpu.VMEM((2 * CHUNK_ROWS, HEAD_DIM), F32),
                        pltpu.VMEM(((nchunk + 1) * 8, HEAD_DIM), F32),
                        pltpu.VMEM(((nchunk + 1) * 8, HEAD_DIM), F32)])
    return pl.pallas_call(
        functools.partial(_nsa_sample_cmp_kernel, past=past, n_steps=n_steps, pps=pps),
        grid_spec=grid_spec,
        out_shape=[jax.ShapeDtypeStruct((db, N_HEADS, HEAD_DIM), F32),
                   jax.ShapeDtypeStruct((db, 2, 256), F32)],
        compiler_params=pltpu.CompilerParams(vmem_limit_bytes=VMEM_LIMIT),
        name="nsa_sample_cmp",
    )(page_table.reshape(-1), *([pages] * pps), kc_new, pwt, w1, w2, kg, q8)


def _select_kernel(imp_ref, idx_ref, *, t_now, nsel):
    imp = imp_ref[...]
    rows, width = imp.shape
    j = lax.broadcasted_iota(jnp.int32, (rows, width), 1)
    jf = j.astype(F32)
    cur = t_now // SEL_BLOCK
    valid = (j <= cur) & (j < nsel)
    forced = (j == 0) | (valid & (j > cur - N_LOCAL_BLOCKS))
    score = jnp.where(valid, imp + FORCE * forced.astype(F32), NEG)
    col = lax.broadcasted_iota(jnp.int32, (rows, 128), 1)
    out = jnp.full((rows, 128), -1, jnp.int32)
    for it in range(min(TOP_N, nsel)):
        m = jnp.max(score, axis=-1, keepdims=True)
        idx = jnp.min(jnp.where(score == m, jf, 1e9), axis=-1, keepdims=True)
        out = jnp.where(col == it, jnp.where(m > 0.5 * NEG, idx.astype(jnp.int32), -1), out)
        score = jnp.where(jf == idx, -3e38, score)
    idx_ref[...] = out


def _select(imp2d, t_now, nsel):
    rows = imp2d.shape[0]
    return pl.pallas_call(
        functools.partial(_select_kernel, t_now=t_now, nsel=nsel),
        out_shape=jax.ShapeDtypeStruct((rows, 128), jnp.int32),
        name="nsa_sample_select",
    )(imp2d)


def _nsa_sample_sel_kernel(idx_ref, pt_ref, *refs, n_past_blk, n_top, nseq):
    per_seq = N_KV_HEADS * n_top
    q_ref, ksn_ref, os_ref = refs[nseq * per_seq:]
    del pt_ref
    for j in range(nseq):
        _sel_one_sequence(idx_ref, pl.program_id(0) * nseq + j, refs[j * per_seq:(j + 1) * per_seq],
                          q_ref.at[j], ksn_ref.at[j], os_ref.at[j], n_past_blk, n_top)


def _sel_one_sequence(idx_ref, b, blk_refs, q_ref, ksn_ref, os_ref, n_past_blk, n_top):
    q8 = q_ref[...]
    q8f = q8.astype(F32)
    ksn = ksn_ref[...]
    blk_rows = SEL_BLOCK * KV_ROWS
    nrows = n_top * blk_rows
    lane = lax.broadcasted_iota(jnp.int32, (1, nrows), 1)
    row = lax.broadcasted_iota(jnp.int32, (N_HEADS, 1), 0)
    first = row < GROUP
    per_head = []
    for h in range(N_KV_HEADS):
        rows_b = jnp.concatenate([blk_refs[h * n_top + n][...] for n in range(n_top)], axis=0).astype(BF16)
        k_new = ksn[h:h + 1, :]
        v_new = ksn[N_KV_HEADS + h:N_KV_HEADS + h + 1, :]
        slot_ok = jnp.zeros((1, nrows), F32)
        new_ok = jnp.zeros((1, 1), F32)
        for n in range(n_top):
            ix = idx_ref[(b * N_KV_HEADS + h) * n_top + n]
            past_ok = jnp.where((ix >= 0) & (ix < n_past_blk), 1.0, 0.0)
            slot_ok = jnp.where((lane >= n * blk_rows) & (lane < (n + 1) * blk_rows), past_ok, slot_ok)
            new_ok = jnp.maximum(new_ok, jnp.where(ix >= n_past_blk, 1.0, 0.0))
        mask = jnp.broadcast_to((slot_ok > 0.5) & ((lane & (KV_ROWS - 1)) == h), (N_HEADS, nrows))
        new_mask = jnp.broadcast_to(new_ok > 0.5, (N_HEADS, 1))
        s = jnp.where(mask, _dot_nt(q8, rows_b), NEG)
        s_new = jnp.where(new_mask, jnp.sum(q8f * k_new, axis=-1, keepdims=True), NEG)
        mx = jnp.maximum(jnp.max(s, axis=-1, keepdims=True), s_new)
        p = jnp.exp(s - mx) * mask.astype(F32)
        p_new = jnp.exp(s_new - mx) * new_mask.astype(F32)
        den = jnp.maximum(jnp.sum(p, axis=-1, keepdims=True) + p_new, 1e-30)
        p_on_v = pltpu.roll(p, N_KV_HEADS, axis=1).astype(BF16)
        per_head.append((_dot(p_on_v, rows_b) + p_new * v_new) / den)
    os_ref[...] = jnp.where(first, per_head[0], per_head[1])


def _nsa_sample_sel(sel_idx, page_table, cache_sel, q8, ks_new):
    db, n_pages = page_table.shape
    n_past_blk = n_pages * PAGE_SIZE // SEL_BLOCK
    n_top = sel_idx.shape[-1]
    halves = PAGE_SIZE // SEL_BLOCK
    blocks = cache_sel.reshape(cache_sel.shape[0] * halves, SEL_BLOCK * KV_ROWS, HEAD_DIM)
    blk = jnp.clip(sel_idx.reshape(db, N_KV_HEADS * n_top), 0, n_past_blk - 1)
    phys = jnp.take_along_axis(page_table, blk // halves, axis=1) * halves + blk % halves

    nseq = SEL_SEQS_PER_STEP if db % SEL_SEQS_PER_STEP == 0 else 1

    def blk_spec(j, h, n):
        return pl.BlockSpec((None, SEL_BLOCK * KV_ROWS, HEAD_DIM),
                            lambda b, idx, pb: (pb[((b * nseq + j) * N_KV_HEADS + h) * n_top + n], 0, 0))

    grid_spec = pltpu.PrefetchScalarGridSpec(
        num_scalar_prefetch=2,
        grid=(db // nseq,),
        in_specs=[blk_spec(j, h, n) for j in range(nseq) for h in range(N_KV_HEADS) for n in range(n_top)] + [
            pl.BlockSpec((nseq, N_HEADS, HEAD_DIM), lambda b, idx, pt: (b, 0, 0)),
            pl.BlockSpec((nseq, KV_ROWS, HEAD_DIM), lambda b, idx, pt: (b, 0, 0))],
        out_specs=pl.BlockSpec((nseq, N_HEADS, HEAD_DIM), lambda b, idx, pt: (b, 0, 0)))
    return pl.pallas_call(
        functools.partial(_nsa_sample_sel_kernel, n_past_blk=n_past_blk, n_top=n_top, nseq=nseq),
        grid_spec=grid_spec,
        out_shape=jax.ShapeDtypeStruct((db, N_HEADS, HEAD_DIM), F32),
        compiler_params=pltpu.CompilerParams(vmem_limit_bytes=VMEM_LIMIT),
        name="nsa_sample_sel",
    )(sel_idx.reshape(-1), phys.reshape(-1), *([blocks] * (nseq * N_KV_HEADS * n_top)), q8, ks_new)


def _nsa_sample_win_kernel(win_ref, kwn_ref, q_ref, wout_ref, ow_ref, *, past):
    for j in range(win_ref.shape[0]):
        _win_one_sequence(win_ref.at[j], kwn_ref.at[j], q_ref.at[j], wout_ref.at[j], ow_ref.at[j], past)


def _win_one_sequence(win_ref, kwn_ref, q_ref, wout_ref, ow_ref, past):
    nrows = win_ref.shape[0]
    wb = nrows // KV_ROWS
    kwn = kwn_ref[...]
    x = win_ref[...]
    wout_ref[...] = pltpu.roll(x, nrows - KV_ROWS, axis=0)
    wout_ref[pl.ds(nrows - KV_ROWS, KV_ROWS), :] = kwn
    q8 = q_ref[...]
    q8f = q8.astype(F32)
    t_now = past
    lane = lax.broadcasted_iota(jnp.int32, (N_HEADS, nrows), 1)
    row = lax.broadcasted_iota(jnp.int32, (N_HEADS, nrows), 0)
    pos = past - wb + _shr(lane, KV_ROWS.bit_length() - 1)
    d = t_now - pos
    mask = ((d >= 0) & (d < WINDOW) & (pos >= 0)
            & ((lane & (KV_ROWS - 1)) == _shr(row, GROUP.bit_length() - 1)))
    first = lax.broadcasted_iota(jnp.int32, (N_HEADS, 1), 0) < GROUP
    k_new = jnp.where(first, kwn[0:1, :], kwn[1:2, :])
    v_new = jnp.where(first, kwn[N_KV_HEADS:N_KV_HEADS + 1, :], kwn[N_KV_HEADS + 1:N_KV_HEADS + 2, :])
    xb = x.astype(BF16)
    s = jnp.where(mask, _dot_nt(q8, xb), NEG)
    s_new = jnp.sum(q8f * k_new, axis=-1, keepdims=True)
    mx = jnp.maximum(jnp.max(s, axis=-1, keepdims=True), s_new)
    p = jnp.exp(s - mx) * mask.astype(F32)
    p_new = jnp.exp(s_new - mx)
    den = jnp.maximum(jnp.sum(p, axis=-1, keepdims=True) + p_new, 1e-30)
    p_on_v = pltpu.roll(p, N_KV_HEADS, axis=1).astype(BF16)
    ow_ref[...] = (_dot(p_on_v, xb) + p_new * v_new) / den


def _nsa_sample_win(win_rows, kw_new, q8, past):
    db, nrows, _ = win_rows.shape
    nseq = WIN_SEQS_PER_STEP if db % WIN_SEQS_PER_STEP == 0 else 1
    return pl.pallas_call(
        functools.partial(_nsa_sample_win_kernel, past=past),
        grid=(db // nseq,),
        in_specs=[pl.BlockSpec((nseq, nrows, HEAD_DIM), lambda b: (b, 0, 0)),
                  pl.BlockSpec((nseq, KV_ROWS, HEAD_DIM), lambda b: (b, 0, 0)),
                  pl.BlockSpec((nseq, N_HEADS, HEAD_DIM), lambda b: (b, 0, 0))],
        out_specs=[pl.BlockSpec((nseq, nrows, HEAD_DIM), lambda b: (b, 0, 0)),
                   pl.BlockSpec((nseq, N_HEADS, HEAD_DIM), lambda b: (b, 0, 0))],
        out_shape=[jax.ShapeDtypeStruct((db, nrows, HEAD_DIM), F32),
                   jax.ShapeDtypeStruct((db, N_HEADS, HEAD_DIM), F32)],
        name="nsa_sample_win",
    )(win_rows, kw_new, q8)


def _gate_mix_kernel(oc_ref, os_ref, ow_ref, g_ref, o_ref):
    g = g_ref[...]
    c = lax.broadcasted_iota(jnp.int32, (128, D_MODEL), 0)
    head = _shr(lax.broadcasted_iota(jnp.int32, (128, D_MODEL), 1), HEAD_SHIFT)
    acc = jnp.zeros(o_ref.shape, F32)
    for r, ref in enumerate((oc_ref, os_ref, ow_ref)):
        expand = (c == r * N_HEADS + head).astype(F32)
        acc = acc + ref[...] * jnp.dot(g, expand, precision=lax.Precision.HIGHEST, preferred_element_type=F32)
    o_ref[...] = acc


def _merge_kernel(x_ref, a_ref, ra_ref, b_ref, on_ref, gate_ref, w_ref, y_ref):
    u = a_ref[...] * ra_ref[...] + b_ref[...] * on_ref[...]
    y_ref[...] = x_ref[...] + gate_ref[0] * _dot(u.astype(BF16), w_ref[...])


def _merge(x2d, a, ra, b, o_nsa, gate, w_out, tm, rows_per_mod):
    m = x2d.shape[0]
    tiles_per_mod = rows_per_mod // tm
    row_spec = pl.BlockSpec((tm, D_MODEL), lambda i: (i, 0))
    return pl.pallas_call(
        _merge_kernel,
        grid=(m // tm,),
        in_specs=[row_spec] * 5 + [
            pl.BlockSpec((1, gate.shape[1], D_MODEL), lambda i: (i // tiles_per_mod, 0, 0)),
            pl.BlockSpec((D_MODEL, D_MODEL), lambda i: (0, 0))],
        out_specs=row_spec,
        out_shape=jax.ShapeDtypeStruct((m, D_MODEL), F32),
        compiler_params=pltpu.CompilerParams(vmem_limit_bytes=VMEM_LIMIT),
        name="merge_out_proj",
    )(x2d, a, ra, b, o_nsa, gate, w_out)


def _split_w_in(w_in):
    n_bg = 3 * N_HEADS
    lo = w_in[:, :C_ZN].astype(BF16)
    hi = w_in[:, C_ZN + n_bg:].astype(BF16)
    bg = jnp.pad(w_in[:, C_ZN:C_ZN + n_bg], ((0, 0), (0, C_END - C_BG - n_bg))).astype(BF16)
    return lo, hi, bg


def kernel(x_prompt, x_sample, c_prompt, c_sample, state_conv, state_rglru, cache_cmp_kv, cache_sel_kv,
           state_win_kv, page_table, norm_g, w_ada, b_ada, w_in, conv_w, conv_b, rg_wa, rg_ba, rg_wx, rg_bx,
           rg_lambda, q_norm_g, k_norm_g, cmp_pool_w, cmp_w1, cmp_w2, w_out):
    depth = norm_g.shape[0]
    assert depth == 1 and x_sample.shape[1] == 1
    bsz, seq, _ = x_prompt.shape
    db = x_sample.shape[0]
    n_pages = page_table.shape[1]
    past = n_pages * PAGE_SIZE
    layer = 0

    w_cat = _split_w_in(w_in[layer])
    w_out_b = w_out[layer].astype(BF16)
    wa_b = rg_wa[layer].astype(BF16)
    wx_b = rg_wx[layer].astype(BF16)
    w1_b = cmp_w1[layer].astype(BF16)
    w2_b = cmp_w2[layer].astype(BF16)
    row = lambda v: v.reshape(1, -1)
    rg_args = (conv_w[layer], row(conv_b[layer]), wa_b, row(rg_ba[layer]), wx_b, row(rg_bx[layer]),
               row(rg_lambda[layer]))
    praw = jnp.broadcast_to(
        cmp_pool_w[layer].reshape(2, 2, CMP_STRIDE).transpose(1, 2, 0)[:, :, :, None, None],
        (2, CMP_STRIDE, 2, N_KV_HEADS, HEAD_DIM)).reshape(2 * CHUNK_ROWS, HEAD_DIM)
    kg = k_norm_g[layer]

    mod = _modulation(jnp.concatenate([c_prompt, c_sample], axis=0), w_ada[layer], b_ada[layer])
    shift, scale, gate = mod[:, :D_MODEL], mod[:, D_MODEL:2 * D_MODEL], mod[:, 2 * D_MODEL:]

    xp2 = x_prompt.reshape(bsz * seq, D_MODEL)
    pm = lambda v: v[:bsz].reshape(bsz, 1, D_MODEL)
    (xr_p, a_p, b_p, q_p, kc_p, ks_p, kw_p, skb_p, svt_p, wkb_p, wvt_p, g_p) = _project(
        xp2, pm(shift), pm(scale), norm_g[layer], w_cat, q_norm_g[layer], kg, tm=256, rows_per_mod=seq)
    ra_p, h_p = _rglru_prompt(xr_p, bsz, seq, *rg_args)
    ck_p, cvt_p = _compress_prompt(kc_p, praw, w1_b, w2_b, kg, bsz, seq)
    on_p = _nsa_prompt(q_p, ck_p, cvt_p, skb_p, svt_p, wkb_p, wvt_p, g_p, bsz, seq)
    y_p = _merge(xp2, a_p, ra_p, b_p, on_p, pm(gate), w_out_b, tm=256, rows_per_mod=seq)

    xs2 = x_sample.reshape(db, D_MODEL)
    sm = lambda v: v[bsz:].reshape(1, db, D_MODEL)
    (xr_s, a_s, b_s, q_s, kc_s, ks_s, kw_s, _, _, _, _, g_s) = _project(
        xs2, sm(shift), sm(scale), norm_g[layer], w_cat, q_norm_g[layer], kg, tm=db, rows_per_mod=db)
    h_s = _rglru_step(xr_s, state_conv[layer].reshape(db, (CONV_W - 1) * D_MODEL), state_rglru[layer], *rg_args)
    q8 = q_s.reshape(db, N_HEADS, HEAD_DIM)
    new_rows = lambda v: v.reshape(db, KV_ROWS, HEAD_DIM)
    n_phys = cache_cmp_kv.shape[1]
    oc_s, imp = _nsa_sample_cmp(page_table, cache_cmp_kv.reshape(depth * n_phys, PAGE_SIZE * KV_ROWS, HEAD_DIM),
                                new_rows(kc_s), praw, w1_b, w2_b, kg, q8)
    nsel = past // SEL_BLOCK + 1
    sel_idx = _select(imp.reshape(db * N_KV_HEADS, 256), past, nsel)[:, :min(TOP_N, nsel)]
    os_s = _nsa_sample_sel(sel_idx, page_table,
                           cache_sel_kv.reshape(depth * n_phys, PAGE_SIZE * KV_ROWS, HEAD_DIM), q8, new_rows(ks_s))
    wb = state_win_kv.shape[2]
    win_s, ow_s = _nsa_sample_win(state_win_kv.reshape(depth * db, wb * KV_ROWS, HEAD_DIM), new_rows(kw_s), q8, past)
    on_s = pl.pallas_call(
        _gate_mix_kernel, out_shape=jax.ShapeDtypeStruct((db, D_MODEL), F32), name="nsa_sample_mix",
    )(oc_s.reshape(db, D_MODEL), os_s.reshape(db, D_MODEL), ow_s.reshape(db, D_MODEL), g_s)
    y_s = _merge(xs2, a_s, h_s, b_s, on_s, sm(gate), w_out_b, tm=db, rows_per_mod=db)

    kv_shape = (2, N_KV_HEADS, HEAD_DIM)
    xr_p3 = xr_p.reshape(bsz, seq, D_MODEL)
    conv_prompt = xr_p3[:, seq - (CONV_W - 1):][None]
    conv_sample = jnp.concatenate([state_conv[layer][:, 1:], xr_s[:, None, :]], axis=1)[None]
    win_len = min(WINDOW, seq)
    return (y_p.reshape(bsz, seq, D_MODEL), y_s.reshape(db, 1, D_MODEL),
            conv_prompt, conv_sample,
            h_p.reshape(1, bsz, D_MODEL), h_s.reshape(1, db, D_MODEL),
            kc_p.reshape(1, bsz, seq, *kv_shape), kc_s.reshape(1, db, 1, *kv_shape),
            ks_p.reshape(1, bsz, seq, *kv_shape), ks_s.reshape(1, db, 1, *kv_shape),
            kw_p.reshape(bsz, seq * KV_ROWS, HEAD_DIM)[:, (seq - win_len) * KV_ROWS:].reshape(
                1, bsz, win_len, *kv_shape),
            win_s.reshape(1, db, wb, *kv_shape))
```

```python
import functools

import jax
import jax.numpy as jnp
from jax import lax
from jax.experimental import pallas as pl
from jax.experimental.pallas import tpu as pltpu

F32 = jnp.float32
BF16 = jnp.bfloat16

D_MODEL = 1024
RG_BLOCKS = 8
RG_BW = D_MODEL // RG_BLOCKS
RG_C = 8.0
CONV_W = 4
N_HEADS = 8
N_KV_HEADS = 2
GROUP = N_HEADS // N_KV_HEADS
HEAD_DIM = D_MODEL // N_HEADS
KV_W = 2 * N_KV_HEADS * HEAD_DIM
KV_ROWS = 2 * N_KV_HEADS
CMP_BLOCK = 32
CMP_STRIDE = 16
SEL_BLOCK = 64
TOP_N = 16
N_LOCAL_BLOCKS = 2
WINDOW = 512
Q_BLOCK = 128
PAGE_SIZE = 128
EPS = 1e-6
LOG2E = 1.4426950408889634
NEG = -1e30
FORCE = 1e4

C_XR, C_ZR, C_Q, C_KC, C_KS, C_KW, C_ZN, C_GA, C_GB, C_BG, C_END = (
    0, 1024, 2048, 3072, 3584, 4096, 4608, 5632, 6656, 7680, 7808)
VMEM_LIMIT = 56 * 1024 * 1024
PAGES_PER_STEP = 64
WIN_SEQS_PER_STEP = 4
SEL_SEQS_PER_STEP = 2

SEL_SHIFT = SEL_BLOCK.bit_length() - 1
HEAD_SHIFT = HEAD_DIM.bit_length() - 1


def _shr(x, k):
    return lax.shift_right_arithmetic(x, jnp.int32(k))


def _sigmoid(x):
    return jax.nn.sigmoid(x)


def _dot(a, b):
    return jnp.dot(a, b, preferred_element_type=F32)


def _dot_nt(a, b):
    return lax.dot_general(a, b, (((1,), (1,)), ((), ())), preferred_element_type=F32)


def _rms_rows(x, g):
    return x * lax.rsqrt(jnp.mean(x * x, axis=-1, keepdims=True) + EPS) * g


def _mod_kernel(c_ref, w_ref, b_ref, o_ref):
    c = c_ref[...]
    s = c * _sigmoid(c)
    o_ref[...] = _dot(s.astype(BF16), w_ref[...].astype(BF16)) + b_ref[...]


def _modulation(c_all, w_ada, b_ada):
    n = c_all.shape[0]
    return pl.pallas_call(
        _mod_kernel,
        grid=(3,),
        in_specs=[pl.BlockSpec((n, D_MODEL), lambda j: (0, 0)),
                  pl.BlockSpec((D_MODEL, D_MODEL), lambda j: (0, j)),
                  pl.BlockSpec((1, D_MODEL), lambda j: (0, j))],
        out_specs=pl.BlockSpec((n, D_MODEL), lambda j: (0, j)),
        out_shape=jax.ShapeDtypeStruct((n, 3 * D_MODEL), F32),
        name="adaln_mod",
    )(c_all, w_ada, b_ada.reshape(1, 3 * D_MODEL))


def _proj_kernel(x_ref, shift_ref, scale_ref, ng_ref, w_lo_ref, w_hi_ref, w_bg_ref, qg_ref, kg_ref,
                 xr_ref, a_ref, b_ref, q_ref, kc_ref, ks_ref, kw_ref, skb_ref, svt_ref, wkb_ref, wvt_ref, g_ref,
                 *, q_scale):
    x = x_ref[...]
    h = _rms_rows(x, ng_ref[...])
    h = h * (1.0 + scale_ref[0]) + shift_ref[0]
    hb = h.astype(BF16)

    def mm(lo, hi):
        if hi <= C_ZN:
            return _dot(hb, w_lo_ref[:, lo:hi])
        if hi <= C_BG:
            return _dot(hb, w_hi_ref[:, lo - C_ZN:hi - C_ZN])
        return _dot(hb, w_bg_ref[...])

    xr_ref[...] = mm(C_XR, C_ZR)
    zr = mm(C_ZR, C_Q)
    ga = mm(C_GA, C_GB)
    a_ref[...] = (_sigmoid(ga) * (zr * _sigmoid(zr))).astype(a_ref.dtype)
    zn = mm(C_ZN, C_GA)
    gb = mm(C_GB, C_BG)
    b_ref[...] = (_sigmoid(gb) * (zn * _sigmoid(zn))).astype(b_ref.dtype)
    q = mm(C_Q, C_KC)
    for hd in range(N_HEADS):
        sl = slice(hd * HEAD_DIM, (hd + 1) * HEAD_DIM)
        q_ref[:, sl] = (_rms_rows(q[:, sl], qg_ref[...]) * q_scale).astype(BF16)
    tm = x.shape[0]

    def store_kv(o_ref, eh, val):
        o_ref[pl.ds(eh, tm, stride=KV_ROWS), :] = val

    kc = mm(C_KC, C_KS)
    for eh in range(KV_ROWS):
        store_kv(kc_ref, eh, kc[:, eh * HEAD_DIM:(eh + 1) * HEAD_DIM])
    for lo, hi, o_ref, kb_ref, vt_ref, gi in ((C_KS, C_KW, ks_ref, skb_ref, svt_ref, 1),
                                              (C_KW, C_ZN, kw_ref, wkb_ref, wvt_ref, 2)):
        kv = mm(lo, hi)
        for hd in range(N_KV_HEADS):
            sl = slice(hd * HEAD_DIM, (hd + 1) * HEAD_DIM)
            kn = _rms_rows(kv[:, sl], kg_ref[gi:gi + 1, :])
            store_kv(o_ref, hd, kn)
            kb_ref[:, sl] = kn.astype(BF16)
            v = kv[:, (N_KV_HEADS + hd) * HEAD_DIM:(N_KV_HEADS + hd + 1) * HEAD_DIM]
            store_kv(o_ref, N_KV_HEADS + hd, v)
            v_t = v.T.astype(BF16)
            for j in range(tm // Q_BLOCK):
                vt_ref[j, sl, :] = v_t[:, j * Q_BLOCK:(j + 1) * Q_BLOCK]
    g_ref[...] = _sigmoid(mm(C_BG, C_END))


def _project(x2d, shift, scale, norm_g, w_parts, q_norm_g, k_norm_g, tm, rows_per_mod, q_scale):
    m = x2d.shape[0]
    tiles_per_mod = rows_per_mod // tm
    mod_rows = shift.shape[1]
    mod_spec = pl.BlockSpec((1, mod_rows, D_MODEL), lambda i: (i // tiles_per_mod, 0, 0))

    def row_spec(width, mult=1):
        return pl.BlockSpec((tm * mult, width), lambda i: (i, 0))

    def const_spec(shape):
        return pl.BlockSpec(shape, lambda i: (0,) * len(shape))

    kv_heads_w = N_KV_HEADS * HEAD_DIM
    k_bf16 = (pl.BlockSpec((tm, kv_heads_w), lambda i: (i, 0)), jax.ShapeDtypeStruct((m, kv_heads_w), BF16))
    vt_bf16 = (pl.BlockSpec((tm // Q_BLOCK, kv_heads_w, Q_BLOCK), lambda i: (i, 0, 0)),
               jax.ShapeDtypeStruct((m // Q_BLOCK, kv_heads_w, Q_BLOCK), BF16))

    def rows(width, dtype, mult=1):
        return row_spec(width, mult), jax.ShapeDtypeStruct((m * mult, width), dtype)

    outs = (rows(D_MODEL, F32), rows(D_MODEL, BF16), rows(D_MODEL, BF16), rows(D_MODEL, BF16),
            rows(HEAD_DIM, F32, KV_ROWS), rows(HEAD_DIM, F32, KV_ROWS), rows(HEAD_DIM, F32, KV_ROWS),
            k_bf16, vt_bf16, k_bf16, vt_bf16, rows(128, F32))
    return pl.pallas_call(
        functools.partial(_proj_kernel, q_scale=q_scale),
        grid=(m // tm,),
        in_specs=[row_spec(D_MODEL), mod_spec, mod_spec, const_spec((1, D_MODEL)),
                  pl.BlockSpec((D_MODEL, C_ZN), lambda i: (0, 0), pipeline_mode=pl.Buffered(1)),
                  pl.BlockSpec((D_MODEL, C_BG - C_ZN), lambda i: (0, 0), pipeline_mode=pl.Buffered(1)),
                  const_spec((D_MODEL, C_END - C_BG)),
                  const_spec((1, HEAD_DIM)), const_spec((3, HEAD_DIM))],
        out_specs=[spec for spec, _ in outs],
        out_shape=[shape for _, shape in outs],
        compiler_params=pltpu.CompilerParams(vmem_limit_bytes=VMEM_LIMIT),
        name="in_proj",
    )(x2d, shift, scale, norm_g.reshape(1, D_MODEL), *w_parts, q_norm_g.reshape(1, HEAD_DIM), k_norm_g)


def _softplus(z):
    return jnp.maximum(z, 0.0) + jnp.log1p(jnp.exp(-jnp.abs(z)))


def _rglru_coeffs(xc, wa_ref, ba_ref, wx_ref, bx_ref, lam_ref, a_out, b_out):
    xcb = xc.astype(BF16)
    sp = _softplus(-lam_ref[...])
    for k in range(RG_BLOCKS):
        sl = slice(k * RG_BW, (k + 1) * RG_BW)
        r = _sigmoid(_dot(xcb[:, sl], wa_ref[k]) + ba_ref[:, sl])
        i = _sigmoid(_dot(xcb[:, sl], wx_ref[k]) + bx_ref[:, sl])
        log_a = -RG_C * r * sp[:, sl]
        a = jnp.exp(log_a)
        a_out[:, sl] = a
        b_out[:, sl] = jnp.sqrt(-jnp.tanh(log_a) * (a * a + 1.0)) * i * xc[:, sl]


def _rglru_prompt_kernel(xr_ref, cw_ref, cb_ref, wa_ref, ba_ref, wx_ref, bx_ref, lam_ref,
                         ra_ref, hl_ref, ext_s, a_s, b_s, h_s):
    t_len = xr_ref.shape[0]

    @pl.when(pl.program_id(1) == 0)
    def _():
        ext_s[0:8, :] = jnp.zeros((8, D_MODEL), F32)
        h_s[...] = jnp.zeros((1, D_MODEL), F32)

    x = xr_ref[...]
    ext_s[8:8 + t_len, :] = x
    xc = ext_s[pl.ds(5, t_len), :] * cw_ref[0:1, :] + cb_ref[...]
    xc = xc + ext_s[pl.ds(6, t_len), :] * cw_ref[1:2, :]
    xc = xc + ext_s[pl.ds(7, t_len), :] * cw_ref[2:3, :]
    xc = xc + x * cw_ref[3:4, :]
    ext_s[0:8, :] = x[t_len - 8:t_len, :]
    _rglru_coeffs(xc, wa_ref, ba_ref, wx_ref, bx_ref, lam_ref, a_s, b_s)

    row = lax.broadcasted_iota(jnp.int32, (8, D_MODEL), 0)

    def scan8(r0, h):
        a = a_s[pl.ds(r0, 8), :]
        b = b_s[pl.ds(r0, 8), :]
        for s in (1, 2, 4):
            keep = row >= s
            b = jnp.where(keep, a * pltpu.roll(b, s, axis=0) + b, b)
            a = jnp.where(keep, a * pltpu.roll(a, s, axis=0), a)
        return a * h + b

    def tile(i, h):
        r0 = pl.multiple_of(i * 16, 16)
        h0 = scan8(r0, h)
        h1 = scan8(r0 + 8, h0[7:8, :])
        ra_ref[pl.ds(r0, 16), :] = jnp.concatenate([h0, h1], axis=0).astype(ra_ref.dtype)
        return h1[7:8, :]

    h = lax.fori_loop(0, t_len // 16, tile, h_s[...])
    h_s[...] = h
    hl_ref[...] = h


def _rglru_prompt(xr, bsz, seq, cw, cb, wa, ba, wx, bx, lam, t_chunk=512):
    nchunk = seq // t_chunk

    def const_spec(shape):
        return pl.BlockSpec(shape, lambda b, c: (0,) * len(shape))

    return pl.pallas_call(
        _rglru_prompt_kernel,
        grid=(bsz, nchunk),
        in_specs=[pl.BlockSpec((t_chunk, D_MODEL), lambda b, c: (b * nchunk + c, 0)),
                  const_spec((CONV_W, D_MODEL)), const_spec((1, D_MODEL)),
                  const_spec((RG_BLOCKS, RG_BW, RG_BW)), const_spec((1, D_MODEL)),
                  const_spec((RG_BLOCKS, RG_BW, RG_BW)), const_spec((1, D_MODEL)),
                  const_spec((1, D_MODEL))],
        out_specs=[pl.BlockSpec((t_chunk, D_MODEL), lambda b, c: (b * nchunk + c, 0)),
                   pl.BlockSpec((None, 1, D_MODEL), lambda b, c: (b, 0, 0))],
        out_shape=[jax.ShapeDtypeStruct((bsz * seq, D_MODEL), BF16),
                   jax.ShapeDtypeStruct((bsz, 1, D_MODEL), F32)],
        scratch_shapes=[pltpu.VMEM((t_chunk + 8, D_MODEL), F32), pltpu.VMEM((t_chunk, D_MODEL), F32),
                        pltpu.VMEM((t_chunk, D_MODEL), F32), pltpu.VMEM((1, D_MODEL), F32)],
        compiler_params=pltpu.CompilerParams(vmem_limit_bytes=VMEM_LIMIT),
        name="rglru_prompt",
    )(xr, cw, cb, wa, ba, wx, bx, lam)


def _rglru_step_kernel(xr_ref, cbuf_ref, h0_ref, cw_ref, cb_ref, wa_ref, ba_ref, wx_ref, bx_ref, lam_ref,
                       h_ref, a_s, b_s):
    xc = cbuf_ref[:, 0:D_MODEL] * cw_ref[0:1, :] + cb_ref[...]
    xc = xc + cbuf_ref[:, D_MODEL:2 * D_MODEL] * cw_ref[1:2, :]
    xc = xc + cbuf_ref[:, 2 * D_MODEL:3 * D_MODEL] * cw_ref[2:3, :]
    xc = xc + xr_ref[...] * cw_ref[3:4, :]
    _rglru_coeffs(xc, wa_ref, ba_ref, wx_ref, bx_ref, lam_ref, a_s, b_s)
    h_ref[...] = a_s[...] * h0_ref[...] + b_s[...]


def _rglru_step(xr, cbuf, h0, cw, cb, wa, ba, wx, bx, lam):
    n = xr.shape[0]
    return pl.pallas_call(
        _rglru_step_kernel,
        out_shape=jax.ShapeDtypeStruct((n, D_MODEL), F32),
        scratch_shapes=[pltpu.VMEM((n, D_MODEL), F32), pltpu.VMEM((n, D_MODEL), F32)],
        name="rglru_step",
    )(xr, cbuf, h0, cw, cb, wa, ba, wx, bx, lam)


CHUNK_ROWS = CMP_STRIDE * KV_ROWS


def _pool_weight_rows(praw_ref):
    praw = praw_ref[...]
    row = lax.broadcasted_iota(jnp.int32, praw.shape, 0)
    is_k = (row & (KV_ROWS - 1)) < N_KV_HEADS
    m_k = jnp.max(jnp.where(is_k, praw, -3e38), axis=0, keepdims=True)
    m_v = jnp.max(jnp.where(is_k, -3e38, praw), axis=0, keepdims=True)
    ex = jnp.exp(praw - jnp.where(is_k, m_k, m_v))
    s_k = jnp.sum(jnp.where(is_k, ex, 0.0), axis=0, keepdims=True) * (1.0 / N_KV_HEADS)
    s_v = jnp.sum(jnp.where(is_k, 0.0, ex), axis=0, keepdims=True) * (1.0 / N_KV_HEADS)
    return ex / jnp.where(is_k, s_k, s_v)


def _pool_rows(x, wv):
    n = x.shape[0] // CHUNK_ROWS
    x4 = x.reshape(n, CHUNK_ROWS // 8, 8, HEAD_DIM)
    w4 = wv.reshape(2, CHUNK_ROWS // 8, 8, HEAD_DIM)
    p0 = x4[:, 0] * w4[0, 0]
    p1 = x4[:, 0] * w4[1, 0]
    for v in range(1, CHUNK_ROWS // 8):
        p0 = p0 + x4[:, v] * w4[0, v]
        p1 = p1 + x4[:, v] * w4[1, v]
    return p0.reshape(n * 8, HEAD_DIM), p1.reshape(n * 8, HEAD_DIM)


def _pooled_head(p0_s, p1_s, eh, nc):
    def col(ref, start):
        return ref[pl.ds(start, nc, stride=8), :]
    return (col(p0_s, eh) + col(p0_s, eh + KV_ROWS)) + (col(p1_s, 8 + eh) + col(p1_s, 8 + eh + KV_ROWS))


def _compress_mlp(p0_s, p1_s, nc, w1_ref, w2_ref, kg_ref):
    outs = []
    for e in range(2):
        per_head = []
        for hd in range(N_KV_HEADS):
            p = _pooled_head(p0_s, p1_s, e * N_KV_HEADS + hd, nc)
            hid = _dot(p.astype(BF16), w1_ref[e])
            hid = hid * _sigmoid(hid)
            comp = p + _dot(hid.astype(BF16), w2_ref[e])
            if e == 0:
                comp = _rms_rows(comp, kg_ref[0:1, :])
            per_head.append(comp)
        outs.append(jnp.concatenate(per_head, axis=1))
    return outs[0], outs[1]


def _masked_softmax_parts(s, mask):
    sm = jnp.where(mask, s, NEG)
    p = jnp.exp(sm - jnp.max(sm, axis=-1, keepdims=True)) * mask.astype(F32)
    return p, jnp.maximum(jnp.sum(p, axis=-1, keepdims=True), 1e-30)


POOL_SLAB = 8


def _compress_prompt_kernel(x_ref, praw_ref, w1_ref, w2_ref, kg_ref, ck_ref, cvt_ref, p0_s, p1_s):
    nch = x_ref.shape[0] // CHUNK_ROWS
    wv = _pool_weight_rows(praw_ref)

    def slab(i, _):
        x = x_ref[pl.ds(pl.multiple_of(i * (POOL_SLAB * CHUNK_ROWS), POOL_SLAB * CHUNK_ROWS),
                        POOL_SLAB * CHUNK_ROWS), :]
        p0, p1 = _pool_rows(x, wv)
        rows = pl.ds(pl.multiple_of(i * (POOL_SLAB * 8), POOL_SLAB * 8), POOL_SLAB * 8)
        p0_s[rows, :] = p0
        p1_s[rows, :] = p1
        return 0

    lax.fori_loop(0, nch // POOL_SLAB, slab, 0)
    p1_s[pl.ds(nch * 8, 8), :] = jnp.zeros((8, HEAD_DIM), F32)
    ck, cv = _compress_mlp(p0_s, p1_s, nch, w1_ref, w2_ref, kg_ref)
    ck_ref[...] = ck
    cvt_ref[...] = cv.T


def _compress_prompt(kc_rows, praw, w1, w2, kg, bsz, seq):
    nch = seq // CMP_STRIDE

    def const_spec(shape):
        return pl.BlockSpec(shape, lambda b: (0,) * len(shape))

    return pl.pallas_call(
        _compress_prompt_kernel,
        grid=(bsz,),
        in_specs=[pl.BlockSpec((seq * KV_ROWS, HEAD_DIM), lambda b: (b, 0)),
                  const_spec((2 * CHUNK_ROWS, HEAD_DIM)), const_spec((2, HEAD_DIM, HEAD_DIM)),
                  const_spec((2, HEAD_DIM, HEAD_DIM)), const_spec((3, HEAD_DIM))],
        out_specs=[pl.BlockSpec((None, nch, 2 * HEAD_DIM), lambda b: (b, 0, 0)),
                   pl.BlockSpec((None, 2 * HEAD_DIM, nch), lambda b: (b, 0, 0))],
        out_shape=[jax.ShapeDtypeStruct((bsz, nch, 2 * HEAD_DIM), F32),
                   jax.ShapeDtypeStruct((bsz, 2 * HEAD_DIM, nch), F32)],
        scratch_shapes=[pltpu.VMEM(((nch + 1) * 8, HEAD_DIM), F32)] * 2,
        name="compress_prompt",
    )(kc_rows, praw, w1, w2, kg)


GQ = GROUP * Q_BLOCK
WIN_BLOCKS = WINDOW // Q_BLOCK + 1


def _bias_groups(s_t, bias):
    return jnp.concatenate([s_t[:, g * Q_BLOCK:(g + 1) * Q_BLOCK] + bias for g in range(GROUP)], axis=1)


def _fold8(x, op):
    parts = [x[i * 8:(i + 1) * 8, :] for i in range(x.shape[0] // 8)]
    while len(parts) > 1:
        parts = [op(parts[i], parts[i + 1]) for i in range(0, len(parts), 2)]
    return parts[0]


def _finish_t(o_t, m, l8):
    den = jnp.maximum(jnp.sum(l8, axis=0, keepdims=True), 1e-30)
    return jnp.where(m > 0.5 * NEG, o_t / den, 0.0)


def _nsa_prompt_kernel(q_ref, ck_ref, cvt_ref, sk_ref, svt_ref, wk_ref, wvt_ref, g_ref, o_ref,
                       bias_s, ps_s, pw_s):
    qi = pl.program_id(1)
    nq = Q_BLOCK
    ncp = ck_ref.shape[0]
    nkb = svt_ref.shape[0]
    nsel = nkb * (Q_BLOCK // SEL_BLOCK)
    n_top = min(TOP_N, nsel)
    start = qi * nq
    t_row = start + lax.broadcasted_iota(jnp.int32, (1, nq), 1)
    gates_t = g_ref[...].T

    n_sub = lax.broadcasted_iota(jnp.int32, (ncp, nq), 0)
    cmp_mask = (n_sub * CMP_STRIDE + (CMP_BLOCK - 1) <= t_row) & (n_sub < ncp - 1)
    jj = lax.broadcasted_iota(jnp.int32, (nsel, ncp), 0)
    nn = lax.broadcasted_iota(jnp.int32, (nsel, ncp), 1)
    ov_t = ((nn * CMP_STRIDE <= jj * SEL_BLOCK + SEL_BLOCK - 1)
            & (nn * CMP_STRIDE + CMP_BLOCK - 1 >= jj * SEL_BLOCK)
            & (nn < ncp - 1)).astype(F32)
    j_col = lax.broadcasted_iota(jnp.int32, (nsel, nq), 0)
    cur = _shr(t_row, SEL_SHIFT)
    valid_t = j_col <= cur
    forced_t = (j_col == 0) | (valid_t & (j_col > cur - N_LOCAL_BLOCKS))
    force_add = FORCE * forced_t.astype(F32)
    key_sub = lax.broadcasted_iota(jnp.int32, (Q_BLOCK, nq), 0)
    q_lane = lax.broadcasted_iota(jnp.int32, (Q_BLOCK, nq), 1)
    win_bias = {0: jnp.where(q_lane < key_sub, 0.0, NEG), WIN_BLOCKS - 1: jnp.where(q_lane >= key_sub, 0.0, NEG)}

    for h in range(N_KV_HEADS):
        qh = jnp.concatenate(
            [q_ref[:, (h * GROUP + g) * HEAD_DIM:(h * GROUP + g + 1) * HEAD_DIM] for g in range(GROUP)], axis=0)
        hs = slice(h * HEAD_DIM, (h + 1) * HEAD_DIM)

        s_c = _dot_nt(ck_ref[:, hs].astype(BF16), qh)
        p_groups = []
        for g in range(GROUP):
            sm = jnp.where(cmp_mask, s_c[:, g * nq:(g + 1) * nq], NEG)
            p = jnp.exp2(sm - jnp.max(sm, axis=0, keepdims=True)) * cmp_mask.astype(F32)
            p_groups.append(p / jnp.maximum(jnp.sum(p, axis=0, keepdims=True), 1e-30))
        o_c = _dot(cvt_ref[hs, :].astype(BF16), jnp.concatenate(p_groups, axis=1).astype(BF16))
        psum = p_groups[0] + p_groups[1] + p_groups[2] + p_groups[3]
        imp_t = jnp.dot(ov_t, psum, precision=lax.Precision.HIGHEST, preferred_element_type=F32)
        score_t = jnp.where(valid_t, imp_t + force_add, NEG)
        rank = jnp.zeros((nsel, nq), F32)
        for k in range(nsel):
            rk = score_t[k:k + 1, :]
            tie = jnp.where(j_col > k, 1.0, 0.0)
            rank = rank + jnp.where(rk > score_t, 1.0, jnp.where(rk == score_t, tie, 0.0))
        sel_f = jnp.where(rank < n_top, jnp.where(score_t > 0.5 * NEG, 1.0, 0.0), 0.0)

        def sel_branch(nblk):
            def run():
                for kb in range(nblk):
                    lo = sel_f[2 * kb:2 * kb + 1, :]
                    hi = sel_f[2 * kb + 1:2 * kb + 2, :]
                    picked = jnp.where(key_sub < SEL_BLOCK, lo, hi) > 0.5
                    bias_s[kb] = jnp.where(picked & (kb * Q_BLOCK + key_sub <= t_row), 0.0, NEG)

                def scores(kb):
                    return _bias_groups(_dot_nt(sk_ref[kb * Q_BLOCK:(kb + 1) * Q_BLOCK, hs], qh), bias_s[kb])

                m8 = jnp.full((8, GQ), NEG, F32)
                for kb in range(nblk):
                    m8 = jnp.maximum(m8, _fold8(scores(kb), jnp.maximum))
                m_s = jnp.max(m8, axis=0, keepdims=True)
                l_s = jnp.zeros((8, GQ), F32)
                for kb in range(nblk):
                    p = jnp.exp2(scores(kb) - m_s)
                    ps_s[kb * Q_BLOCK:(kb + 1) * Q_BLOCK, :] = p.astype(BF16)
                    l_s = l_s + _fold8(p, jnp.add)
                v_sel = jnp.concatenate([svt_ref[kb, hs, :] for kb in range(nblk)], axis=1)
                return _finish_t(_dot(v_sel, ps_s[0:nblk * Q_BLOCK, :]), m_s, l_s)
            return run

        quarter = nkb // 4
        o_s = lax.switch(qi // quarter, [sel_branch((v + 1) * quarter) for v in range(4)])

        def win_scores(r):
            kb = qi - (WIN_BLOCKS - 1) + r
            kbc = jnp.maximum(kb, 0)
            off = pl.multiple_of(kbc * Q_BLOCK, Q_BLOCK)
            s_t = _dot_nt(wk_ref[pl.ds(off, Q_BLOCK), hs], qh) + jnp.where(kb < 0, NEG, 0.0)
            if r in win_bias:
                s_t = _bias_groups(s_t, win_bias[r])
            return s_t, kbc

        m8 = jnp.full((8, GQ), NEG, F32)
        for r in range(WIN_BLOCKS):
            m8 = jnp.maximum(m8, _fold8(win_scores(r)[0], jnp.maximum))
        m_w = jnp.max(m8, axis=0, keepdims=True)
        l_w = jnp.zeros((8, GQ), F32)
        v_blocks = []
        for r in range(WIN_BLOCKS):
            s_t, kbc = win_scores(r)
            p = jnp.exp2(s_t - m_w)
            pw_s[r * Q_BLOCK:(r + 1) * Q_BLOCK, :] = p.astype(BF16)
            l_w = l_w + _fold8(p, jnp.add)
            v_blocks.append(wvt_ref[kbc, hs, :])
        o_w = _finish_t(_dot(jnp.concatenate(v_blocks, axis=1), pw_s[...]), m_w, l_w)

        for g in range(GROUP):
            head = h * GROUP + g
            cs = slice(g * nq, (g + 1) * nq)
            o = (o_c[:, cs] * gates_t[head:head + 1, :]
                 + o_s[:, cs] * gates_t[N_HEADS + head:N_HEADS + head + 1, :]
                 + o_w[:, cs] * gates_t[2 * N_HEADS + head:2 * N_HEADS + head + 1, :])
            o_ref[:, head * HEAD_DIM:(head + 1) * HEAD_DIM] = o.T.astype(o_ref.dtype)


def _nsa_prompt(q, ck, cvt, sk, svt, wk, wvt, gates, bsz, seq):
    nqb = seq // Q_BLOCK
    ncp = ck.shape[1]
    kv_heads_w = N_KV_HEADS * HEAD_DIM
    k_spec = pl.BlockSpec((seq, kv_heads_w), lambda b, i: (b, 0))
    vt_spec = pl.BlockSpec((nqb, kv_heads_w, Q_BLOCK), lambda b, i: (b, 0, 0))
    return pl.pallas_call(
        _nsa_prompt_kernel,
        grid=(bsz, nqb),
        in_specs=[pl.BlockSpec((Q_BLOCK, D_MODEL), lambda b, i: (b * nqb + i, 0)),
                  pl.BlockSpec((None, ncp, kv_heads_w), lambda b, i: (b, 0, 0)),
                  pl.BlockSpec((None, kv_heads_w, ncp), lambda b, i: (b, 0, 0)),
                  k_spec, vt_spec, k_spec, vt_spec,
                  pl.BlockSpec((Q_BLOCK, 128), lambda b, i: (b * nqb + i, 0))],
        out_specs=pl.BlockSpec((Q_BLOCK, D_MODEL), lambda b, i: (b * nqb + i, 0)),
        out_shape=jax.ShapeDtypeStruct((bsz * seq, D_MODEL), BF16),
        scratch_shapes=[pltpu.VMEM((nqb, Q_BLOCK, Q_BLOCK), F32),
                        pltpu.VMEM((seq, GQ), BF16),
                        pltpu.VMEM((WIN_BLOCKS * Q_BLOCK, GQ), BF16)],
        compiler_params=pltpu.CompilerParams(vmem_limit_bytes=VMEM_LIMIT),
        name="nsa_prompt",
    )(q, ck, cvt, sk, svt, wk, wvt, gates)


def _nsa_sample_cmp_kernel(pt_ref, *refs, past, n_steps, pps):
    page_refs = refs[:pps]
    (kcn_ref, praw_ref, w1_ref, w2_ref, kg_ref, q_ref, oc_ref, imp_ref, wv_s, p0_s, p1_s) = refs[pps:]
    del pt_ref
    g = pl.program_id(1)
    chunks_per_page = PAGE_SIZE // CMP_STRIDE
    nchunk = past // CMP_STRIDE
    nsel = past // SEL_BLOCK + 1
    t_now = past

    @pl.when(g == 0)
    def _():
        wv_s[...] = _pool_weight_rows(praw_ref)

    wv = wv_s[...]
    for k in range(pps):
        p0, p1 = _pool_rows(page_refs[k][...], wv)
        rows = pl.ds(pl.multiple_of((g * pps + k) * (chunks_per_page * 8), chunks_per_page * 8),
                     chunks_per_page * 8)
        p0_s[rows, :] = p0
        p1_s[rows, :] = p1

    @pl.when(g == n_steps - 1)
    def _():
        p1_s[pl.ds(nchunk * 8, KV_ROWS), :] = kcn_ref[...] * wv_s[pl.ds(CHUNK_ROWS, KV_ROWS), :]
        p1_s[pl.ds(nchunk * 8 + KV_ROWS, KV_ROWS), :] = jnp.zeros((KV_ROWS, HEAD_DIM), F32)
        ck, cv = _compress_mlp(p0_s, p1_s, nchunk, w1_ref, w2_ref, kg_ref)
        q8 = q_ref[...]
        row = lax.broadcasted_iota(jnp.int32, (N_HEADS, 1), 0)
        first = row < GROUP
        n_row = lax.broadcasted_iota(jnp.int32, (1, nchunk), 1)
        mask = jnp.broadcast_to(n_row * CMP_STRIDE + (CMP_BLOCK - 1) <= t_now, (N_HEADS, nchunk))
        ckb = ck.astype(BF16)
        cvb = cv.astype(BF16)
        s = jnp.where(first, _dot_nt(q8, ckb[:, 0:HEAD_DIM]), _dot_nt(q8, ckb[:, HEAD_DIM:2 * HEAD_DIM]))
        p, den = _masked_softmax_parts(s, mask)
        p = p / den
        pb = p.astype(BF16)
        oc_ref[...] = jnp.where(first, _dot(pb, cvb[:, 0:HEAD_DIM]), _dot(pb, cvb[:, HEAD_DIM:2 * HEAD_DIM]))
        nn = lax.broadcasted_iota(jnp.int32, (nchunk, 256), 0)
        jj = lax.broadcasted_iota(jnp.int32, (nchunk, 256), 1)
        ov = ((nn * CMP_STRIDE <= jj * SEL_BLOCK + SEL_BLOCK - 1)
              & (nn * CMP_STRIDE + CMP_BLOCK - 1 >= jj * SEL_BLOCK) & (jj < nsel)).astype(F32)
        imp8 = jnp.dot(p, ov, precision=lax.Precision.HIGHEST, preferred_element_type=F32)
        imp_ref[0:1, :] = jnp.sum(jnp.where(first, imp8, 0.0), axis=0, keepdims=True)
        imp_ref[1:2, :] = jnp.sum(jnp.where(first, 0.0, imp8), axis=0, keepdims=True)


def _nsa_sample_cmp(page_table, cache_cmp, kc_new, pwt, w1, w2, kg, q8):
    db, n_pages = page_table.shape
    past = n_pages * PAGE_SIZE
    pps = min(PAGES_PER_STEP, n_pages)
    n_steps = n_pages // pps
    pages = cache_cmp.reshape(cache_cmp.shape[0], PAGE_SIZE * KV_ROWS, HEAD_DIM)

    def page_spec(k):
        return pl.BlockSpec((None, PAGE_SIZE * KV_ROWS, HEAD_DIM),
                            lambda b, g, pt: (pt[b * n_pages + g * pps + k], 0, 0))

    def const_spec(shape):
        return pl.BlockSpec(shape, lambda b, g, pt: (0,) * len(shape))

    nchunk = past // CMP_STRIDE
    grid_spec = pltpu.PrefetchScalarGridSpec(
        num_scalar_prefetch=1,
        grid=(db, n_steps),
        in_specs=[page_spec(k) for k in range(pps)] + [
            pl.BlockSpec((None, KV_ROWS, HEAD_DIM), lambda b, g, pt: (b, 0, 0)),
            const_spec((2 * CHUNK_ROWS, HEAD_DIM)), const_spec((2, HEAD_DIM, HEAD_DIM)),
            const_spec((2, HEAD_DIM, HEAD_DIM)), const_spec((3, HEAD_DIM)),
            pl.BlockSpec((None, N_HEADS, HEAD_DIM), lambda b, g, pt: (b, 0, 0))],
        out_specs=[pl.BlockSpec((None, N_HEADS, HEAD_DIM), lambda b, g, pt: (b, 0, 0)),
                   pl.BlockSpec((None, 2, 256), lambda b, g, pt: (b, 0, 0))],
        scratch_shapes=[pltpu.VMEM((2 * CHUNK_ROWS, HEAD_DIM), F32),
                        pltpu.VMEM(((nchunk + 1) * 8, HEAD_DIM), F32),
                        pltpu.VMEM(((nchunk + 1) * 8, HEAD_DIM), F32)])
    return pl.pallas_call(
        functools.partial(_nsa_sample_cmp_kernel, past=past, n_steps=n_steps, pps=pps),
        grid_spec=grid_spec,
        out_shape=[jax.ShapeDtypeStruct((db, N_HEADS, HEAD_DIM), F32),
                   jax.ShapeDtypeStruct((db, 2, 256), F32)],
        compiler_params=pltpu.CompilerParams(vmem_limit_bytes=VMEM_LIMIT),
        name="nsa_sample_cmp",
    )(page_table.reshape(-1), *([pages] * pps), kc_new, pwt, w1, w2, kg, q8)


def _select_kernel(imp_ref, idx_ref, *, t_now, nsel):
    imp = imp_ref[...]
    rows, width = imp.shape
    j = lax.broadcasted_iota(jnp.int32, (rows, width), 1)
    jf = j.astype(F32)
    cur = t_now // SEL_BLOCK
    valid = (j <= cur) & (j < nsel)
    forced = (j == 0) | (valid & (j > cur - N_LOCAL_BLOCKS))
    score = jnp.where(valid, imp + FORCE * forced.astype(F32), NEG)
    col = lax.broadcasted_iota(jnp.int32, (rows, 128), 1)
    out = jnp.full((rows, 128), -1, jnp.int32)
    for it in range(min(TOP_N, nsel)):
        m = jnp.max(score, axis=-1, keepdims=True)
        idx = jnp.min(jnp.where(score == m, jf, 1e9), axis=-1, keepdims=True)
        out = jnp.where(col == it, jnp.where(m > 0.5 * NEG, idx.astype(jnp.int32), -1), out)
        score = jnp.where(jf == idx, -3e38, score)
    idx_ref[...] = out


def _select(imp2d, t_now, nsel):
    rows = imp2d.shape[0]
    return pl.pallas_call(
        functools.partial(_select_kernel, t_now=t_now, nsel=nsel),
        out_shape=jax.ShapeDtypeStruct((rows, 128), jnp.int32),
        name="nsa_sample_select",
    )(imp2d)


def _nsa_sample_sel_kernel(idx_ref, pt_ref, *refs, n_past_blk, n_top, nseq):
    per_seq = N_KV_HEADS * n_top
    q_ref, ksn_ref, os_ref = refs[nseq * per_seq:]
    del pt_ref
    for j in range(nseq):
        _sel_one_sequence(idx_ref, pl.program_id(0) * nseq + j, refs[j * per_seq:(j + 1) * per_seq],
                          q_ref.at[j], ksn_ref.at[j], os_ref.at[j], n_past_blk, n_top)


def _sel_one_sequence(idx_ref, b, blk_refs, q_ref, ksn_ref, os_ref, n_past_blk, n_top):
    q8 = q_ref[...]
    q8f = q8.astype(F32)
    ksn = ksn_ref[...]
    blk_rows = SEL_BLOCK * KV_ROWS
    nrows = n_top * blk_rows
    lane = lax.broadcasted_iota(jnp.int32, (1, nrows), 1)
    row = lax.broadcasted_iota(jnp.int32, (N_HEADS, 1), 0)
    first = row < GROUP
    per_head = []
    for h in range(N_KV_HEADS):
        rows_b = jnp.concatenate([blk_refs[h * n_top + n][...] for n in range(n_top)], axis=0).astype(BF16)
        k_new = ksn[h:h + 1, :]
        v_new = ksn[N_KV_HEADS + h:N_KV_HEADS + h + 1, :]
        slot_ok = jnp.zeros((1, nrows), F32)
        new_ok = jnp.zeros((1, 1), F32)
        for n in range(n_top):
            ix = idx_ref[(b * N_KV_HEADS + h) * n_top + n]
            past_ok = jnp.where((ix >= 0) & (ix < n_past_blk), 1.0, 0.0)
            slot_ok = jnp.where((lane >= n * blk_rows) & (lane < (n + 1) * blk_rows), past_ok, slot_ok)
            new_ok = jnp.maximum(new_ok, jnp.where(ix >= n_past_blk, 1.0, 0.0))
        mask = jnp.broadcast_to((slot_ok > 0.5) & ((lane & (KV_ROWS - 1)) == h), (N_HEADS, nrows))
        new_mask = jnp.broadcast_to(new_ok > 0.5, (N_HEADS, 1))
        s = jnp.where(mask, _dot_nt(q8, rows_b), NEG)
        s_new = jnp.where(new_mask, jnp.sum(q8f * k_new, axis=-1, keepdims=True), NEG)
        mx = jnp.maximum(jnp.max(s, axis=-1, keepdims=True), s_new)
        p = jnp.exp(s - mx) * mask.astype(F32)
        p_new = jnp.exp(s_new - mx) * new_mask.astype(F32)
        den = jnp.maximum(jnp.sum(p, axis=-1, keepdims=True) + p_new, 1e-30)
        p_on_v = pltpu.roll(p, N_KV_HEADS, axis=1).astype(BF16)
        per_head.append((_dot(p_on_v, rows_b) + p_new * v_new) / den)
    os_ref[...] = jnp.where(first, per_head[0], per_head[1])


def _nsa_sample_sel(sel_idx, page_table, cache_sel, q8, ks_new):
    db, n_pages = page_table.shape
    n_past_blk = n_pages * PAGE_SIZE // SEL_BLOCK
    n_top = sel_idx.shape[-1]
    halves = PAGE_SIZE // SEL_BLOCK
    blocks = cache_sel.reshape(cache_sel.shape[0] * halves, SEL_BLOCK * KV_ROWS, HEAD_DIM)
    blk = jnp.clip(sel_idx.reshape(db, N_KV_HEADS * n_top), 0, n_past_blk - 1)
    phys = jnp.take_along_axis(page_table, blk // halves, axis=1) * halves + blk % halves

    nseq = SEL_SEQS_PER_STEP if db % SEL_SEQS_PER_STEP == 0 else 1

    def blk_spec(j, h, n):
        return pl.BlockSpec((None, SEL_BLOCK * KV_ROWS, HEAD_DIM),
                            lambda b, idx, pb: (pb[((b * nseq + j) * N_KV_HEADS + h) * n_top + n], 0, 0))

    grid_spec = pltpu.PrefetchScalarGridSpec(
        num_scalar_prefetch=2,
        grid=(db // nseq,),
        in_specs=[blk_spec(j, h, n) for j in range(nseq) for h in range(N_KV_HEADS) for n in range(n_top)] + [
            pl.BlockSpec((nseq, N_HEADS, HEAD_DIM), lambda b, idx, pt: (b, 0, 0)),
            pl.BlockSpec((nseq, KV_ROWS, HEAD_DIM), lambda b, idx, pt: (b, 0, 0))],
        out_specs=pl.BlockSpec((nseq, N_HEADS, HEAD_DIM), lambda b, idx, pt: (b, 0, 0)))
    return pl.pallas_call(
        functools.partial(_nsa_sample_sel_kernel, n_past_blk=n_past_blk, n_top=n_top, nseq=nseq),
        grid_spec=grid_spec,
        out_shape=jax.ShapeDtypeStruct((db, N_HEADS, HEAD_DIM), F32),
        compiler_params=pltpu.CompilerParams(vmem_limit_bytes=VMEM_LIMIT),
        name="nsa_sample_sel",
    )(sel_idx.reshape(-1), phys.reshape(-1), *([blocks] * (nseq * N_KV_HEADS * n_top)), q8, ks_new)


def _nsa_sample_win_kernel(win_ref, kwn_ref, q_ref, wout_ref, ow_ref, *, past):
    for j in range(win_ref.shape[0]):
        _win_one_sequence(win_ref.at[j], kwn_ref.at[j], q_ref.at[j], wout_ref.at[j], ow_ref.at[j], past)


def _win_one_sequence(win_ref, kwn_ref, q_ref, wout_ref, ow_ref, past):
    nrows = win_ref.shape[0]
    wb = nrows // KV_ROWS
    kwn = kwn_ref[...]
    x = win_ref[...]
    wout_ref[...] = pltpu.roll(x, nrows - KV_ROWS, axis=0)
    wout_ref[pl.ds(nrows - KV_ROWS, KV_ROWS), :] = kwn
    q8 = q_ref[...]
    q8f = q8.astype(F32)
    t_now = past
    lane = lax.broadcasted_iota(jnp.int32, (N_HEADS, nrows), 1)
    row = lax.broadcasted_iota(jnp.int32, (N_HEADS, nrows), 0)
    pos = past - wb + _shr(lane, KV_ROWS.bit_length() - 1)
    d = t_now - pos
    mask = ((d >= 0) & (d < WINDOW) & (pos >= 0)
            & ((lane & (KV_ROWS - 1)) == _shr(row, GROUP.bit_length() - 1)))
    first = lax.broadcasted_iota(jnp.int32, (N_HEADS, 1), 0) < GROUP
    k_new = jnp.where(first, kwn[0:1, :], kwn[1:2, :])
    v_new = jnp.where(first, kwn[N_KV_HEADS:N_KV_HEADS + 1, :], kwn[N_KV_HEADS + 1:N_KV_HEADS + 2, :])
    xb = x.astype(BF16)
    s = jnp.where(mask, _dot_nt(q8, xb), NEG)
    s_new = jnp.sum(q8f * k_new, axis=-1, keepdims=True)
    mx = jnp.maximum(jnp.max(s, axis=-1, keepdims=True), s_new)
    p = jnp.exp(s - mx) * mask.astype(F32)
    p_new = jnp.exp(s_new - mx)
    den = jnp.maximum(jnp.sum(p, axis=-1, keepdims=True) + p_new, 1e-30)
    p_on_v = pltpu.roll(p, N_KV_HEADS, axis=1).astype(BF16)
    ow_ref[...] = (_dot(p_on_v, xb) + p_new * v_new) / den


def _nsa_sample_win(win_rows, kw_new, q8, past):
    db, nrows, _ = win_rows.shape
    nseq = WIN_SEQS_PER_STEP if db % WIN_SEQS_PER_STEP == 0 else 1
    return pl.pallas_call(
        functools.partial(_nsa_sample_win_kernel, past=past),
        grid=(db // nseq,),
        in_specs=[pl.BlockSpec((nseq, nrows, HEAD_DIM), lambda b: (b, 0, 0)),
                  pl.BlockSpec((nseq, KV_ROWS, HEAD_DIM), lambda b: (b, 0, 0)),
                  pl.BlockSpec((nseq, N_HEADS, HEAD_DIM), lambda b: (b, 0, 0))],
        out_specs=[pl.BlockSpec((nseq, nrows, HEAD_DIM), lambda b: (b, 0, 0)),
                   pl.BlockSpec((nseq, N_HEADS, HEAD_DIM), lambda b: (b, 0, 0))],
        out_shape=[jax.ShapeDtypeStruct((db, nrows, HEAD_DIM), F32),
                   jax.ShapeDtypeStruct((db, N_HEADS, HEAD_DIM), F32)],
        name="nsa_sample_win",
    )(win_rows, kw_new, q8)


def _gate_mix_kernel(oc_ref, os_ref, ow_ref, g_ref, o_ref):
    g = g_ref[...]
    c = lax.broadcasted_iota(jnp.int32, (128, D_MODEL), 0)
    head = _shr(lax.broadcasted_iota(jnp.int32, (128, D_MODEL), 1), HEAD_SHIFT)
    acc = jnp.zeros(o_ref.shape, F32)
    for r, ref in enumerate((oc_ref, os_ref, ow_ref)):
        expand = (c == r * N_HEADS + head).astype(F32)
        acc = acc + ref[...] * jnp.dot(g, expand, precision=lax.Precision.HIGHEST, preferred_element_type=F32)
    o_ref[...] = acc


def _merge_kernel(x_ref, a_ref, ra_ref, b_ref, on_ref, gate_ref, w_ref, y_ref):
    f32 = lambda ref: ref[...].astype(F32)
    u = f32(a_ref) * f32(ra_ref) + f32(b_ref) * f32(on_ref)
    y_ref[...] = x_ref[...] + gate_ref[0] * _dot(u.astype(BF16), w_ref[...])


def _merge(x2d, a, ra, b, o_nsa, gate, w_out, tm, rows_per_mod):
    m = x2d.shape[0]
    tiles_per_mod = rows_per_mod // tm
    row_spec = pl.BlockSpec((tm, D_MODEL), lambda i: (i, 0))
    return pl.pallas_call(
        _merge_kernel,
        grid=(m // tm,),
        in_specs=[row_spec] * 5 + [
            pl.BlockSpec((1, gate.shape[1], D_MODEL), lambda i: (i // tiles_per_mod, 0, 0)),
            pl.BlockSpec((D_MODEL, D_MODEL), lambda i: (0, 0))],
        out_specs=row_spec,
        out_shape=jax.ShapeDtypeStruct((m, D_MODEL), F32),
        compiler_params=pltpu.CompilerParams(vmem_limit_bytes=VMEM_LIMIT),
        name="merge_out_proj",
    )(x2d, a, ra, b, o_nsa, gate, w_out)


def _split_w_in(w_in):
    n_bg = 3 * N_HEADS
    lo = w_in[:, :C_ZN].astype(BF16)
    hi = w_in[:, C_ZN + n_bg:].astype(BF16)
    bg = jnp.pad(w_in[:, C_ZN:C_ZN + n_bg], ((0, 0), (0, C_END - C_BG - n_bg))).astype(BF16)
    return lo, hi, bg


def kernel(x_prompt, x_sample, c_prompt, c_sample, state_conv, state_rglru, cache_cmp_kv, cache_sel_kv,
           state_win_kv, page_table, norm_g, w_ada, b_ada, w_in, conv_w, conv_b, rg_wa, rg_ba, rg_wx, rg_bx,
           rg_lambda, q_norm_g, k_norm_g, cmp_pool_w, cmp_w1, cmp_w2, w_out):
    depth = norm_g.shape[0]
    assert depth == 1 and x_sample.shape[1] == 1
    bsz, seq, _ = x_prompt.shape
    db = x_sample.shape[0]
    n_pages = page_table.shape[1]
    past = n_pages * PAGE_SIZE
    layer = 0

    w_cat = _split_w_in(w_in[layer])
    w_out_b = w_out[layer].astype(BF16)
    wa_b = rg_wa[layer].astype(BF16)
    wx_b = rg_wx[layer].astype(BF16)
    w1_b = cmp_w1[layer].astype(BF16)
    w2_b = cmp_w2[layer].astype(BF16)
    row = lambda v: v.reshape(1, -1)
    rg_args = (conv_w[layer], row(conv_b[layer]), wa_b, row(rg_ba[layer]), wx_b, row(rg_bx[layer]),
               row(rg_lambda[layer]))
    praw = jnp.broadcast_to(
        cmp_pool_w[layer].reshape(2, 2, CMP_STRIDE).transpose(1, 2, 0)[:, :, :, None, None],
        (2, CMP_STRIDE, 2, N_KV_HEADS, HEAD_DIM)).reshape(2 * CHUNK_ROWS, HEAD_DIM)
    kg = k_norm_g[layer]

    mod = _modulation(jnp.concatenate([c_prompt, c_sample], axis=0), w_ada[layer], b_ada[layer])
    shift, scale, gate = mod[:, :D_MODEL], mod[:, D_MODEL:2 * D_MODEL], mod[:, 2 * D_MODEL:]

    xp2 = x_prompt.reshape(bsz * seq, D_MODEL)
    pm = lambda v: v[:bsz].reshape(bsz, 1, D_MODEL)
    (xr_p, a_p, b_p, q_p, kc_p, ks_p, kw_p, skb_p, svt_p, wkb_p, wvt_p, g_p) = _project(
        xp2, pm(shift), pm(scale), norm_g[layer], w_cat, q_norm_g[layer], kg, tm=256, rows_per_mod=seq,
        q_scale=HEAD_DIM ** -0.5 * LOG2E)
    ra_p, h_p = _rglru_prompt(xr_p, bsz, seq, *rg_args)
    ck_p, cvt_p = _compress_prompt(kc_p, praw, w1_b, w2_b, kg, bsz, seq)
    on_p = _nsa_prompt(q_p, ck_p, cvt_p, skb_p, svt_p, wkb_p, wvt_p, g_p, bsz, seq)
    y_p = _merge(xp2, a_p, ra_p, b_p, on_p, pm(gate), w_out_b, tm=256, rows_per_mod=seq)

    xs2 = x_sample.reshape(db, D_MODEL)
    sm = lambda v: v[bsz:].reshape(1, db, D_MODEL)
    (xr_s, a_s, b_s, q_s, kc_s, ks_s, kw_s, _, _, _, _, g_s) = _project(
        xs2, sm(shift), sm(scale), norm_g[layer], w_cat, q_norm_g[layer], kg, tm=db, rows_per_mod=db,
        q_scale=HEAD_DIM ** -0.5)
    h_s = _rglru_step(xr_s, state_conv[layer].reshape(db, (CONV_W - 1) * D_MODEL), state_rglru[layer], *rg_args)
    q8 = q_s.reshape(db, N_HEADS, HEAD_DIM)
    new_rows = lambda v: v.reshape(db, KV_ROWS, HEAD_DIM)
    n_phys = cache_cmp_kv.shape[1]
    oc_s, imp = _nsa_sample_cmp(page_table, cache_cmp_kv.reshape(depth * n_phys, PAGE_SIZE * KV_ROWS, HEAD_DIM),
                                new_rows(kc_s), praw, w1_b, w2_b, kg, q8)
    nsel = past // SEL_BLOCK + 1
    sel_idx = _select(imp.reshape(db * N_KV_HEADS, 256), past, nsel)[:, :min(TOP_N, nsel)]
    os_s = _nsa_sample_sel(sel_idx, page_table,
                           cache_sel_kv.reshape(depth * n_phys, PAGE_SIZE * KV_ROWS, HEAD_DIM), q8, new_rows(ks_s))
    wb = state_win_kv.shape[2]
    win_s, ow_s = _nsa_sample_win(state_win_kv.reshape(depth * db, wb * KV_ROWS, HEAD_DIM), new_rows(kw_s), q8, past)
    on_s = pl.pallas_call(
        _gate_mix_kernel, out_shape=jax.ShapeDtypeStruct((db, D_MODEL), F32), name="nsa_sample_mix",
    )(oc_s.reshape(db, D_MODEL), os_s.reshape(db, D_MODEL), ow_s.reshape(db, D_MODEL), g_s)
    y_s = _merge(xs2, a_s, h_s, b_s, on_s, sm(gate), w_out_b, tm=db, rows_per_mod=db)

    kv_shape = (2, N_KV_HEADS, HEAD_DIM)
    xr_p3 = xr_p.reshape(bsz, seq, D_MODEL)
    conv_prompt = xr_p3[:, seq - (CONV_W - 1):][None]
    conv_sample = jnp.concatenate([state_conv[layer][:, 1:], xr_s[:, None, :]], axis=1)[None]
    win_len = min(WINDOW, seq)
    return (y_p.reshape(bsz, seq, D_MODEL), y_s.reshape(db, 1, D_MODEL),
            conv_prompt, conv_sample,
            h_p.reshape(1, bsz, D_MODEL), h_s.reshape(1, db, D_MODEL),
            kc_p.reshape(1, bsz, seq, *kv_shape), kc_s.reshape(1, db, 1, *kv_shape),
            ks_p.reshape(1, bsz, seq, *kv_shape), ks_s.reshape(1, db, 1, *kv_shape),
            kw_p.reshape(bsz, seq * KV_ROWS, HEAD_DIM)[:, (seq - win_len) * KV_ROWS:].reshape(
                1, bsz, win_len, *kv_shape),
            win_s.reshape(1, db, wb, *kv_shape))
```

```python
import functools

import jax
import jax.numpy as jnp
from jax import lax
from jax.experimental import pallas as pl
from jax.experimental.pallas import tpu as pltpu

F32 = jnp.float32
BF16 = jnp.bfloat16

D_MODEL = 1024
RG_BLOCKS = 8
RG_BW = D_MODEL // RG_BLOCKS
RG_C = 8.0
CONV_W = 4
N_HEADS = 8
N_KV_HEADS = 2
GROUP = N_HEADS // N_KV_HEADS
HEAD_DIM = D_MODEL // N_HEADS
KV_W = 2 * N_KV_HEADS * HEAD_DIM
KV_ROWS = 2 * N_KV_HEADS
CMP_BLOCK = 32
CMP_STRIDE = 16
SEL_BLOCK = 64
TOP_N = 16
N_LOCAL_BLOCKS = 2
WINDOW = 512
Q_BLOCK = 128
PAGE_SIZE = 128
EPS = 1e-6
LOG2E = 1.4426950408889634
NEG = -1e30
FORCE = 1e4

C_XR, C_ZR, C_Q, C_KC, C_KS, C_KW, C_ZN, C_GA, C_GB, C_BG, C_END = (
    0, 1024, 2048, 3072, 3584, 4096, 4608, 5632, 6656, 7680, 7808)
VMEM_LIMIT = 56 * 1024 * 1024
PAGES_PER_STEP = 64
WIN_SEQS_PER_STEP = 4
SEL_SEQS_PER_STEP = 2

SEL_SHIFT = SEL_BLOCK.bit_length() - 1
HEAD_SHIFT = HEAD_DIM.bit_length() - 1


def _shr(x, k):
    return lax.shift_right_arithmetic(x, jnp.int32(k))


def _sigmoid(x):
    return jax.nn.sigmoid(x)


def _dot(a, b):
    return jnp.dot(a, b, preferred_element_type=F32)


def _dot_nt(a, b):
    return lax.dot_general(a, b, (((1,), (1,)), ((), ())), preferred_element_type=F32)


def _rms_rows(x, g):
    return x * lax.rsqrt(jnp.mean(x * x, axis=-1, keepdims=True) + EPS) * g


def _mod_kernel(c_ref, w_ref, b_ref, o_ref):
    c = c_ref[...]
    s = c * _sigmoid(c)
    o_ref[...] = _dot(s.astype(BF16), w_ref[...].astype(BF16)) + b_ref[...]


def _modulation(c_all, w_ada, b_ada):
    n = c_all.shape[0]
    return pl.pallas_call(
        _mod_kernel,
        grid=(3,),
        in_specs=[pl.BlockSpec((n, D_MODEL), lambda j: (0, 0)),
                  pl.BlockSpec((D_MODEL, D_MODEL), lambda j: (0, j)),
                  pl.BlockSpec((1, D_MODEL), lambda j: (0, j))],
        out_specs=pl.BlockSpec((n, D_MODEL), lambda j: (0, j)),
        out_shape=jax.ShapeDtypeStruct((n, 3 * D_MODEL), F32),
        name="adaln_mod",
    )(c_all, w_ada, b_ada.reshape(1, 3 * D_MODEL))


def _proj_kernel(x_ref, shift_ref, scale_ref, ng_ref, w_lo_ref, w_hi_ref, w_bg_ref, qg_ref, kg_ref,
                 xr_ref, a_ref, b_ref, q_ref, kc_ref, ks_ref, kw_ref, skb_ref, svt_ref, wkb_ref, wvt_ref, g_ref,
                 *, q_scale):
    x = x_ref[...]
    h = _rms_rows(x, ng_ref[...])
    h = h * (1.0 + scale_ref[0]) + shift_ref[0]
    hb = h.astype(BF16)

    def mm(lo, hi):
        if hi <= C_ZN:
            return _dot(hb, w_lo_ref[:, lo:hi])
        if hi <= C_BG:
            return _dot(hb, w_hi_ref[:, lo - C_ZN:hi - C_ZN])
        return _dot(hb, w_bg_ref[...])

    xr_ref[...] = mm(C_XR, C_ZR)
    zr = mm(C_ZR, C_Q)
    ga = mm(C_GA, C_GB)
    a_ref[...] = (_sigmoid(ga) * (zr * _sigmoid(zr))).astype(a_ref.dtype)
    zn = mm(C_ZN, C_GA)
    gb = mm(C_GB, C_BG)
    b_ref[...] = (_sigmoid(gb) * (zn * _sigmoid(zn))).astype(b_ref.dtype)
    q = mm(C_Q, C_KC)
    for hd in range(N_HEADS):
        sl = slice(hd * HEAD_DIM, (hd + 1) * HEAD_DIM)
        q_ref[:, sl] = (_rms_rows(q[:, sl], qg_ref[...]) * q_scale).astype(BF16)
    tm = x.shape[0]

    def store_kv(o_ref, eh, val):
        o_ref[pl.ds(eh, tm, stride=KV_ROWS), :] = val

    kc = mm(C_KC, C_KS)
    for eh in range(KV_ROWS):
        store_kv(kc_ref, eh, kc[:, eh * HEAD_DIM:(eh + 1) * HEAD_DIM])
    for lo, hi, o_ref, kb_ref, vt_ref, gi in ((C_KS, C_KW, ks_ref, skb_ref, svt_ref, 1),
                                              (C_KW, C_ZN, kw_ref, wkb_ref, wvt_ref, 2)):
        kv = mm(lo, hi)
        for hd in range(N_KV_HEADS):
            sl = slice(hd * HEAD_DIM, (hd + 1) * HEAD_DIM)
            kn = _rms_rows(kv[:, sl], kg_ref[gi:gi + 1, :])
            store_kv(o_ref, hd, kn)
            kb_ref[:, sl] = kn.astype(BF16)
            v = kv[:, (N_KV_HEADS + hd) * HEAD_DIM:(N_KV_HEADS + hd + 1) * HEAD_DIM]
            store_kv(o_ref, N_KV_HEADS + hd, v)
            v_t = v.T.astype(BF16)
            for j in range(tm // Q_BLOCK):
                vt_ref[j, sl, :] = v_t[:, j * Q_BLOCK:(j + 1) * Q_BLOCK]
    g_ref[...] = _sigmoid(mm(C_BG, C_END))


def _project(x2d, shift, scale, norm_g, w_parts, q_norm_g, k_norm_g, tm, rows_per_mod, q_scale):
    m = x2d.shape[0]
    tiles_per_mod = rows_per_mod // tm
    mod_rows = shift.shape[1]
    mod_spec = pl.BlockSpec((1, mod_rows, D_MODEL), lambda i: (i // tiles_per_mod, 0, 0))

    def row_spec(width, mult=1):
        return pl.BlockSpec((tm * mult, width), lambda i: (i, 0))

    def const_spec(shape):
        return pl.BlockSpec(shape, lambda i: (0,) * len(shape))

    kv_heads_w = N_KV_HEADS * HEAD_DIM
    k_bf16 = (pl.BlockSpec((tm, kv_heads_w), lambda i: (i, 0)), jax.ShapeDtypeStruct((m, kv_heads_w), BF16))
    vt_bf16 = (pl.BlockSpec((tm // Q_BLOCK, kv_heads_w, Q_BLOCK), lambda i: (i, 0, 0)),
               jax.ShapeDtypeStruct((m // Q_BLOCK, kv_heads_w, Q_BLOCK), BF16))

    def rows(width, dtype, mult=1):
        return row_spec(width, mult), jax.ShapeDtypeStruct((m * mult, width), dtype)

    outs = (rows(D_MODEL, F32), rows(D_MODEL, BF16), rows(D_MODEL, BF16), rows(D_MODEL, BF16),
            rows(HEAD_DIM, F32, KV_ROWS), rows(HEAD_DIM, F32, KV_ROWS), rows(HEAD_DIM, F32, KV_ROWS),
            k_bf16, vt_bf16, k_bf16, vt_bf16, rows(128, F32))
    return pl.pallas_call(
        functools.partial(_proj_kernel, q_scale=q_scale),
        grid=(m // tm,),
        in_specs=[row_spec(D_MODEL), mod_spec, mod_spec, const_spec((1, D_MODEL)),
                  pl.BlockSpec((D_MODEL, C_ZN), lambda i: (0, 0), pipeline_mode=pl.Buffered(1)),
                  pl.BlockSpec((D_MODEL, C_BG - C_ZN), lambda i: (0, 0), pipeline_mode=pl.Buffered(1)),
                  const_spec((D_MODEL, C_END - C_BG)),
                  const_spec((1, HEAD_DIM)), const_spec((3, HEAD_DIM))],
        out_specs=[spec for spec, _ in outs],
        out_shape=[shape for _, shape in outs],
        compiler_params=pltpu.CompilerParams(vmem_limit_bytes=VMEM_LIMIT),
        name="in_proj",
    )(x2d, shift, scale, norm_g.reshape(1, D_MODEL), *w_parts, q_norm_g.reshape(1, HEAD_DIM), k_norm_g)


def _softplus(z):
    return jnp.maximum(z, 0.0) + jnp.log1p(jnp.exp(-jnp.abs(z)))


def _rglru_coeffs(xc, wa_ref, ba_ref, wx_ref, bx_ref, lam_ref, a_out, b_out):
    xcb = xc.astype(BF16)
    sp = _softplus(-lam_ref[...])
    for k in range(RG_BLOCKS):
        sl = slice(k * RG_BW, (k + 1) * RG_BW)
        r = _sigmoid(_dot(xcb[:, sl], wa_ref[k]) + ba_ref[:, sl])
        i = _sigmoid(_dot(xcb[:, sl], wx_ref[k]) + bx_ref[:, sl])
        log_a = -RG_C * r * sp[:, sl]
        a = jnp.exp(log_a)
        a_out[:, sl] = a
        b_out[:, sl] = jnp.sqrt(-jnp.tanh(log_a) * (a * a + 1.0)) * i * xc[:, sl]


def _rglru_prompt_kernel(xr_ref, cw_ref, cb_ref, wa_ref, ba_ref, wx_ref, bx_ref, lam_ref,
                         ra_ref, hl_ref, ext_s, a_s, b_s, h_s):
    t_len = xr_ref.shape[0]

    @pl.when(pl.program_id(1) == 0)
    def _():
        ext_s[0:8, :] = jnp.zeros((8, D_MODEL), F32)
        h_s[...] = jnp.zeros((1, D_MODEL), F32)

    x = xr_ref[...]
    ext_s[8:8 + t_len, :] = x
    xc = ext_s[pl.ds(5, t_len), :] * cw_ref[0:1, :] + cb_ref[...]
    xc = xc + ext_s[pl.ds(6, t_len), :] * cw_ref[1:2, :]
    xc = xc + ext_s[pl.ds(7, t_len), :] * cw_ref[2:3, :]
    xc = xc + x * cw_ref[3:4, :]
    ext_s[0:8, :] = x[t_len - 8:t_len, :]
    _rglru_coeffs(xc, wa_ref, ba_ref, wx_ref, bx_ref, lam_ref, a_s, b_s)

    row = lax.broadcasted_iota(jnp.int32, (8, D_MODEL), 0)

    def scan8(r0, h):
        a = a_s[pl.ds(r0, 8), :]
        b = b_s[pl.ds(r0, 8), :]
        for s in (1, 2, 4):
            keep = row >= s
            b = jnp.where(keep, a * pltpu.roll(b, s, axis=0) + b, b)
            a = jnp.where(keep, a * pltpu.roll(a, s, axis=0), a)
        return a * h + b

    def tile(i, h):
        r0 = pl.multiple_of(i * 16, 16)
        h0 = scan8(r0, h)
        h1 = scan8(r0 + 8, h0[7:8, :])
        ra_ref[pl.ds(r0, 16), :] = jnp.concatenate([h0, h1], axis=0).astype(ra_ref.dtype)
        return h1[7:8, :]

    h = lax.fori_loop(0, t_len // 16, tile, h_s[...])
    h_s[...] = h
    hl_ref[...] = h


def _rglru_prompt(xr, bsz, seq, cw, cb, wa, ba, wx, bx, lam, t_chunk=1024):
    t_chunk = min(t_chunk, seq)
    nchunk = seq // t_chunk

    def const_spec(shape):
        return pl.BlockSpec(shape, lambda b, c: (0,) * len(shape))

    return pl.pallas_call(
        _rglru_prompt_kernel,
        grid=(bsz, nchunk),
        in_specs=[pl.BlockSpec((t_chunk, D_MODEL), lambda b, c: (b * nchunk + c, 0)),
                  const_spec((CONV_W, D_MODEL)), const_spec((1, D_MODEL)),
                  const_spec((RG_BLOCKS, RG_BW, RG_BW)), const_spec((1, D_MODEL)),
                  const_spec((RG_BLOCKS, RG_BW, RG_BW)), const_spec((1, D_MODEL)),
                  const_spec((1, D_MODEL))],
        out_specs=[pl.BlockSpec((t_chunk, D_MODEL), lambda b, c: (b * nchunk + c, 0)),
                   pl.BlockSpec((None, 1, D_MODEL), lambda b, c: (b, 0, 0))],
        out_shape=[jax.ShapeDtypeStruct((bsz * seq, D_MODEL), BF16),
                   jax.ShapeDtypeStruct((bsz, 1, D_MODEL), F32)],
        scratch_shapes=[pltpu.VMEM((t_chunk + 8, D_MODEL), F32), pltpu.VMEM((t_chunk, D_MODEL), F32),
                        pltpu.VMEM((t_chunk, D_MODEL), F32), pltpu.VMEM((1, D_MODEL), F32)],
        compiler_params=pltpu.CompilerParams(vmem_limit_bytes=VMEM_LIMIT),
        name="rglru_prompt",
    )(xr, cw, cb, wa, ba, wx, bx, lam)


def _rglru_step_kernel(xr_ref, cbuf_ref, h0_ref, cw_ref, cb_ref, wa_ref, ba_ref, wx_ref, bx_ref, lam_ref,
                       h_ref, a_s, b_s):
    xc = cbuf_ref[:, 0:D_MODEL] * cw_ref[0:1, :] + cb_ref[...]
    xc = xc + cbuf_ref[:, D_MODEL:2 * D_MODEL] * cw_ref[1:2, :]
    xc = xc + cbuf_ref[:, 2 * D_MODEL:3 * D_MODEL] * cw_ref[2:3, :]
    xc = xc + xr_ref[...] * cw_ref[3:4, :]
    _rglru_coeffs(xc, wa_ref, ba_ref, wx_ref, bx_ref, lam_ref, a_s, b_s)
    h_ref[...] = a_s[...] * h0_ref[...] + b_s[...]


def _rglru_step(xr, cbuf, h0, cw, cb, wa, ba, wx, bx, lam):
    n = xr.shape[0]
    return pl.pallas_call(
        _rglru_step_kernel,
        out_shape=jax.ShapeDtypeStruct((n, D_MODEL), F32),
        scratch_shapes=[pltpu.VMEM((n, D_MODEL), F32), pltpu.VMEM((n, D_MODEL), F32)],
        name="rglru_step",
    )(xr, cbuf, h0, cw, cb, wa, ba, wx, bx, lam)


CHUNK_ROWS = CMP_STRIDE * KV_ROWS


def _pool_weight_rows(praw_ref):
    praw = praw_ref[...]
    row = lax.broadcasted_iota(jnp.int32, praw.shape, 0)
    is_k = (row & (KV_ROWS - 1)) < N_KV_HEADS
    m_k = jnp.max(jnp.where(is_k, praw, -3e38), axis=0, keepdims=True)
    m_v = jnp.max(jnp.where(is_k, -3e38, praw), axis=0, keepdims=True)
    ex = jnp.exp(praw - jnp.where(is_k, m_k, m_v))
    s_k = jnp.sum(jnp.where(is_k, ex, 0.0), axis=0, keepdims=True) * (1.0 / N_KV_HEADS)
    s_v = jnp.sum(jnp.where(is_k, 0.0, ex), axis=0, keepdims=True) * (1.0 / N_KV_HEADS)
    return ex / jnp.where(is_k, s_k, s_v)


def _pool_rows(x, wv):
    n = x.shape[0] // CHUNK_ROWS
    x4 = x.reshape(n, CHUNK_ROWS // 8, 8, HEAD_DIM)
    w4 = wv.reshape(2, CHUNK_ROWS // 8, 8, HEAD_DIM)
    p0 = x4[:, 0] * w4[0, 0]
    p1 = x4[:, 0] * w4[1, 0]
    for v in range(1, CHUNK_ROWS // 8):
        p0 = p0 + x4[:, v] * w4[0, v]
        p1 = p1 + x4[:, v] * w4[1, v]
    return p0.reshape(n * 8, HEAD_DIM), p1.reshape(n * 8, HEAD_DIM)


def _pooled_head(p0_s, p1_s, eh, nc):
    def col(ref, start):
        return ref[pl.ds(start, nc, stride=8), :]
    return (col(p0_s, eh) + col(p0_s, eh + KV_ROWS)) + (col(p1_s, 8 + eh) + col(p1_s, 8 + eh + KV_ROWS))


def _compress_mlp(p0_s, p1_s, nc, w1_ref, w2_ref, kg_ref):
    outs = []
    for e in range(2):
        per_head = []
        for hd in range(N_KV_HEADS):
            p = _pooled_head(p0_s, p1_s, e * N_KV_HEADS + hd, nc)
            hid = _dot(p.astype(BF16), w1_ref[e])
            hid = hid * _sigmoid(hid)
            comp = p + _dot(hid.astype(BF16), w2_ref[e])
            if e == 0:
                comp = _rms_rows(comp, kg_ref[0:1, :])
            per_head.append(comp)
        outs.append(jnp.concatenate(per_head, axis=1))
    return outs[0], outs[1]


def _masked_softmax_parts(s, mask):
    sm = jnp.where(mask, s, NEG)
    p = jnp.exp(sm - jnp.max(sm, axis=-1, keepdims=True)) * mask.astype(F32)
    return p, jnp.maximum(jnp.sum(p, axis=-1, keepdims=True), 1e-30)


POOL_SLAB = 8


def _compress_prompt_kernel(x_ref, praw_ref, w1_ref, w2_ref, kg_ref, ck_ref, cvt_ref, p0_s, p1_s):
    nch = x_ref.shape[0] // CHUNK_ROWS
    wv = _pool_weight_rows(praw_ref)

    def slab(i, _):
        x = x_ref[pl.ds(pl.multiple_of(i * (POOL_SLAB * CHUNK_ROWS), POOL_SLAB * CHUNK_ROWS),
                        POOL_SLAB * CHUNK_ROWS), :]
        p0, p1 = _pool_rows(x, wv)
        rows = pl.ds(pl.multiple_of(i * (POOL_SLAB * 8), POOL_SLAB * 8), POOL_SLAB * 8)
        p0_s[rows, :] = p0
        p1_s[rows, :] = p1
        return 0

    lax.fori_loop(0, nch // POOL_SLAB, slab, 0)
    p1_s[pl.ds(nch * 8, 8), :] = jnp.zeros((8, HEAD_DIM), F32)
    ck, cv = _compress_mlp(p0_s, p1_s, nch, w1_ref, w2_ref, kg_ref)
    ck_ref[...] = ck
    cvt_ref[...] = cv.T


def _compress_prompt(kc_rows, praw, w1, w2, kg, bsz, seq):
    nch = seq // CMP_STRIDE

    def const_spec(shape):
        return pl.BlockSpec(shape, lambda b: (0,) * len(shape))

    return pl.pallas_call(
        _compress_prompt_kernel,
        grid=(bsz,),
        in_specs=[pl.BlockSpec((seq * KV_ROWS, HEAD_DIM), lambda b: (b, 0)),
                  const_spec((2 * CHUNK_ROWS, HEAD_DIM)), const_spec((2, HEAD_DIM, HEAD_DIM)),
                  const_spec((2, HEAD_DIM, HEAD_DIM)), const_spec((3, HEAD_DIM))],
        out_specs=[pl.BlockSpec((None, nch, 2 * HEAD_DIM), lambda b: (b, 0, 0)),
                   pl.BlockSpec((None, 2 * HEAD_DIM, nch), lambda b: (b, 0, 0))],
        out_shape=[jax.ShapeDtypeStruct((bsz, nch, 2 * HEAD_DIM), F32),
                   jax.ShapeDtypeStruct((bsz, 2 * HEAD_DIM, nch), F32)],
        scratch_shapes=[pltpu.VMEM(((nch + 1) * 8, HEAD_DIM), F32)] * 2,
        name="compress_prompt",
    )(kc_rows, praw, w1, w2, kg)


GQ = GROUP * Q_BLOCK
WIN_BLOCKS = WINDOW // Q_BLOCK + 1


def _bias_groups(s_t, bias):
    return jnp.concatenate([s_t[:, g * Q_BLOCK:(g + 1) * Q_BLOCK] + bias for g in range(GROUP)], axis=1)


def _fold8(x, op):
    parts = [x[i * 8:(i + 1) * 8, :] for i in range(x.shape[0] // 8)]
    while len(parts) > 1:
        parts = [op(parts[i], parts[i + 1]) for i in range(0, len(parts), 2)]
    return parts[0]


def _finish_t(o_t, m, l8):
    den = jnp.maximum(jnp.sum(l8, axis=0, keepdims=True), 1e-30)
    return jnp.where(m > 0.5 * NEG, o_t / den, 0.0)


def _nsa_prompt_kernel(q_ref, ck_ref, cvt_ref, sk_ref, svt_ref, wk_ref, wvt_ref, g_ref, o_ref,
                       bias_s, ps_s, pw_s):
    qi = pl.program_id(1)
    nq = Q_BLOCK
    ncp = ck_ref.shape[0]
    nkb = svt_ref.shape[0]
    nsel = nkb * (Q_BLOCK // SEL_BLOCK)
    n_top = min(TOP_N, nsel)
    start = qi * nq
    t_row = start + lax.broadcasted_iota(jnp.int32, (1, nq), 1)
    gates_t = g_ref[...].T

    n_sub = lax.broadcasted_iota(jnp.int32, (ncp, nq), 0)
    cmp_mask = (n_sub * CMP_STRIDE + (CMP_BLOCK - 1) <= t_row) & (n_sub < ncp - 1)
    jj = lax.broadcasted_iota(jnp.int32, (nsel, ncp), 0)
    nn = lax.broadcasted_iota(jnp.int32, (nsel, ncp), 1)
    ov_t = ((nn * CMP_STRIDE <= jj * SEL_BLOCK + SEL_BLOCK - 1)
            & (nn * CMP_STRIDE + CMP_BLOCK - 1 >= jj * SEL_BLOCK)
            & (nn < ncp - 1)).astype(F32)
    j_col = lax.broadcasted_iota(jnp.int32, (nsel, nq), 0)
    cur = _shr(t_row, SEL_SHIFT)
    valid_t = j_col <= cur
    forced_t = (j_col == 0) | (valid_t & (j_col > cur - N_LOCAL_BLOCKS))
    force_add = FORCE * forced_t.astype(F32)
    key_sub = lax.broadcasted_iota(jnp.int32, (Q_BLOCK, nq), 0)
    q_lane = lax.broadcasted_iota(jnp.int32, (Q_BLOCK, nq), 1)
    win_bias = {0: jnp.where(q_lane < key_sub, 0.0, NEG), WIN_BLOCKS - 1: jnp.where(q_lane >= key_sub, 0.0, NEG)}

    for h in range(N_KV_HEADS):
        qh = jnp.concatenate(
            [q_ref[:, (h * GROUP + g) * HEAD_DIM:(h * GROUP + g + 1) * HEAD_DIM] for g in range(GROUP)], axis=0)
        hs = slice(h * HEAD_DIM, (h + 1) * HEAD_DIM)

        s_c = _dot_nt(ck_ref[:, hs].astype(BF16), qh)
        p_groups = []
        for g in range(GROUP):
            sm = jnp.where(cmp_mask, s_c[:, g * nq:(g + 1) * nq], NEG)
            p = jnp.exp2(sm - jnp.max(sm, axis=0, keepdims=True)) * cmp_mask.astype(F32)
            p_groups.append(p / jnp.maximum(jnp.sum(p, axis=0, keepdims=True), 1e-30))
        o_c = _dot(cvt_ref[hs, :].astype(BF16), jnp.concatenate(p_groups, axis=1).astype(BF16))
        psum = p_groups[0] + p_groups[1] + p_groups[2] + p_groups[3]
        imp_t = jnp.dot(ov_t, psum, precision=lax.Precision.HIGHEST, preferred_element_type=F32)
        score_t = jnp.where(valid_t, imp_t + force_add, NEG)
        rank = jnp.zeros((nsel, nq), F32)
        for k in range(nsel):
            rk = score_t[k:k + 1, :]
            tie = jnp.where(j_col > k, 1.0, 0.0)
            rank = rank + jnp.where(rk > score_t, 1.0, jnp.where(rk == score_t, tie, 0.0))
        sel_f = jnp.where(rank < n_top, jnp.where(score_t > 0.5 * NEG, 1.0, 0.0), 0.0)

        def sel_branch(nblk):
            def run():
                for kb in range(nblk):
                    lo = sel_f[2 * kb:2 * kb + 1, :]
                    hi = sel_f[2 * kb + 1:2 * kb + 2, :]
                    picked = jnp.where(key_sub < SEL_BLOCK, lo, hi) > 0.5
                    bias_s[kb] = jnp.where(picked & (kb * Q_BLOCK + key_sub <= t_row), 0.0, NEG)

                def scores(kb):
                    return _bias_groups(_dot_nt(sk_ref[kb * Q_BLOCK:(kb + 1) * Q_BLOCK, hs], qh), bias_s[kb])

                m8 = jnp.full((8, GQ), NEG, F32)
                for kb in range(nblk):
                    m8 = jnp.maximum(m8, _fold8(scores(kb), jnp.maximum))
                m_s = jnp.max(m8, axis=0, keepdims=True)
                l_s = jnp.zeros((8, GQ), F32)
                for kb in range(nblk):
                    p = jnp.exp2(scores(kb) - m_s)
                    ps_s[kb * Q_BLOCK:(kb + 1) * Q_BLOCK, :] = p.astype(BF16)
                    l_s = l_s + _fold8(p, jnp.add)
                v_sel = jnp.concatenate([svt_ref[kb, hs, :] for kb in range(nblk)], axis=1)
                return _finish_t(_dot(v_sel, ps_s[0:nblk * Q_BLOCK, :]), m_s, l_s)
            return run

        quarter = nkb // 4
        o_s = lax.switch(qi // quarter, [sel_branch((v + 1) * quarter) for v in range(4)])

        def win_scores(r):
            kb = qi - (WIN_BLOCKS - 1) + r
            kbc = jnp.maximum(kb, 0)
            off = pl.multiple_of(kbc * Q_BLOCK, Q_BLOCK)
            s_t = _dot_nt(wk_ref[pl.ds(off, Q_BLOCK), hs], qh) + jnp.where(kb < 0, NEG, 0.0)
            if r in win_bias:
                s_t = _bias_groups(s_t, win_bias[r])
            return s_t, kbc

        m8 = jnp.full((8, GQ), NEG, F32)
        for r in range(WIN_BLOCKS):
            m8 = jnp.maximum(m8, _fold8(win_scores(r)[0], jnp.maximum))
        m_w = jnp.max(m8, axis=0, keepdims=True)
        l_w = jnp.zeros((8, GQ), F32)
        v_blocks = []
        for r in range(WIN_BLOCKS):
            s_t, kbc = win_scores(r)
            p = jnp.exp2(s_t - m_w)
            pw_s[r * Q_BLOCK:(r + 1) * Q_BLOCK, :] = p.astype(BF16)
            l_w = l_w + _fold8(p, jnp.add)
            v_blocks.append(wvt_ref[kbc, hs, :])
        o_w = _finish_t(_dot(jnp.concatenate(v_blocks, axis=1), pw_s[...]), m_w, l_w)

        for g in range(GROUP):
            head = h * GROUP + g
            cs = slice(g * nq, (g + 1) * nq)
            o = (o_c[:, cs] * gates_t[head:head + 1, :]
                 + o_s[:, cs] * gates_t[N_HEADS + head:N_HEADS + head + 1, :]
                 + o_w[:, cs] * gates_t[2 * N_HEADS + head:2 * N_HEADS + head + 1, :])
            o_ref[:, head * HEAD_DIM:(head + 1) * HEAD_DIM] = o.T.astype(o_ref.dtype)


def _nsa_prompt(q, ck, cvt, sk, svt, wk, wvt, gates, bsz, seq):
    nqb = seq // Q_BLOCK
    ncp = ck.shape[1]
    kv_heads_w = N_KV_HEADS * HEAD_DIM
    k_spec = pl.BlockSpec((seq, kv_heads_w), lambda b, i: (b, 0))
    vt_spec = pl.BlockSpec((nqb, kv_heads_w, Q_BLOCK), lambda b, i: (b, 0, 0))
    return pl.pallas_call(
        _nsa_prompt_kernel,
        grid=(bsz, nqb),
        in_specs=[pl.BlockSpec((Q_BLOCK, D_MODEL), lambda b, i: (b * nqb + i, 0)),
                  pl.BlockSpec((None, ncp, kv_heads_w), lambda b, i: (b, 0, 0)),
                  pl.BlockSpec((None, kv_heads_w, ncp), lambda b, i: (b, 0, 0)),
                  k_spec, vt_spec, k_spec, vt_spec,
                  pl.BlockSpec((Q_BLOCK, 128), lambda b, i: (b * nqb + i, 0))],
        out_specs=pl.BlockSpec((Q_BLOCK, D_MODEL), lambda b, i: (b * nqb + i, 0)),
        out_shape=jax.ShapeDtypeStruct((bsz * seq, D_MODEL), BF16),
        scratch_shapes=[pltpu.VMEM((nqb, Q_BLOCK, Q_BLOCK), F32),
                        pltpu.VMEM((seq, GQ), BF16),
                        pltpu.VMEM((WIN_BLOCKS * Q_BLOCK, GQ), BF16)],
        compiler_params=pltpu.CompilerParams(vmem_limit_bytes=VMEM_LIMIT),
        name="nsa_prompt",
    )(q, ck, cvt, sk, svt, wk, wvt, gates)


def _nsa_sample_cmp_kernel(pt_ref, *refs, past, n_steps, pps):
    page_refs = refs[:pps]
    (kcn_ref, praw_ref, w1_ref, w2_ref, kg_ref, q_ref, oc_ref, imp_ref, wv_s, p0_s, p1_s) = refs[pps:]
    del pt_ref
    g = pl.program_id(1)
    chunks_per_page = PAGE_SIZE // CMP_STRIDE
    nchunk = past // CMP_STRIDE
    nsel = past // SEL_BLOCK + 1
    t_now = past

    @pl.when(g == 0)
    def _():
        wv_s[...] = _pool_weight_rows(praw_ref)

    wv = wv_s[...]
    for k in range(pps):
        p0, p1 = _pool_rows(page_refs[k][...], wv)
        rows = pl.ds(pl.multiple_of((g * pps + k) * (chunks_per_page * 8), chunks_per_page * 8),
                     chunks_per_page * 8)
        p0_s[rows, :] = p0
        p1_s[rows, :] = p1

    @pl.when(g == n_steps - 1)
    def _():
        p1_s[pl.ds(nchunk * 8, KV_ROWS), :] = kcn_ref[...] * wv_s[pl.ds(CHUNK_ROWS, KV_ROWS), :]
        p1_s[pl.ds(nchunk * 8 + KV_ROWS, KV_ROWS), :] = jnp.zeros((KV_ROWS, HEAD_DIM), F32)
        ck, cv = _compress_mlp(p0_s, p1_s, nchunk, w1_ref, w2_ref, kg_ref)
        q8 = q_ref[...]
        row = lax.broadcasted_iota(jnp.int32, (N_HEADS, 1), 0)
        first = row < GROUP
        n_row = lax.broadcasted_iota(jnp.int32, (1, nchunk), 1)
        mask = jnp.broadcast_to(n_row * CMP_STRIDE + (CMP_BLOCK - 1) <= t_now, (N_HEADS, nchunk))
        ckb = ck.astype(BF16)
        cvb = cv.astype(BF16)
        s = jnp.where(first, _dot_nt(q8, ckb[:, 0:HEAD_DIM]), _dot_nt(q8, ckb[:, HEAD_DIM:2 * HEAD_DIM]))
        p, den = _masked_softmax_parts(s, mask)
        p = p / den
        pb = p.astype(BF16)
        oc_ref[...] = jnp.where(first, _dot(pb, cvb[:, 0:HEAD_DIM]), _dot(pb, cvb[:, HEAD_DIM:2 * HEAD_DIM]))
        nn = lax.broadcasted_iota(jnp.int32, (nchunk, 256), 0)
        jj = lax.broadcasted_iota(jnp.int32, (nchunk, 256), 1)
        ov = ((nn * CMP_STRIDE <= jj * SEL_BLOCK + SEL_BLOCK - 1)
              & (nn * CMP_STRIDE + CMP_BLOCK - 1 >= jj * SEL_BLOCK) & (jj < nsel)).astype(F32)
        imp8 = jnp.dot(p, ov, precision=lax.Precision.HIGHEST, preferred_element_type=F32)
        imp_ref[0:1, :] = jnp.sum(jnp.where(first, imp8, 0.0), axis=0, keepdims=True)
        imp_ref[1:2, :] = jnp.sum(jnp.where(first, 0.0, imp8), axis=0, keepdims=True)


def _nsa_sample_cmp(page_table, cache_cmp, kc_new, pwt, w1, w2, kg, q8):
    db, n_pages = page_table.shape
    past = n_pages * PAGE_SIZE
    pps = min(PAGES_PER_STEP, n_pages)
    n_steps = n_pages // pps
    pages = cache_cmp.reshape(cache_cmp.shape[0], PAGE_SIZE * KV_ROWS, HEAD_DIM)

    def page_spec(k):
        return pl.BlockSpec((None, PAGE_SIZE * KV_ROWS, HEAD_DIM),
                            lambda b, g, pt: (pt[b * n_pages + g * pps + k], 0, 0))

    def const_spec(shape):
        return pl.BlockSpec(shape, lambda b, g, pt: (0,) * len(shape))

    nchunk = past // CMP_STRIDE
    grid_spec = pltpu.PrefetchScalarGridSpec(
        num_scalar_prefetch=1,
        grid=(db, n_steps),
        in_specs=[page_spec(k) for k in range(pps)] + [
            pl.BlockSpec((None, KV_ROWS, HEAD_DIM), lambda b, g, pt: (b, 0, 0)),
            const_spec((2 * CHUNK_ROWS, HEAD_DIM)), const_spec((2, HEAD_DIM, HEAD_DIM)),
            const_spec((2, HEAD_DIM, HEAD_DIM)), const_spec((3, HEAD_DIM)),
            pl.BlockSpec((None, N_HEADS, HEAD_DIM), lambda b, g, pt: (b, 0, 0))],
        out_specs=[pl.BlockSpec((None, N_HEADS, HEAD_DIM), lambda b, g, pt: (b, 0, 0)),
                   pl.BlockSpec((None, 2, 256), lambda b, g, pt: (b, 0, 0))],
        scratch_shapes=[pltpu.VMEM((2 * CHUNK_ROWS, HEAD_DIM), F32),
                        pltpu.VMEM(((nchunk + 1) * 8, HEAD_DIM), F32),
                        pltpu.VMEM(((nchunk + 1) * 8, HEAD_DIM), F32)])
    return pl.pallas_call(
        functools.partial(_nsa_sample_cmp_kernel, past=past, n_steps=n_steps, pps=pps),
        grid_spec=grid_spec,
        out_shape=[jax.ShapeDtypeStruct((db, N_HEADS, HEAD_DIM), F32),
                   jax.ShapeDtypeStruct((db, 2, 256), F32)],
        compiler_params=pltpu.CompilerParams(vmem_limit_bytes=VMEM_LIMIT),
        name="nsa_sample_cmp",
    )(page_table.reshape(-1), *([pages] * pps), kc_new, pwt, w1, w2, kg, q8)


def _select_kernel(imp_ref, idx_ref, *, t_now, nsel):
    imp = imp_ref[...]
    rows, width = imp.shape
    j = lax.broadcasted_iota(jnp.int32, (rows, width), 1)
    jf = j.astype(F32)
    cur = t_now // SEL_BLOCK
    valid = (j <= cur) & (j < nsel)
    forced = (j == 0) | (valid & (j > cur - N_LOCAL_BLOCKS))
    score = jnp.where(valid, imp + FORCE * forced.astype(F32), NEG)
    col = lax.broadcasted_iota(jnp.int32, (rows, 128), 1)
    out = jnp.full((rows, 128), -1, jnp.int32)
    for it in range(min(TOP_N, nsel)):
        m = jnp.max(score, axis=-1, keepdims=True)
        idx = jnp.min(jnp.where(score == m, jf, 1e9), axis=-1, keepdims=True)
        out = jnp.where(col == it, jnp.where(m > 0.5 * NEG, idx.astype(jnp.int32), -1), out)
        score = jnp.where(jf == idx, -3e38, score)
    idx_ref[...] = out


def _select(imp2d, t_now, nsel):
    rows = imp2d.shape[0]
    return pl.pallas_call(
        functools.partial(_select_kernel, t_now=t_now, nsel=nsel),
        out_shape=jax.ShapeDtypeStruct((rows, 128), jnp.int32),
        name="nsa_sample_select",
    )(imp2d)


def _nsa_sample_sel_kernel(idx_ref, pt_ref, *refs, n_past_blk, n_top, nseq):
    per_seq = N_KV_HEADS * n_top
    q_ref, ksn_ref, os_ref = refs[nseq * per_seq:]
    del pt_ref
    for j in range(nseq):
        _sel_one_sequence(idx_ref, pl.program_id(0) * nseq + j, refs[j * per_seq:(j + 1) * per_seq],
                          q_ref.at[j], ksn_ref.at[j], os_ref.at[j], n_past_blk, n_top)


def _sel_one_sequence(idx_ref, b, blk_refs, q_ref, ksn_ref, os_ref, n_past_blk, n_top):
    q8 = q_ref[...]
    q8f = q8.astype(F32)
    ksn = ksn_ref[...]
    blk_rows = SEL_BLOCK * KV_ROWS
    nrows = n_top * blk_rows
    lane = lax.broadcasted_iota(jnp.int32, (1, nrows), 1)
    row = lax.broadcasted_iota(jnp.int32, (N_HEADS, 1), 0)
    first = row < GROUP
    per_head = []
    for h in range(N_KV_HEADS):
        rows_b = jnp.concatenate([blk_refs[h * n_top + n][...] for n in range(n_top)], axis=0).astype(BF16)
        k_new = ksn[h:h + 1, :]
        v_new = ksn[N_KV_HEADS + h:N_KV_HEADS + h + 1, :]
        slot_ok = jnp.zeros((1, nrows), F32)
        new_ok = jnp.zeros((1, 1), F32)
        for n in range(n_top):
            ix = idx_ref[(b * N_KV_HEADS + h) * n_top + n]
            past_ok = jnp.where((ix >= 0) & (ix < n_past_blk), 1.0, 0.0)
            slot_ok = jnp.where((lane >= n * blk_rows) & (lane < (n + 1) * blk_rows), past_ok, slot_ok)
            new_ok = jnp.maximum(new_ok, jnp.where(ix >= n_past_blk, 1.0, 0.0))
        mask = jnp.broadcast_to((slot_ok > 0.5) & ((lane & (KV_ROWS - 1)) == h), (N_HEADS, nrows))
        new_mask = jnp.broadcast_to(new_ok > 0.5, (N_HEADS, 1))
        s = jnp.where(mask, _dot_nt(q8, rows_b), NEG)
        s_new = jnp.where(new_mask, jnp.sum(q8f * k_new, axis=-1, keepdims=True), NEG)
        mx = jnp.maximum(jnp.max(s, axis=-1, keepdims=True), s_new)
        p = jnp.exp(s - mx) * mask.astype(F32)
        p_new = jnp.exp(s_new - mx) * new_mask.astype(F32)
        den = jnp.maximum(jnp.sum(p, axis=-1, keepdims=True) + p_new, 1e-30)
        p_on_v = pltpu.roll(p, N_KV_HEADS, axis=1).astype(BF16)
        per_head.append((_dot(p_on_v, rows_b) + p_new * v_new) / den)
    os_ref[...] = jnp.where(first, per_head[0], per_head[1])


def _nsa_sample_sel(sel_idx, page_table, cache_sel, q8, ks_new):
    db, n_pages = page_table.shape
    n_past_blk = n_pages * PAGE_SIZE // SEL_BLOCK
    n_top = sel_idx.shape[-1]
    halves = PAGE_SIZE // SEL_BLOCK
    blocks = cache_sel.reshape(cache_sel.shape[0] * halves, SEL_BLOCK * KV_ROWS, HEAD_DIM)
    blk = jnp.clip(sel_idx.reshape(db, N_KV_HEADS * n_top), 0, n_past_blk - 1)
    phys = jnp.take_along_axis(page_table, blk // halves, axis=1) * halves + blk % halves

    nseq = SEL_SEQS_PER_STEP if db % SEL_SEQS_PER_STEP == 0 else 1

    def blk_spec(j, h, n):
        return pl.BlockSpec((None, SEL_BLOCK * KV_ROWS, HEAD_DIM),
                            lambda b, idx, pb: (pb[((b * nseq + j) * N_KV_HEADS + h) * n_top + n], 0, 0))

    grid_spec = pltpu.PrefetchScalarGridSpec(
        num_scalar_prefetch=2,
        grid=(db // nseq,),
        in_specs=[blk_spec(j, h, n) for j in range(nseq) for h in range(N_KV_HEADS) for n in range(n_top)] + [
            pl.BlockSpec((nseq, N_HEADS, HEAD_DIM), lambda b, idx, pt: (b, 0, 0)),
            pl.BlockSpec((nseq, KV_ROWS, HEAD_DIM), lambda b, idx, pt: (b, 0, 0))],
        out_specs=pl.BlockSpec((nseq, N_HEADS, HEAD_DIM), lambda b, idx, pt: (b, 0, 0)))
    return pl.pallas_call(
        functools.partial(_nsa_sample_sel_kernel, n_past_blk=n_past_blk, n_top=n_top, nseq=nseq),
        grid_spec=grid_spec,
        out_shape=jax.ShapeDtypeStruct((db, N_HEADS, HEAD_DIM), F32),
        compiler_params=pltpu.CompilerParams(vmem_limit_bytes=VMEM_LIMIT),
        name="nsa_sample_sel",
    )(sel_idx.reshape(-1), phys.reshape(-1), *([blocks] * (nseq * N_KV_HEADS * n_top)), q8, ks_new)


def _nsa_sample_win_kernel(win_ref, kwn_ref, q_ref, wout_ref, ow_ref, *, past):
    for j in range(win_ref.shape[0]):
        _win_one_sequence(win_ref.at[j], kwn_ref.at[j], q_ref.at[j], wout_ref.at[j], ow_ref.at[j], past)


def _win_one_sequence(win_ref, kwn_ref, q_ref, wout_ref, ow_ref, past):
    nrows = win_ref.shape[0]
    wb = nrows // KV_ROWS
    kwn = kwn_ref[...]
    x = win_ref[...]
    wout_ref[...] = pltpu.roll(x, nrows - KV_ROWS, axis=0)
    wout_ref[pl.ds(nrows - KV_ROWS, KV_ROWS), :] = kwn
    q8 = q_ref[...]
    q8f = q8.astype(F32)
    t_now = past
    lane = lax.broadcasted_iota(jnp.int32, (N_HEADS, nrows), 1)
    row = lax.broadcasted_iota(jnp.int32, (N_HEADS, nrows), 0)
    pos = past - wb + _shr(lane, KV_ROWS.bit_length() - 1)
    d = t_now - pos
    mask = ((d >= 0) & (d < WINDOW) & (pos >= 0)
            & ((lane & (KV_ROWS - 1)) == _shr(row, GROUP.bit_length() - 1)))
    first = lax.broadcasted_iota(jnp.int32, (N_HEADS, 1), 0) < GROUP
    k_new = jnp.where(first, kwn[0:1, :], kwn[1:2, :])
    v_new = jnp.where(first, kwn[N_KV_HEADS:N_KV_HEADS + 1, :], kwn[N_KV_HEADS + 1:N_KV_HEADS + 2, :])
    xb = x.astype(BF16)
    s = jnp.where(mask, _dot_nt(q8, xb), NEG)
    s_new = jnp.sum(q8f * k_new, axis=-1, keepdims=True)
    mx = jnp.maximum(jnp.max(s, axis=-1, keepdims=True), s_new)
    p = jnp.exp(s - mx) * mask.astype(F32)
    p_new = jnp.exp(s_new - mx)
    den = jnp.maximum(jnp.sum(p, axis=-1, keepdims=True) + p_new, 1e-30)
    p_on_v = pltpu.roll(p, N_KV_HEADS, axis=1).astype(BF16)
    ow_ref[...] = (_dot(p_on_v, xb) + p_new * v_new) / den


def _nsa_sample_win(win_rows, kw_new, q8, past):
    db, nrows, _ = win_rows.shape
    nseq = WIN_SEQS_PER_STEP if db % WIN_SEQS_PER_STEP == 0 else 1
    return pl.pallas_call(
        functools.partial(_nsa_sample_win_kernel, past=past),
        grid=(db // nseq,),
        in_specs=[pl.BlockSpec((nseq, nrows, HEAD_DIM), lambda b: (b, 0, 0)),
                  pl.BlockSpec((nseq, KV_ROWS, HEAD_DIM), lambda b: (b, 0, 0)),
                  pl.BlockSpec((nseq, N_HEADS, HEAD_DIM), lambda b: (b, 0, 0))],
        out_specs=[pl.BlockSpec((nseq, nrows, HEAD_DIM), lambda b: (b, 0, 0)),
                   pl.BlockSpec((nseq, N_HEADS, HEAD_DIM), lambda b: (b, 0, 0))],
        out_shape=[jax.ShapeDtypeStruct((db, nrows, HEAD_DIM), F32),
                   jax.ShapeDtypeStruct((db, N_HEADS, HEAD_DIM), F32)],
        name="nsa_sample_win",
    )(win_rows, kw_new, q8)


def _gate_mix_kernel(oc_ref, os_ref, ow_ref, g_ref, o_ref):
    g = g_ref[...]
    c = lax.broadcasted_iota(jnp.int32, (128, D_MODEL), 0)
    head = _shr(lax.broadcasted_iota(jnp.int32, (128, D_MODEL), 1), HEAD_SHIFT)
    acc = jnp.zeros(o_ref.shape, F32)
    for r, ref in enumerate((oc_ref, os_ref, ow_ref)):
        expand = (c == r * N_HEADS + head).astype(F32)
        acc = acc + ref[...] * jnp.dot(g, expand, precision=lax.Precision.HIGHEST, preferred_element_type=F32)
    o_ref[...] = acc


def _merge_kernel(x_ref, a_ref, ra_ref, b_ref, on_ref, gate_ref, w_ref, y_ref):
    f32 = lambda ref: ref[...].astype(F32)
    u = f32(a_ref) * f32(ra_ref) + f32(b_ref) * f32(on_ref)
    y_ref[...] = x_ref[...] + gate_ref[0] * _dot(u.astype(BF16), w_ref[...])


def _merge(x2d, a, ra, b, o_nsa, gate, w_out, tm, rows_per_mod):
    m = x2d.shape[0]
    tiles_per_mod = rows_per_mod // tm
    row_spec = pl.BlockSpec((tm, D_MODEL), lambda i: (i, 0))
    return pl.pallas_call(
        _merge_kernel,
        grid=(m // tm,),
        in_specs=[row_spec] * 5 + [
            pl.BlockSpec((1, gate.shape[1], D_MODEL), lambda i: (i // tiles_per_mod, 0, 0)),
            pl.BlockSpec((D_MODEL, D_MODEL), lambda i: (0, 0))],
        out_specs=row_spec,
        out_shape=jax.ShapeDtypeStruct((m, D_MODEL), F32),
        compiler_params=pltpu.CompilerParams(vmem_limit_bytes=VMEM_LIMIT),
        name="merge_out_proj",
    )(x2d, a, ra, b, o_nsa, gate, w_out)


def _split_w_in(w_in):
    n_bg = 3 * N_HEADS
    lo = w_in[:, :C_ZN].astype(BF16)
    hi = w_in[:, C_ZN + n_bg:].astype(BF16)
    bg = jnp.pad(w_in[:, C_ZN:C_ZN + n_bg], ((0, 0), (0, C_END - C_BG - n_bg))).astype(BF16)
    return lo, hi, bg


def kernel(x_prompt, x_sample, c_prompt, c_sample, state_conv, state_rglru, cache_cmp_kv, cache_sel_kv,
           state_win_kv, page_table, norm_g, w_ada, b_ada, w_in, conv_w, conv_b, rg_wa, rg_ba, rg_wx, rg_bx,
           rg_lambda, q_norm_g, k_norm_g, cmp_pool_w, cmp_w1, cmp_w2, w_out):
    depth = norm_g.shape[0]
    assert depth == 1 and x_sample.shape[1] == 1
    bsz, seq, _ = x_prompt.shape
    db = x_sample.shape[0]
    n_pages = page_table.shape[1]
    past = n_pages * PAGE_SIZE
    layer = 0

    w_cat = _split_w_in(w_in[layer])
    w_out_b = w_out[layer].astype(BF16)
    wa_b = rg_wa[layer].astype(BF16)
    wx_b = rg_wx[layer].astype(BF16)
    w1_b = cmp_w1[layer].astype(BF16)
    w2_b = cmp_w2[layer].astype(BF16)
    row = lambda v: v.reshape(1, -1)
    rg_args = (conv_w[layer], row(conv_b[layer]), wa_b, row(rg_ba[layer]), wx_b, row(rg_bx[layer]),
               row(rg_lambda[layer]))
    praw = jnp.broadcast_to(
        cmp_pool_w[layer].reshape(2, 2, CMP_STRIDE).transpose(1, 2, 0)[:, :, :, None, None],
        (2, CMP_STRIDE, 2, N_KV_HEADS, HEAD_DIM)).reshape(2 * CHUNK_ROWS, HEAD_DIM)
    kg = k_norm_g[layer]

    mod = _modulation(jnp.concatenate([c_prompt, c_sample], axis=0), w_ada[layer], b_ada[layer])
    shift, scale, gate = mod[:, :D_MODEL], mod[:, D_MODEL:2 * D_MODEL], mod[:, 2 * D_MODEL:]

    xp2 = x_prompt.reshape(bsz * seq, D_MODEL)
    pm = lambda v: v[:bsz].reshape(bsz, 1, D_MODEL)
    (xr_p, a_p, b_p, q_p, kc_p, ks_p, kw_p, skb_p, svt_p, wkb_p, wvt_p, g_p) = _project(
        xp2, pm(shift), pm(scale), norm_g[layer], w_cat, q_norm_g[layer], kg, tm=256, rows_per_mod=seq,
        q_scale=HEAD_DIM ** -0.5 * LOG2E)
    ra_p, h_p = _rglru_prompt(xr_p, bsz, seq, *rg_args)
    ck_p, cvt_p = _compress_prompt(kc_p, praw, w1_b, w2_b, kg, bsz, seq)
    on_p = _nsa_prompt(q_p, ck_p, cvt_p, skb_p, svt_p, wkb_p, wvt_p, g_p, bsz, seq)
    y_p = _merge(xp2, a_p, ra_p, b_p, on_p, pm(gate), w_out_b, tm=512, rows_per_mod=seq)

    xs2 = x_sample.reshape(db, D_MODEL)
    sm = lambda v: v[bsz:].reshape(1, db, D_MODEL)
    (xr_s, a_s, b_s, q_s, kc_s, ks_s, kw_s, _, _, _, _, g_s) = _project(
        xs2, sm(shift), sm(scale), norm_g[layer], w_cat, q_norm_g[layer], kg, tm=db, rows_per_mod=db,
        q_scale=HEAD_DIM ** -0.5)
    h_s = _rglru_step(xr_s, state_conv[layer].reshape(db, (CONV_W - 1) * D_MODEL), state_rglru[layer], *rg_args)
    q8 = q_s.reshape(db, N_HEADS, HEAD_DIM)
    new_rows = lambda v: v.reshape(db, KV_ROWS, HEAD_DIM)
    n_phys = cache_cmp_kv.shape[1]
    oc_s, imp = _nsa_sample_cmp(page_table, cache_cmp_kv.reshape(depth * n_phys, PAGE_SIZE * KV_ROWS, HEAD_DIM),
                                new_rows(kc_s), praw, w1_b, w2_b, kg, q8)
    nsel = past // SEL_BLOCK + 1
    sel_idx = _select(imp.reshape(db * N_KV_HEADS, 256), past, nsel)[:, :min(TOP_N, nsel)]
    os_s = _nsa_sample_sel(sel_idx, page_table,
                           cache_sel_kv.reshape(depth * n_phys, PAGE_SIZE * KV_ROWS, HEAD_DIM), q8, new_rows(ks_s))
    wb = state_win_kv.shape[2]
    win_s, ow_s = _nsa_sample_win(state_win_kv.reshape(depth * db, wb * KV_ROWS, HEAD_DIM), new_rows(kw_s), q8, past)
    on_s = pl.pallas_call(
        _gate_mix_kernel, out_shape=jax.ShapeDtypeStruct((db, D_MODEL), F32), name="nsa_sample_mix",
    )(oc_s.reshape(db, D_MODEL), os_s.reshape(db, D_MODEL), ow_s.reshape(db, D_MODEL), g_s)
    y_s = _merge(xs2, a_s, h_s, b_s, on_s, sm(gate), w_out_b, tm=db, rows_per_mod=db)

    kv_shape = (2, N_KV_HEADS, HEAD_DIM)
    xr_p3 = xr_p.reshape(bsz, seq, D_MODEL)
    conv_prompt = xr_p3[:, seq - (CONV_W - 1):][None]
    conv_sample = jnp.concatenate([state_conv[layer][:, 1:], xr_s[:, None, :]], axis=1)[None]
    win_len = min(WINDOW, seq)
    return (y_p.reshape(bsz, seq, D_MODEL), y_s.reshape(db, 1, D_MODEL),
            conv_prompt, conv_sample,
            h_p.reshape(1, bsz, D_MODEL), h_s.reshape(1, db, D_MODEL),
            kc_p.reshape(1, bsz, seq, *kv_shape), kc_s.reshape(1, db, 1, *kv_shape),
            ks_p.reshape(1, bsz, seq, *kv_shape), ks_s.reshape(1, db, 1, *kv_shape),
            kw_p.reshape(bsz, seq * KV_ROWS, HEAD_DIM)[:, (seq - win_len) * KV_ROWS:].reshape(
                1, bsz, win_len, *kv_shape),
            win_s.reshape(1, db, wb, *kv_shape))
```

```python
import functools

import jax
import jax.numpy as jnp
from jax import lax
from jax.experimental import pallas as pl
from jax.experimental.pallas import tpu as pltpu

F32 = jnp.float32
BF16 = jnp.bfloat16

D_MODEL = 1024
RG_BLOCKS = 8
RG_BW = D_MODEL // RG_BLOCKS
RG_C = 8.0
CONV_W = 4
N_HEADS = 8
N_KV_HEADS = 2
GROUP = N_HEADS // N_KV_HEADS
HEAD_DIM = D_MODEL // N_HEADS
KV_W = 2 * N_KV_HEADS * HEAD_DIM
KV_ROWS = 2 * N_KV_HEADS
CMP_BLOCK = 32
CMP_STRIDE = 16
SEL_BLOCK = 64
TOP_N = 16
N_LOCAL_BLOCKS = 2
WINDOW = 512
Q_BLOCK = 128
PAGE_SIZE = 128
EPS = 1e-6
LOG2E = 1.4426950408889634
NEG = -1e30
FORCE = 1e4

C_XR, C_ZR, C_Q, C_KC, C_KS, C_KW, C_ZN, C_GA, C_GB, C_BG, C_END = (
    0, 1024, 2048, 3072, 3584, 4096, 4608, 5632, 6656, 7680, 7808)
VMEM_LIMIT = 56 * 1024 * 1024
PAGES_PER_STEP = 64
WIN_SEQS_PER_STEP = 8
SEL_SEQS_PER_STEP = 4

SEL_SHIFT = SEL_BLOCK.bit_length() - 1
HEAD_SHIFT = HEAD_DIM.bit_length() - 1


def _shr(x, k):
    return lax.shift_right_arithmetic(x, jnp.int32(k))


def _sigmoid(x):
    return jax.nn.sigmoid(x)


def _dot(a, b):
    return jnp.dot(a, b, preferred_element_type=F32)


def _dot_nt(a, b):
    return lax.dot_general(a, b, (((1,), (1,)), ((), ())), preferred_element_type=F32)


def _rms_rows(x, g):
    return x * lax.rsqrt(jnp.mean(x * x, axis=-1, keepdims=True) + EPS) * g


def _mod_kernel(c_ref, w_ref, b_ref, o_ref):
    c = c_ref[...]
    s = c * _sigmoid(c)
    o_ref[...] = _dot(s.astype(BF16), w_ref[...].astype(BF16)) + b_ref[...]


def _modulation(c_all, w_ada, b_ada):
    n = c_all.shape[0]
    return pl.pallas_call(
        _mod_kernel,
        grid=(3,),
        in_specs=[pl.BlockSpec((n, D_MODEL), lambda j: (0, 0)),
                  pl.BlockSpec((D_MODEL, D_MODEL), lambda j: (0, j)),
                  pl.BlockSpec((1, D_MODEL), lambda j: (0, j))],
        out_specs=pl.BlockSpec((n, D_MODEL), lambda j: (0, j)),
        out_shape=jax.ShapeDtypeStruct((n, 3 * D_MODEL), F32),
        name="adaln_mod",
    )(c_all, w_ada, b_ada.reshape(1, 3 * D_MODEL))


def _proj_kernel(x_ref, shift_ref, scale_ref, ng_ref, w_lo_ref, w_hi_ref, w_bg_ref, qg_ref, kg_ref,
                 xr_ref, a_ref, b_ref, q_ref, kc_ref, ks_ref, kw_ref, skb_ref, svt_ref, wkb_ref, wvt_ref, g_ref,
                 *, q_scale):
    x = x_ref[...]
    h = _rms_rows(x, ng_ref[...])
    h = h * (1.0 + scale_ref[0]) + shift_ref[0]
    hb = h.astype(BF16)

    def mm(lo, hi):
        if hi <= C_ZN:
            return _dot(hb, w_lo_ref[:, lo:hi])
        if hi <= C_BG:
            return _dot(hb, w_hi_ref[:, lo - C_ZN:hi - C_ZN])
        return _dot(hb, w_bg_ref[...])

    xr_ref[...] = mm(C_XR, C_ZR)
    zr = mm(C_ZR, C_Q)
    ga = mm(C_GA, C_GB)
    a_ref[...] = (_sigmoid(ga) * (zr * _sigmoid(zr))).astype(a_ref.dtype)
    zn = mm(C_ZN, C_GA)
    gb = mm(C_GB, C_BG)
    b_ref[...] = (_sigmoid(gb) * (zn * _sigmoid(zn))).astype(b_ref.dtype)
    q = mm(C_Q, C_KC)
    for hd in range(N_HEADS):
        sl = slice(hd * HEAD_DIM, (hd + 1) * HEAD_DIM)
        q_ref[:, sl] = (_rms_rows(q[:, sl], qg_ref[...]) * q_scale).astype(BF16)
    tm = x.shape[0]

    def store_kv(o_ref, eh, val):
        o_ref[pl.ds(eh, tm, stride=KV_ROWS), :] = val

    kc = mm(C_KC, C_KS)
    for eh in range(KV_ROWS):
        store_kv(kc_ref, eh, kc[:, eh * HEAD_DIM:(eh + 1) * HEAD_DIM])
    for lo, hi, o_ref, kb_ref, vt_ref, gi in ((C_KS, C_KW, ks_ref, skb_ref, svt_ref, 1),
                                              (C_KW, C_ZN, kw_ref, wkb_ref, wvt_ref, 2)):
        kv = mm(lo, hi)
        for hd in range(N_KV_HEADS):
            sl = slice(hd * HEAD_DIM, (hd + 1) * HEAD_DIM)
            kn = _rms_rows(kv[:, sl], kg_ref[gi:gi + 1, :])
            store_kv(o_ref, hd, kn)
            kb_ref[:, sl] = kn.astype(BF16)
            v = kv[:, (N_KV_HEADS + hd) * HEAD_DIM:(N_KV_HEADS + hd + 1) * HEAD_DIM]
            store_kv(o_ref, N_KV_HEADS + hd, v)
            v_t = v.T.astype(BF16)
            for j in range(tm // Q_BLOCK):
                vt_ref[j, sl, :] = v_t[:, j * Q_BLOCK:(j + 1) * Q_BLOCK]
    g_ref[...] = _sigmoid(mm(C_BG, C_END))


def _project(x2d, shift, scale, norm_g, w_parts, q_norm_g, k_norm_g, tm, rows_per_mod, q_scale):
    m = x2d.shape[0]
    tiles_per_mod = rows_per_mod // tm
    mod_rows = shift.shape[1]
    mod_spec = pl.BlockSpec((1, mod_rows, D_MODEL), lambda i: (i // tiles_per_mod, 0, 0))

    def row_spec(width, mult=1):
        return pl.BlockSpec((tm * mult, width), lambda i: (i, 0))

    def const_spec(shape):
        return pl.BlockSpec(shape, lambda i: (0,) * len(shape))

    kv_heads_w = N_KV_HEADS * HEAD_DIM
    k_bf16 = (pl.BlockSpec((tm, kv_heads_w), lambda i: (i, 0)), jax.ShapeDtypeStruct((m, kv_heads_w), BF16))
    vt_bf16 = (pl.BlockSpec((tm // Q_BLOCK, kv_heads_w, Q_BLOCK), lambda i: (i, 0, 0)),
               jax.ShapeDtypeStruct((m // Q_BLOCK, kv_heads_w, Q_BLOCK), BF16))

    def rows(width, dtype, mult=1):
        return row_spec(width, mult), jax.ShapeDtypeStruct((m * mult, width), dtype)

    outs = (rows(D_MODEL, F32), rows(D_MODEL, BF16), rows(D_MODEL, BF16), rows(D_MODEL, BF16),
            rows(HEAD_DIM, F32, KV_ROWS), rows(HEAD_DIM, F32, KV_ROWS), rows(HEAD_DIM, F32, KV_ROWS),
            k_bf16, vt_bf16, k_bf16, vt_bf16, rows(128, F32))
    return pl.pallas_call(
        functools.partial(_proj_kernel, q_scale=q_scale),
        grid=(m // tm,),
        in_specs=[row_spec(D_MODEL), mod_spec, mod_spec, const_spec((1, D_MODEL)),
                  pl.BlockSpec((D_MODEL, C_ZN), lambda i: (0, 0), pipeline_mode=pl.Buffered(1)),
                  pl.BlockSpec((D_MODEL, C_BG - C_ZN), lambda i: (0, 0), pipeline_mode=pl.Buffered(1)),
                  const_spec((D_MODEL, C_END - C_BG)),
                  const_spec((1, HEAD_DIM)), const_spec((3, HEAD_DIM))],
        out_specs=[spec for spec, _ in outs],
        out_shape=[shape for _, shape in outs],
        compiler_params=pltpu.CompilerParams(vmem_limit_bytes=VMEM_LIMIT),
        name="in_proj",
    )(x2d, shift, scale, norm_g.reshape(1, D_MODEL), *w_parts, q_norm_g.reshape(1, HEAD_DIM), k_norm_g)


def _softplus(z):
    return jnp.maximum(z, 0.0) + jnp.log1p(jnp.exp(-jnp.abs(z)))


def _rglru_coeffs(xc, wa_ref, ba_ref, wx_ref, bx_ref, lam_ref, a_out, b_out):
    xcb = xc.astype(BF16)
    sp = _softplus(-lam_ref[...])
    for k in range(RG_BLOCKS):
        sl = slice(k * RG_BW, (k + 1) * RG_BW)
        r = _sigmoid(_dot(xcb[:, sl], wa_ref[k]) + ba_ref[:, sl])
        i = _sigmoid(_dot(xcb[:, sl], wx_ref[k]) + bx_ref[:, sl])
        log_a = -RG_C * r * sp[:, sl]
        a = jnp.exp(log_a)
        a_out[:, sl] = a
        b_out[:, sl] = jnp.sqrt(-jnp.tanh(log_a) * (a * a + 1.0)) * i * xc[:, sl]


def _rglru_prompt_kernel(xr_ref, cw_ref, cb_ref, wa_ref, ba_ref, wx_ref, bx_ref, lam_ref,
                         ra_ref, hl_ref, ext_s, a_s, b_s, h_s):
    t_len = xr_ref.shape[0]

    @pl.when(pl.program_id(1) == 0)
    def _():
        ext_s[0:8, :] = jnp.zeros((8, D_MODEL), F32)
        h_s[...] = jnp.zeros((1, D_MODEL), F32)

    x = xr_ref[...]
    ext_s[8:8 + t_len, :] = x
    xc = ext_s[pl.ds(5, t_len), :] * cw_ref[0:1, :] + cb_ref[...]
    xc = xc + ext_s[pl.ds(6, t_len), :] * cw_ref[1:2, :]
    xc = xc + ext_s[pl.ds(7, t_len), :] * cw_ref[2:3, :]
    xc = xc + x * cw_ref[3:4, :]
    ext_s[0:8, :] = x[t_len - 8:t_len, :]
    _rglru_coeffs(xc, wa_ref, ba_ref, wx_ref, bx_ref, lam_ref, a_s, b_s)

    row = lax.broadcasted_iota(jnp.int32, (8, D_MODEL), 0)

    def scan8(r0, h):
        a = a_s[pl.ds(r0, 8), :]
        b = b_s[pl.ds(r0, 8), :]
        for s in (1, 2, 4):
            keep = row >= s
            b = jnp.where(keep, a * pltpu.roll(b, s, axis=0) + b, b)
            a = jnp.where(keep, a * pltpu.roll(a, s, axis=0), a)
        return a * h + b

    def tile(i, h):
        r0 = pl.multiple_of(i * 16, 16)
        h0 = scan8(r0, h)
        h1 = scan8(r0 + 8, h0[7:8, :])
        ra_ref[pl.ds(r0, 16), :] = jnp.concatenate([h0, h1], axis=0).astype(ra_ref.dtype)
        return h1[7:8, :]

    h = lax.fori_loop(0, t_len // 16, tile, h_s[...])
    h_s[...] = h
    hl_ref[...] = h


def _rglru_prompt(xr, bsz, seq, cw, cb, wa, ba, wx, bx, lam, t_chunk=1024):
    t_chunk = min(t_chunk, seq)
    nchunk = seq // t_chunk

    def const_spec(shape):
        return pl.BlockSpec(shape, lambda b, c: (0,) * len(shape))

    return pl.pallas_call(
        _rglru_prompt_kernel,
        grid=(bsz, nchunk),
        in_specs=[pl.BlockSpec((t_chunk, D_MODEL), lambda b, c: (b * nchunk + c, 0)),
                  const_spec((CONV_W, D_MODEL)), const_spec((1, D_MODEL)),
                  const_spec((RG_BLOCKS, RG_BW, RG_BW)), const_spec((1, D_MODEL)),
                  const_spec((RG_BLOCKS, RG_BW, RG_BW)), const_spec((1, D_MODEL)),
                  const_spec((1, D_MODEL))],
        out_specs=[pl.BlockSpec((t_chunk, D_MODEL), lambda b, c: (b * nchunk + c, 0)),
                   pl.BlockSpec((None, 1, D_MODEL), lambda b, c: (b, 0, 0))],
        out_shape=[jax.ShapeDtypeStruct((bsz * seq, D_MODEL), BF16),
                   jax.ShapeDtypeStruct((bsz, 1, D_MODEL), F32)],
        scratch_shapes=[pltpu.VMEM((t_chunk + 8, D_MODEL), F32), pltpu.VMEM((t_chunk, D_MODEL), F32),
                        pltpu.VMEM((t_chunk, D_MODEL), F32), pltpu.VMEM((1, D_MODEL), F32)],
        compiler_params=pltpu.CompilerParams(vmem_limit_bytes=VMEM_LIMIT),
        name="rglru_prompt",
    )(xr, cw, cb, wa, ba, wx, bx, lam)


def _rglru_step_kernel(xr_ref, cbuf_ref, h0_ref, cw_ref, cb_ref, wa_ref, ba_ref, wx_ref, bx_ref, lam_ref,
                       h_ref, a_s, b_s):
    xc = cbuf_ref[:, 0:D_MODEL] * cw_ref[0:1, :] + cb_ref[...]
    xc = xc + cbuf_ref[:, D_MODEL:2 * D_MODEL] * cw_ref[1:2, :]
    xc = xc + cbuf_ref[:, 2 * D_MODEL:3 * D_MODEL] * cw_ref[2:3, :]
    xc = xc + xr_ref[...] * cw_ref[3:4, :]
    _rglru_coeffs(xc, wa_ref, ba_ref, wx_ref, bx_ref, lam_ref, a_s, b_s)
    h_ref[...] = a_s[...] * h0_ref[...] + b_s[...]


def _rglru_step(xr, cbuf, h0, cw, cb, wa, ba, wx, bx, lam):
    n = xr.shape[0]
    return pl.pallas_call(
        _rglru_step_kernel,
        out_shape=jax.ShapeDtypeStruct((n, D_MODEL), F32),
        scratch_shapes=[pltpu.VMEM((n, D_MODEL), F32), pltpu.VMEM((n, D_MODEL), F32)],
        name="rglru_step",
    )(xr, cbuf, h0, cw, cb, wa, ba, wx, bx, lam)


CHUNK_ROWS = CMP_STRIDE * KV_ROWS


def _pool_weight_rows(praw_ref):
    praw = praw_ref[...]
    row = lax.broadcasted_iota(jnp.int32, praw.shape, 0)
    is_k = (row & (KV_ROWS - 1)) < N_KV_HEADS
    m_k = jnp.max(jnp.where(is_k, praw, -3e38), axis=0, keepdims=True)
    m_v = jnp.max(jnp.where(is_k, -3e38, praw), axis=0, keepdims=True)
    ex = jnp.exp(praw - jnp.where(is_k, m_k, m_v))
    s_k = jnp.sum(jnp.where(is_k, ex, 0.0), axis=0, keepdims=True) * (1.0 / N_KV_HEADS)
    s_v = jnp.sum(jnp.where(is_k, 0.0, ex), axis=0, keepdims=True) * (1.0 / N_KV_HEADS)
    return ex / jnp.where(is_k, s_k, s_v)


def _pool_rows(x, wv):
    n = x.shape[0] // CHUNK_ROWS
    x4 = x.reshape(n, CHUNK_ROWS // 8, 8, HEAD_DIM)
    w4 = wv.reshape(2, CHUNK_ROWS // 8, 8, HEAD_DIM)
    p0 = x4[:, 0] * w4[0, 0]
    p1 = x4[:, 0] * w4[1, 0]
    for v in range(1, CHUNK_ROWS // 8):
        p0 = p0 + x4[:, v] * w4[0, v]
        p1 = p1 + x4[:, v] * w4[1, v]
    return p0.reshape(n * 8, HEAD_DIM), p1.reshape(n * 8, HEAD_DIM)


def _pooled_head(p0_s, p1_s, eh, nc):
    def col(ref, start):
        return ref[pl.ds(start, nc, stride=8), :]
    return (col(p0_s, eh) + col(p0_s, eh + KV_ROWS)) + (col(p1_s, 8 + eh) + col(p1_s, 8 + eh + KV_ROWS))


def _compress_mlp(p0_s, p1_s, nc, w1_ref, w2_ref, kg_ref):
    outs = []
    for e in range(2):
        per_head = []
        for hd in range(N_KV_HEADS):
            p = _pooled_head(p0_s, p1_s, e * N_KV_HEADS + hd, nc)
            hid = _dot(p.astype(BF16), w1_ref[e])
            hid = hid * _sigmoid(hid)
            comp = p + _dot(hid.astype(BF16), w2_ref[e])
            if e == 0:
                comp = _rms_rows(comp, kg_ref[0:1, :])
            per_head.append(comp)
        outs.append(jnp.concatenate(per_head, axis=1))
    return outs[0], outs[1]


def _masked_softmax_parts(s, mask):
    sm = jnp.where(mask, s, NEG)
    p = jnp.exp(sm - jnp.max(sm, axis=-1, keepdims=True)) * mask.astype(F32)
    return p, jnp.maximum(jnp.sum(p, axis=-1, keepdims=True), 1e-30)


POOL_SLAB = 8


def _compress_prompt_kernel(x_ref, praw_ref, w1_ref, w2_ref, kg_ref, ck_ref, cvt_ref, p0_s, p1_s):
    nch = x_ref.shape[0] // CHUNK_ROWS
    wv = _pool_weight_rows(praw_ref)

    def slab(i, _):
        x = x_ref[pl.ds(pl.multiple_of(i * (POOL_SLAB * CHUNK_ROWS), POOL_SLAB * CHUNK_ROWS),
                        POOL_SLAB * CHUNK_ROWS), :]
        p0, p1 = _pool_rows(x, wv)
        rows = pl.ds(pl.multiple_of(i * (POOL_SLAB * 8), POOL_SLAB * 8), POOL_SLAB * 8)
        p0_s[rows, :] = p0
        p1_s[rows, :] = p1
        return 0

    lax.fori_loop(0, nch // POOL_SLAB, slab, 0)
    p1_s[pl.ds(nch * 8, 8), :] = jnp.zeros((8, HEAD_DIM), F32)
    ck, cv = _compress_mlp(p0_s, p1_s, nch, w1_ref, w2_ref, kg_ref)
    ck_ref[...] = ck
    cvt_ref[...] = cv.T


def _compress_prompt(kc_rows, praw, w1, w2, kg, bsz, seq):
    nch = seq // CMP_STRIDE

    def const_spec(shape):
        return pl.BlockSpec(shape, lambda b: (0,) * len(shape))

    return pl.pallas_call(
        _compress_prompt_kernel,
        grid=(bsz,),
        in_specs=[pl.BlockSpec((seq * KV_ROWS, HEAD_DIM), lambda b: (b, 0)),
                  const_spec((2 * CHUNK_ROWS, HEAD_DIM)), const_spec((2, HEAD_DIM, HEAD_DIM)),
                  const_spec((2, HEAD_DIM, HEAD_DIM)), const_spec((3, HEAD_DIM))],
        out_specs=[pl.BlockSpec((None, nch, 2 * HEAD_DIM), lambda b: (b, 0, 0)),
                   pl.BlockSpec((None, 2 * HEAD_DIM, nch), lambda b: (b, 0, 0))],
        out_shape=[jax.ShapeDtypeStruct((bsz, nch, 2 * HEAD_DIM), F32),
                   jax.ShapeDtypeStruct((bsz, 2 * HEAD_DIM, nch), F32)],
        scratch_shapes=[pltpu.VMEM(((nch + 1) * 8, HEAD_DIM), F32)] * 2,
        name="compress_prompt",
    )(kc_rows, praw, w1, w2, kg)


GQ = GROUP * Q_BLOCK
WIN_BLOCKS = WINDOW // Q_BLOCK + 1


def _bias_groups(s_t, bias):
    return jnp.concatenate([s_t[:, g * Q_BLOCK:(g + 1) * Q_BLOCK] + bias for g in range(GROUP)], axis=1)


def _fold8(x, op):
    parts = [x[i * 8:(i + 1) * 8, :] for i in range(x.shape[0] // 8)]
    while len(parts) > 1:
        parts = [op(parts[i], parts[i + 1]) for i in range(0, len(parts), 2)]
    return parts[0]


def _finish_t(o_t, m, l8):
    den = jnp.maximum(jnp.sum(l8, axis=0, keepdims=True), 1e-30)
    return jnp.where(m > 0.5 * NEG, o_t / den, 0.0)


def _nsa_prompt_kernel(q_ref, ck_ref, cvt_ref, sk_ref, svt_ref, wk_ref, wvt_ref, g_ref, o_ref,
                       bias_s, ps_s, pw_s):
    qi = pl.program_id(1)
    nq = Q_BLOCK
    ncp = ck_ref.shape[0]
    nkb = svt_ref.shape[0]
    nsel = nkb * (Q_BLOCK // SEL_BLOCK)
    n_top = min(TOP_N, nsel)
    start = qi * nq
    t_row = start + lax.broadcasted_iota(jnp.int32, (1, nq), 1)
    gates_t = g_ref[...].T

    n_sub = lax.broadcasted_iota(jnp.int32, (ncp, nq), 0)
    cmp_mask = (n_sub * CMP_STRIDE + (CMP_BLOCK - 1) <= t_row) & (n_sub < ncp - 1)
    jj = lax.broadcasted_iota(jnp.int32, (nsel, ncp), 0)
    nn = lax.broadcasted_iota(jnp.int32, (nsel, ncp), 1)
    ov_t = ((nn * CMP_STRIDE <= jj * SEL_BLOCK + SEL_BLOCK - 1)
            & (nn * CMP_STRIDE + CMP_BLOCK - 1 >= jj * SEL_BLOCK)
            & (nn < ncp - 1)).astype(F32)
    j_col = lax.broadcasted_iota(jnp.int32, (nsel, nq), 0)
    cur = _shr(t_row, SEL_SHIFT)
    valid_t = j_col <= cur
    forced_t = (j_col == 0) | (valid_t & (j_col > cur - N_LOCAL_BLOCKS))
    force_add = FORCE * forced_t.astype(F32)
    key_sub = lax.broadcasted_iota(jnp.int32, (Q_BLOCK, nq), 0)
    q_lane = lax.broadcasted_iota(jnp.int32, (Q_BLOCK, nq), 1)
    win_bias = {0: jnp.where(q_lane < key_sub, 0.0, NEG), WIN_BLOCKS - 1: jnp.where(q_lane >= key_sub, 0.0, NEG)}

    for h in range(N_KV_HEADS):
        qh = jnp.concatenate(
            [q_ref[:, (h * GROUP + g) * HEAD_DIM:(h * GROUP + g + 1) * HEAD_DIM] for g in range(GROUP)], axis=0)
        hs = slice(h * HEAD_DIM, (h + 1) * HEAD_DIM)

        s_c = _dot_nt(ck_ref[:, hs].astype(BF16), qh)
        p_groups = []
        for g in range(GROUP):
            sm = jnp.where(cmp_mask, s_c[:, g * nq:(g + 1) * nq], NEG)
            p = jnp.exp2(sm - jnp.max(sm, axis=0, keepdims=True)) * cmp_mask.astype(F32)
            p_groups.append(p / jnp.maximum(jnp.sum(p, axis=0, keepdims=True), 1e-30))
        o_c = _dot(cvt_ref[hs, :].astype(BF16), jnp.concatenate(p_groups, axis=1).astype(BF16))
        psum = p_groups[0] + p_groups[1] + p_groups[2] + p_groups[3]
        imp_t = jnp.dot(ov_t, psum, precision=lax.Precision.HIGHEST, preferred_element_type=F32)
        score_t = jnp.where(valid_t, imp_t + force_add, NEG)
        rank = jnp.zeros((nsel, nq), F32)
        for k in range(nsel):
            rk = score_t[k:k + 1, :]
            tie = jnp.where(j_col > k, 1.0, 0.0)
            rank = rank + jnp.where(rk > score_t, 1.0, jnp.where(rk == score_t, tie, 0.0))
        sel_f = jnp.where(rank < n_top, jnp.where(score_t > 0.5 * NEG, 1.0, 0.0), 0.0)

        def sel_branch(nblk):
            def run():
                for kb in range(nblk):
                    lo = sel_f[2 * kb:2 * kb + 1, :]
                    hi = sel_f[2 * kb + 1:2 * kb + 2, :]
                    picked = jnp.where(key_sub < SEL_BLOCK, lo, hi) > 0.5
                    bias_s[kb] = jnp.where(picked & (kb * Q_BLOCK + key_sub <= t_row), 0.0, NEG)

                def scores(kb):
                    return _bias_groups(_dot_nt(sk_ref[kb * Q_BLOCK:(kb + 1) * Q_BLOCK, hs], qh), bias_s[kb])

                m8 = jnp.full((8, GQ), NEG, F32)
                for kb in range(nblk):
                    m8 = jnp.maximum(m8, _fold8(scores(kb), jnp.maximum))
                m_s = jnp.max(m8, axis=0, keepdims=True)
                l_s = jnp.zeros((8, GQ), F32)
                for kb in range(nblk):
                    p = jnp.exp2(scores(kb) - m_s)
                    ps_s[kb * Q_BLOCK:(kb + 1) * Q_BLOCK, :] = p.astype(BF16)
                    l_s = l_s + _fold8(p, jnp.add)
                v_sel = jnp.concatenate([svt_ref[kb, hs, :] for kb in range(nblk)], axis=1)
                return _finish_t(_dot(v_sel, ps_s[0:nblk * Q_BLOCK, :]), m_s, l_s)
            return run

        quarter = nkb // 4
        o_s = lax.switch(qi // quarter, [sel_branch((v + 1) * quarter) for v in range(4)])

        def win_scores(r):
            kb = qi - (WIN_BLOCKS - 1) + r
            kbc = jnp.maximum(kb, 0)
            off = pl.multiple_of(kbc * Q_BLOCK, Q_BLOCK)
            s_t = _dot_nt(wk_ref[pl.ds(off, Q_BLOCK), hs], qh) + jnp.where(kb < 0, NEG, 0.0)
            if r in win_bias:
                s_t = _bias_groups(s_t, win_bias[r])
            return s_t, kbc

        m8 = jnp.full((8, GQ), NEG, F32)
        for r in range(WIN_BLOCKS):
            m8 = jnp.maximum(m8, _fold8(win_scores(r)[0], jnp.maximum))
        m_w = jnp.max(m8, axis=0, keepdims=True)
        l_w = jnp.zeros((8, GQ), F32)
        v_blocks = []
        for r in range(WIN_BLOCKS):
            s_t, kbc = win_scores(r)
            p = jnp.exp2(s_t - m_w)
            pw_s[r * Q_BLOCK:(r + 1) * Q_BLOCK, :] = p.astype(BF16)
            l_w = l_w + _fold8(p, jnp.add)
            v_blocks.append(wvt_ref[kbc, hs, :])
        o_w = _finish_t(_dot(jnp.concatenate(v_blocks, axis=1), pw_s[...]), m_w, l_w)

        for g in range(GROUP):
            head = h * GROUP + g
            cs = slice(g * nq, (g + 1) * nq)
            o = (o_c[:, cs] * gates_t[head:head + 1, :]
                 + o_s[:, cs] * gates_t[N_HEADS + head:N_HEADS + head + 1, :]
                 + o_w[:, cs] * gates_t[2 * N_HEADS + head:2 * N_HEADS + head + 1, :])
            o_ref[:, head * HEAD_DIM:(head + 1) * HEAD_DIM] = o.T.astype(o_ref.dtype)


def _nsa_prompt(q, ck, cvt, sk, svt, wk, wvt, gates, bsz, seq):
    nqb = seq // Q_BLOCK
    ncp = ck.shape[1]
    kv_heads_w = N_KV_HEADS * HEAD_DIM
    k_spec = pl.BlockSpec((seq, kv_heads_w), lambda b, i: (b, 0))
    vt_spec = pl.BlockSpec((nqb, kv_heads_w, Q_BLOCK), lambda b, i: (b, 0, 0))
    return pl.pallas_call(
        _nsa_prompt_kernel,
        grid=(bsz, nqb),
        in_specs=[pl.BlockSpec((Q_BLOCK, D_MODEL), lambda b, i: (b * nqb + i, 0)),
                  pl.BlockSpec((None, ncp, kv_heads_w), lambda b, i: (b, 0, 0)),
                  pl.BlockSpec((None, kv_heads_w, ncp), lambda b, i: (b, 0, 0)),
                  k_spec, vt_spec, k_spec, vt_spec,
                  pl.BlockSpec((Q_BLOCK, 128), lambda b, i: (b * nqb + i, 0))],
        out_specs=pl.BlockSpec((Q_BLOCK, D_MODEL), lambda b, i: (b * nqb + i, 0)),
        out_shape=jax.ShapeDtypeStruct((bsz * seq, D_MODEL), BF16),
        scratch_shapes=[pltpu.VMEM((nqb, Q_BLOCK, Q_BLOCK), F32),
                        pltpu.VMEM((seq, GQ), BF16),
                        pltpu.VMEM((WIN_BLOCKS * Q_BLOCK, GQ), BF16)],
        compiler_params=pltpu.CompilerParams(vmem_limit_bytes=VMEM_LIMIT),
        name="nsa_prompt",
    )(q, ck, cvt, sk, svt, wk, wvt, gates)


def _nsa_sample_cmp_kernel(pt_ref, *refs, past, n_steps, pps):
    page_refs = refs[:pps]
    (kcn_ref, praw_ref, w1_ref, w2_ref, kg_ref, q_ref, oc_ref, imp_ref, wv_s, p0_s, p1_s) = refs[pps:]
    del pt_ref
    g = pl.program_id(1)
    chunks_per_page = PAGE_SIZE // CMP_STRIDE
    nchunk = past // CMP_STRIDE
    nsel = past // SEL_BLOCK + 1
    t_now = past

    @pl.when(g == 0)
    def _():
        wv_s[...] = _pool_weight_rows(praw_ref)

    wv = wv_s[...]
    for k in range(pps):
        p0, p1 = _pool_rows(page_refs[k][...], wv)
        rows = pl.ds(pl.multiple_of((g * pps + k) * (chunks_per_page * 8), chunks_per_page * 8),
                     chunks_per_page * 8)
        p0_s[rows, :] = p0
        p1_s[rows, :] = p1

    @pl.when(g == n_steps - 1)
    def _():
        p1_s[pl.ds(nchunk * 8, KV_ROWS), :] = kcn_ref[...] * wv_s[pl.ds(CHUNK_ROWS, KV_ROWS), :]
        p1_s[pl.ds(nchunk * 8 + KV_ROWS, KV_ROWS), :] = jnp.zeros((KV_ROWS, HEAD_DIM), F32)
        ck, cv = _compress_mlp(p0_s, p1_s, nchunk, w1_ref, w2_ref, kg_ref)
        q8 = q_ref[...]
        row = lax.broadcasted_iota(jnp.int32, (N_HEADS, 1), 0)
        first = row < GROUP
        n_row = lax.broadcasted_iota(jnp.int32, (1, nchunk), 1)
        mask = jnp.broadcast_to(n_row * CMP_STRIDE + (CMP_BLOCK - 1) <= t_now, (N_HEADS, nchunk))
        ckb = ck.astype(BF16)
        cvb = cv.astype(BF16)
        s = jnp.where(first, _dot_nt(q8, ckb[:, 0:HEAD_DIM]), _dot_nt(q8, ckb[:, HEAD_DIM:2 * HEAD_DIM]))
        p, den = _masked_softmax_parts(s, mask)
        p = p / den
        pb = p.astype(BF16)
        oc_ref[...] = jnp.where(first, _dot(pb, cvb[:, 0:HEAD_DIM]), _dot(pb, cvb[:, HEAD_DIM:2 * HEAD_DIM]))
        nn = lax.broadcasted_iota(jnp.int32, (nchunk, 256), 0)
        jj = lax.broadcasted_iota(jnp.int32, (nchunk, 256), 1)
        ov = ((nn * CMP_STRIDE <= jj * SEL_BLOCK + SEL_BLOCK - 1)
              & (nn * CMP_STRIDE + CMP_BLOCK - 1 >= jj * SEL_BLOCK) & (jj < nsel)).astype(F32)
        imp8 = jnp.dot(p, ov, precision=lax.Precision.HIGHEST, preferred_element_type=F32)
        imp_ref[0:1, :] = jnp.sum(jnp.where(first, imp8, 0.0), axis=0, keepdims=True)
        imp_ref[1:2, :] = jnp.sum(jnp.where(first, 0.0, imp8), axis=0, keepdims=True)


def _nsa_sample_cmp(page_table, cache_cmp, kc_new, pwt, w1, w2, kg, q8):
    db, n_pages = page_table.shape
    past = n_pages * PAGE_SIZE
    pps = min(PAGES_PER_STEP, n_pages)
    n_steps = n_pages // pps
    pages = cache_cmp.reshape(cache_cmp.shape[0], PAGE_SIZE * KV_ROWS, HEAD_DIM)

    def page_spec(k):
        return pl.BlockSpec((None, PAGE_SIZE * KV_ROWS, HEAD_DIM),
                            lambda b, g, pt: (pt[b * n_pages + g * pps + k], 0, 0))

    def const_spec(shape):
        return pl.BlockSpec(shape, lambda b, g, pt: (0,) * len(shape))

    nchunk = past // CMP_STRIDE
    grid_spec = pltpu.PrefetchScalarGridSpec(
        num_scalar_prefetch=1,
        grid=(db, n_steps),
        in_specs=[page_spec(k) for k in range(pps)] + [
            pl.BlockSpec((None, KV_ROWS, HEAD_DIM), lambda b, g, pt: (b, 0, 0)),
            const_spec((2 * CHUNK_ROWS, HEAD_DIM)), const_spec((2, HEAD_DIM, HEAD_DIM)),
            const_spec((2, HEAD_DIM, HEAD_DIM)), const_spec((3, HEAD_DIM)),
            pl.BlockSpec((None, N_HEADS, HEAD_DIM), lambda b, g, pt: (b, 0, 0))],
        out_specs=[pl.BlockSpec((None, N_HEADS, HEAD_DIM), lambda b, g, pt: (b, 0, 0)),
                   pl.BlockSpec((None, 2, 256), lambda b, g, pt: (b, 0, 0))],
        scratch_shapes=[pltpu.VMEM((2 * CHUNK_ROWS, HEAD_DIM), F32),
                        pltpu.VMEM(((nchunk + 1) * 8, HEAD_DIM), F32),
                        pltpu.VMEM(((nchunk + 1) * 8, HEAD_DIM), F32)])
    return pl.pallas_call(
        functools.partial(_nsa_sample_cmp_kernel, past=past, n_steps=n_steps, pps=pps),
        grid_spec=grid_spec,
        out_shape=[jax.ShapeDtypeStruct((db, N_HEADS, HEAD_DIM), F32),
                   jax.ShapeDtypeStruct((db, 2, 256), F32)],
        compiler_params=pltpu.CompilerParams(vmem_limit_bytes=VMEM_LIMIT),
        name="nsa_sample_cmp",
    )(page_table.reshape(-1), *([pages] * pps), kc_new, pwt, w1, w2, kg, q8)


def _select_kernel(imp_ref, idx_ref, *, t_now, nsel):
    imp = imp_ref[...]
    rows, width = imp.shape
    j = lax.broadcasted_iota(jnp.int32, (rows, width), 1)
    jf = j.astype(F32)
    cur = t_now // SEL_BLOCK
    valid = (j <= cur) & (j < nsel)
    forced = (j == 0) | (valid & (j > cur - N_LOCAL_BLOCKS))
    score = jnp.where(valid, imp + FORCE * forced.astype(F32), NEG)
    col = lax.broadcasted_iota(jnp.int32, (rows, 128), 1)
    out = jnp.full((rows, 128), -1, jnp.int32)
    for it in range(min(TOP_N, nsel)):
        m = jnp.max(score, axis=-1, keepdims=True)
        idx = jnp.min(jnp.where(score == m, jf, 1e9), axis=-1, keepdims=True)
        out = jnp.where(col == it, jnp.where(m > 0.5 * NEG, idx.astype(jnp.int32), -1), out)
        score = jnp.where(jf == idx, -3e38, score)
    idx_ref[...] = out


def _select(imp2d, t_now, nsel):
    rows = imp2d.shape[0]
    return pl.pallas_call(
        functools.partial(_select_kernel, t_now=t_now, nsel=nsel),
        out_shape=jax.ShapeDtypeStruct((rows, 128), jnp.int32),
        name="nsa_sample_select",
    )(imp2d)


def _nsa_sample_sel_kernel(idx_ref, pt_ref, *refs, n_past_blk, n_top, nseq):
    per_seq = N_KV_HEADS * n_top
    q_ref, ksn_ref, os_ref = refs[nseq * per_seq:]
    del pt_ref
    for j in range(nseq):
        _sel_one_sequence(idx_ref, pl.program_id(0) * nseq + j, refs[j * per_seq:(j + 1) * per_seq],
                          q_ref.at[j], ksn_ref.at[j], os_ref.at[j], n_past_blk, n_top)


def _sel_one_sequence(idx_ref, b, blk_refs, q_ref, ksn_ref, os_ref, n_past_blk, n_top):
    q8 = q_ref[...]
    q8f = q8.astype(F32)
    ksn = ksn_ref[...]
    blk_rows = SEL_BLOCK * KV_ROWS
    nrows = n_top * blk_rows
    lane = lax.broadcasted_iota(jnp.int32, (1, nrows), 1)
    row = lax.broadcasted_iota(jnp.int32, (N_HEADS, 1), 0)
    first = row < GROUP
    per_head = []
    for h in range(N_KV_HEADS):
        rows_b = jnp.concatenate([blk_refs[h * n_top + n][...] for n in range(n_top)], axis=0).astype(BF16)
        k_new = ksn[h:h + 1, :]
        v_new = ksn[N_KV_HEADS + h:N_KV_HEADS + h + 1, :]
        slot_ok = jnp.zeros((1, nrows), F32)
        new_ok = jnp.zeros((1, 1), F32)
        for n in range(n_top):
            ix = idx_ref[(b * N_KV_HEADS + h) * n_top + n]
            past_ok = jnp.where((ix >= 0) & (ix < n_past_blk), 1.0, 0.0)
            slot_ok = jnp.where((lane >= n * blk_rows) & (lane < (n + 1) * blk_rows), past_ok, slot_ok)
            new_ok = jnp.maximum(new_ok, jnp.where(ix >= n_past_blk, 1.0, 0.0))
        mask = jnp.broadcast_to((slot_ok > 0.5) & ((lane & (KV_ROWS - 1)) == h), (N_HEADS, nrows))
        new_mask = jnp.broadcast_to(new_ok > 0.5, (N_HEADS, 1))
        s = jnp.where(mask, _dot_nt(q8, rows_b), NEG)
        s_new = jnp.where(new_mask, jnp.sum(q8f * k_new, axis=-1, keepdims=True), NEG)
        mx = jnp.maximum(jnp.max(s, axis=-1, keepdims=True), s_new)
        p = jnp.exp(s - mx) * mask.astype(F32)
        p_new = jnp.exp(s_new - mx) * new_mask.astype(F32)
        den = jnp.maximum(jnp.sum(p, axis=-1, keepdims=True) + p_new, 1e-30)
        p_on_v = pltpu.roll(p, N_KV_HEADS, axis=1).astype(BF16)
        per_head.append((_dot(p_on_v, rows_b) + p_new * v_new) / den)
    os_ref[...] = jnp.where(first, per_head[0], per_head[1])


def _nsa_sample_sel(sel_idx, page_table, cache_sel, q8, ks_new):
    db, n_pages = page_table.shape
    n_past_blk = n_pages * PAGE_SIZE // SEL_BLOCK
    n_top = sel_idx.shape[-1]
    halves = PAGE_SIZE // SEL_BLOCK
    blocks = cache_sel.reshape(cache_sel.shape[0] * halves, SEL_BLOCK * KV_ROWS, HEAD_DIM)
    blk = jnp.clip(sel_idx.reshape(db, N_KV_HEADS * n_top), 0, n_past_blk - 1)
    phys = jnp.take_along_axis(page_table, blk // halves, axis=1) * halves + blk % halves

    nseq = SEL_SEQS_PER_STEP if db % SEL_SEQS_PER_STEP == 0 else 1

    def blk_spec(j, h, n):
        return pl.BlockSpec((None, SEL_BLOCK * KV_ROWS, HEAD_DIM),
                            lambda b, idx, pb: (pb[((b * nseq + j) * N_KV_HEADS + h) * n_top + n], 0, 0))

    grid_spec = pltpu.PrefetchScalarGridSpec(
        num_scalar_prefetch=2,
        grid=(db // nseq,),
        in_specs=[blk_spec(j, h, n) for j in range(nseq) for h in range(N_KV_HEADS) for n in range(n_top)] + [
            pl.BlockSpec((nseq, N_HEADS, HEAD_DIM), lambda b, idx, pt: (b, 0, 0)),
            pl.BlockSpec((nseq, KV_ROWS, HEAD_DIM), lambda b, idx, pt: (b, 0, 0))],
        out_specs=pl.BlockSpec((nseq, N_HEADS, HEAD_DIM), lambda b, idx, pt: (b, 0, 0)))
    return pl.pallas_call(
        functools.partial(_nsa_sample_sel_kernel, n_past_blk=n_past_blk, n_top=n_top, nseq=nseq),
        grid_spec=grid_spec,
        out_shape=jax.ShapeDtypeStruct((db, N_HEADS, HEAD_DIM), F32),
        compiler_params=pltpu.CompilerParams(vmem_limit_bytes=VMEM_LIMIT),
        name="nsa_sample_sel",
    )(sel_idx.reshape(-1), phys.reshape(-1), *([blocks] * (nseq * N_KV_HEADS * n_top)), q8, ks_new)


def _nsa_sample_win_kernel(win_ref, kwn_ref, q_ref, wout_ref, ow_ref, *, past):
    for j in range(win_ref.shape[0]):
        _win_one_sequence(win_ref.at[j], kwn_ref.at[j], q_ref.at[j], wout_ref.at[j], ow_ref.at[j], past)


def _win_one_sequence(win_ref, kwn_ref, q_ref, wout_ref, ow_ref, past):
    nrows = win_ref.shape[0]
    wb = nrows // KV_ROWS
    kwn = kwn_ref[...]
    x = win_ref[...]
    wout_ref[...] = pltpu.roll(x, nrows - KV_ROWS, axis=0)
    wout_ref[pl.ds(nrows - KV_ROWS, KV_ROWS), :] = kwn
    q8 = q_ref[...]
    q8f = q8.astype(F32)
    t_now = past
    lane = lax.broadcasted_iota(jnp.int32, (N_HEADS, nrows), 1)
    row = lax.broadcasted_iota(jnp.int32, (N_HEADS, nrows), 0)
    pos = past - wb + _shr(lane, KV_ROWS.bit_length() - 1)
    d = t_now - pos
    mask = ((d >= 0) & (d < WINDOW) & (pos >= 0)
            & ((lane & (KV_ROWS - 1)) == _shr(row, GROUP.bit_length() - 1)))
    first = lax.broadcasted_iota(jnp.int32, (N_HEADS, 1), 0) < GROUP
    k_new = jnp.where(first, kwn[0:1, :], kwn[1:2, :])
    v_new = jnp.where(first, kwn[N_KV_HEADS:N_KV_HEADS + 1, :], kwn[N_KV_HEADS + 1:N_KV_HEADS + 2, :])
    xb = x.astype(BF16)
    s = jnp.where(mask, _dot_nt(q8, xb), NEG)
    s_new = jnp.sum(q8f * k_new, axis=-1, keepdims=True)
    mx = jnp.maximum(jnp.max(s, axis=-1, keepdims=True), s_new)
    p = jnp.exp(s - mx) * mask.astype(F32)
    p_new = jnp.exp(s_new - mx)
    den = jnp.maximum(jnp.sum(p, axis=-1, keepdims=True) + p_new, 1e-30)
    p_on_v = pltpu.roll(p, N_KV_HEADS, axis=1).astype(BF16)
    ow_ref[...] = (_dot(p_on_v, xb) + p_new * v_new) / den


def _nsa_sample_win(win_rows, kw_new, q8, past):
    db, nrows, _ = win_rows.shape
    nseq = WIN_SEQS_PER_STEP if db % WIN_SEQS_PER_STEP == 0 else 1
    return pl.pallas_call(
        functools.partial(_nsa_sample_win_kernel, past=past),
        grid=(db // nseq,),
        in_specs=[pl.BlockSpec((nseq, nrows, HEAD_DIM), lambda b: (b, 0, 0)),
                  pl.BlockSpec((nseq, KV_ROWS, HEAD_DIM), lambda b: (b, 0, 0)),
                  pl.BlockSpec((nseq, N_HEADS, HEAD_DIM), lambda b: (b, 0, 0))],
        out_specs=[pl.BlockSpec((nseq, nrows, HEAD_DIM), lambda b: (b, 0, 0)),
                   pl.BlockSpec((nseq, N_HEADS, HEAD_DIM), lambda b: (b, 0, 0))],
        out_shape=[jax.ShapeDtypeStruct((db, nrows, HEAD_DIM), F32),
                   jax.ShapeDtypeStruct((db, N_HEADS, HEAD_DIM), F32)],
        name="nsa_sample_win",
    )(win_rows, kw_new, q8)


def _gate_mix_kernel(oc_ref, os_ref, ow_ref, g_ref, o_ref):
    g = g_ref[...]
    c = lax.broadcasted_iota(jnp.int32, (128, D_MODEL), 0)
    head = _shr(lax.broadcasted_iota(jnp.int32, (128, D_MODEL), 1), HEAD_SHIFT)
    acc = jnp.zeros(o_ref.shape, F32)
    for r, ref in enumerate((oc_ref, os_ref, ow_ref)):
        expand = (c == r * N_HEADS + head).astype(F32)
        acc = acc + ref[...] * jnp.dot(g, expand, precision=lax.Precision.HIGHEST, preferred_element_type=F32)
    o_ref[...] = acc


def _merge_kernel(x_ref, a_ref, ra_ref, b_ref, on_ref, gate_ref, w_ref, y_ref):
    f32 = lambda ref: ref[...].astype(F32)
    u = f32(a_ref) * f32(ra_ref) + f32(b_ref) * f32(on_ref)
    y_ref[...] = x_ref[...] + gate_ref[0] * _dot(u.astype(BF16), w_ref[...])


def _merge(x2d, a, ra, b, o_nsa, gate, w_out, tm, rows_per_mod):
    m = x2d.shape[0]
    tiles_per_mod = rows_per_mod // tm
    row_spec = pl.BlockSpec((tm, D_MODEL), lambda i: (i, 0))
    return pl.pallas_call(
        _merge_kernel,
        grid=(m // tm,),
        in_specs=[row_spec] * 5 + [
            pl.BlockSpec((1, gate.shape[1], D_MODEL), lambda i: (i // tiles_per_mod, 0, 0)),
            pl.BlockSpec((D_MODEL, D_MODEL), lambda i: (0, 0))],
        out_specs=row_spec,
        out_shape=jax.ShapeDtypeStruct((m, D_MODEL), F32),
        compiler_params=pltpu.CompilerParams(vmem_limit_bytes=VMEM_LIMIT),
        name="merge_out_proj",
    )(x2d, a, ra, b, o_nsa, gate, w_out)


def _split_w_in(w_in):
    n_bg = 3 * N_HEADS
    lo = w_in[:, :C_ZN].astype(BF16)
    hi = w_in[:, C_ZN + n_bg:].astype(BF16)
    bg = jnp.pad(w_in[:, C_ZN:C_ZN + n_bg], ((0, 0), (0, C_END - C_BG - n_bg))).astype(BF16)
    return lo, hi, bg


def kernel(x_prompt, x_sample, c_prompt, c_sample, state_conv, state_rglru, cache_cmp_kv, cache_sel_kv,
           state_win_kv, page_table, norm_g, w_ada, b_ada, w_in, conv_w, conv_b, rg_wa, rg_ba, rg_wx, rg_bx,
           rg_lambda, q_norm_g, k_norm_g, cmp_pool_w, cmp_w1, cmp_w2, w_out):
    depth = norm_g.shape[0]
    assert depth == 1 and x_sample.shape[1] == 1
    bsz, seq, _ = x_prompt.shape
    db = x_sample.shape[0]
    n_pages = page_table.shape[1]
    past = n_pages * PAGE_SIZE
    layer = 0

    w_cat = _split_w_in(w_in[layer])
    w_out_b = w_out[layer].astype(BF16)
    wa_b = rg_wa[layer].astype(BF16)
    wx_b = rg_wx[layer].astype(BF16)
    w1_b = cmp_w1[layer].astype(BF16)
    w2_b = cmp_w2[layer].astype(BF16)
    row = lambda v: v.reshape(1, -1)
    rg_args = (conv_w[layer], row(conv_b[layer]), wa_b, row(rg_ba[layer]), wx_b, row(rg_bx[layer]),
               row(rg_lambda[layer]))
    praw = jnp.broadcast_to(
        cmp_pool_w[layer].reshape(2, 2, CMP_STRIDE).transpose(1, 2, 0)[:, :, :, None, None],
        (2, CMP_STRIDE, 2, N_KV_HEADS, HEAD_DIM)).reshape(2 * CHUNK_ROWS, HEAD_DIM)
    kg = k_norm_g[layer]

    mod = _modulation(jnp.concatenate([c_prompt, c_sample], axis=0), w_ada[layer], b_ada[layer])
    shift, scale, gate = mod[:, :D_MODEL], mod[:, D_MODEL:2 * D_MODEL], mod[:, 2 * D_MODEL:]

    xp2 = x_prompt.reshape(bsz * seq, D_MODEL)
    pm = lambda v: v[:bsz].reshape(bsz, 1, D_MODEL)
    (xr_p, a_p, b_p, q_p, kc_p, ks_p, kw_p, skb_p, svt_p, wkb_p, wvt_p, g_p) = _project(
        xp2, pm(shift), pm(scale), norm_g[layer], w_cat, q_norm_g[layer], kg, tm=256, rows_per_mod=seq,
        q_scale=HEAD_DIM ** -0.5 * LOG2E)
    ra_p, h_p = _rglru_prompt(xr_p, bsz, seq, *rg_args)
    ck_p, cvt_p = _compress_prompt(kc_p, praw, w1_b, w2_b, kg, bsz, seq)
    on_p = _nsa_prompt(q_p, ck_p, cvt_p, skb_p, svt_p, wkb_p, wvt_p, g_p, bsz, seq)
    y_p = _merge(xp2, a_p, ra_p, b_p, on_p, pm(gate), w_out_b, tm=512, rows_per_mod=seq)

    xs2 = x_sample.reshape(db, D_MODEL)
    sm = lambda v: v[bsz:].reshape(1, db, D_MODEL)
    (xr_s, a_s, b_s, q_s, kc_s, ks_s, kw_s, _, _, _, _, g_s) = _project(
        xs2, sm(shift), sm(scale), norm_g[layer], w_cat, q_norm_g[layer], kg, tm=db, rows_per_mod=db,
        q_scale=HEAD_DIM ** -0.5)
    h_s = _rglru_step(xr_s, state_conv[layer].reshape(db, (CONV_W - 1) * D_MODEL), state_rglru[layer], *rg_args)
    q8 = q_s.reshape(db, N_HEADS, HEAD_DIM)
    new_rows = lambda v: v.reshape(db, KV_ROWS, HEAD_DIM)
    n_phys = cache_cmp_kv.shape[1]
    oc_s, imp = _nsa_sample_cmp(page_table, cache_cmp_kv.reshape(depth * n_phys, PAGE_SIZE * KV_ROWS, HEAD_DIM),
                                new_rows(kc_s), praw, w1_b, w2_b, kg, q8)
    nsel = past // SEL_BLOCK + 1
    sel_idx = _select(imp.reshape(db * N_KV_HEADS, 256), past, nsel)[:, :min(TOP_N, nsel)]
    os_s = _nsa_sample_sel(sel_idx, page_table,
                           cache_sel_kv.reshape(depth * n_phys, PAGE_SIZE * KV_ROWS, HEAD_DIM), q8, new_rows(ks_s))
    wb = state_win_kv.shape[2]
    win_s, ow_s = _nsa_sample_win(state_win_kv.reshape(depth * db, wb * KV_ROWS, HEAD_DIM), new_rows(kw_s), q8, past)
    on_s = pl.pallas_call(
        _gate_mix_kernel, out_shape=jax.ShapeDtypeStruct((db, D_MODEL), F32), name="nsa_sample_mix",
    )(oc_s.reshape(db, D_MODEL), os_s.reshape(db, D_MODEL), ow_s.reshape(db, D_MODEL), g_s)
    y_s = _merge(xs2, a_s, h_s, b_s, on_s, sm(gate), w_out_b, tm=db, rows_per_mod=db)

    kv_shape = (2, N_KV_HEADS, HEAD_DIM)
    xr_p3 = xr_p.reshape(bsz, seq, D_MODEL)
    conv_prompt = xr_p3[:, seq - (CONV_W - 1):][None]
    conv_sample = jnp.concatenate([state_conv[layer][:, 1:], xr_s[:, None, :]], axis=1)[None]
    win_len = min(WINDOW, seq)
    return (y_p.reshape(bsz, seq, D_MODEL), y_s.reshape(db, 1, D_MODEL),
            conv_prompt, conv_sample,
            h_p.reshape(1, bsz, D_MODEL), h_s.reshape(1, db, D_MODEL),
            kc_p.reshape(1, bsz, seq, *kv_shape), kc_s.reshape(1, db, 1, *kv_shape),
            ks_p.reshape(1, bsz, seq, *kv_shape), ks_s.reshape(1, db, 1, *kv_shape),
            kw_p.reshape(bsz, seq * KV_ROWS, HEAD_DIM)[:, (seq - win_len) * KV_ROWS:].reshape(
                1, bsz, win_len, *kv_shape),
            win_s.reshape(1, db, wb, *kv_shape))
```
